```python
import jax
import jax.numpy as jnp
from jax import lax
import numpy as np

D_MODEL = 2048
BATCH = 4
SEQ = 8192
DEPTH = 1
DEC_BATCH = 16
DEC_SEQ = 64
PAST_LEN = 2048

CHUNK = 64
QBLOCK = 128
HEAD_DIM = 64
SB_HEADS = D_MODEL // (2 * HEAD_DIM)
SB_WIDTH = SB_HEADS * HEAD_DIM
RW_HEADS = D_MODEL // (2 * HEAD_DIM)
RW_WIDTH = RW_HEADS * HEAD_DIM
RW_DECAY_LORA = max(32, int(round(RW_WIDTH ** 0.5 * 1.8 / 32)) * 32)
RW_AAA_LORA = max(32, int(round(RW_WIDTH ** 0.5 * 1.8 / 32)) * 32)
RW_GATE_LORA = max(32, int(round(RW_WIDTH ** 0.8 * 0.6 / 32)) * 32)
RW_COLS = 3 * RW_WIDTH + RW_DECAY_LORA + RW_AAA_LORA + RW_GATE_LORA
N_IN = 3 * SB_WIDTH + RW_COLS + 2 * D_MODEL
IN_SPLITS = [SB_WIDTH, 2 * SB_WIDTH, 3 * SB_WIDTH, 3 * SB_WIDTH + RW_COLS, 3 * SB_WIDTH + RW_COLS + D_MODEL]
RW_SPLITS = [RW_WIDTH, 2 * RW_WIDTH, 3 * RW_WIDTH, 3 * RW_WIDTH + RW_DECAY_LORA,
             3 * RW_WIDTH + RW_DECAY_LORA + RW_AAA_LORA]
N_GROUPS = 4
EXPERTS_PER_GROUP = 8
N_EXPERTS = N_GROUPS * EXPERTS_PER_GROUP
TOP_K = 2
D_EXPERT = D_MODEL // 2
MOE_BLOCK = 128
LN_EPS = 1e-5
GN_EPS = 64e-5
DEEPNORM_ALPHA = (2 * DEPTH) ** 0.25
DEEPNORM_BETA = (8 * DEPTH) ** -0.25

kernel_name = 'streaming_sb_rwkv7_hmoe_encoder_step'


def layer_norm(x, g, b):
    xf = x.astype(jnp.float32)
    mu = jnp.mean(xf, axis=-1, keepdims=True)
    var = jnp.mean(jnp.square(xf - mu), axis=-1, keepdims=True)
    return ((xf - mu) * lax.rsqrt(var + LN_EPS) * g + b).astype(x.dtype)


def head_group_norm(y, w, b):
    mu = jnp.mean(y, axis=-1, keepdims=True)
    var = jnp.mean(jnp.square(y - mu), axis=-1, keepdims=True)
    return (y - mu) * lax.rsqrt(var + GN_EPS) * w + b


def stick_breaking_attention(q, k, v, past_len):
    B, T, H, Dh = q.shape
    S = k.shape[1]
    n_blocks = -(-T // QBLOCK)
    Tp = n_blocks * QBLOCK
    qp = jnp.pad(q, ((0, 0), (0, Tp - T), (0, 0), (0, 0)))
    q_blocks = jnp.moveaxis(qp.reshape(B, n_blocks, QBLOCK, H, Dh), 1, 0)
    q_pos = (past_len + jnp.arange(Tp, dtype=jnp.int32)).reshape(n_blocks, QBLOCK)
    k_pos = jnp.arange(S, dtype=jnp.int32)
    scale = Dh ** -0.5

    def block(args):
        qb, pos = args
        z = jnp.einsum('bqhd,bshd->bhqs', qb, k).astype(jnp.float32) * scale
        mask = k_pos[None, :] < pos[:, None]
        log_keep = jnp.where(mask, -jax.nn.softplus(z), 0.0)
        tail = lax.cumsum(log_keep, axis=3, reverse=True)
        after = jnp.concatenate([tail[..., 1:], jnp.zeros_like(tail[..., :1])], axis=3)
        w = jnp.where(mask, jnp.exp(jax.nn.log_sigmoid(z) + after), 0.0)
        return jnp.einsum('bhqs,bshd->bqhd', w.astype(v.dtype), v)

    out = lax.map(block, (q_blocks, q_pos))
    return jnp.moveaxis(out, 0, 1).reshape(B, Tp, H, Dh)[:, :T]


def wkv7_scan(S0, r, w, k, v, a, b):
    def step(S, inp):
        r_t, w_t, k_t, v_t, a_t, b_t = inp
        sa = jnp.einsum('bhij,bhj->bhi', S, a_t)
        S = S * w_t[:, :, None, :] + sa[..., None] * b_t[:, :, None, :] + v_t[..., None] * k_t[:, :, None, :]
        return S, jnp.einsum('bhij,bhj->bhi', S, r_t)
    S, y = lax.scan(step, S0, tuple(jnp.moveaxis(t, 1, 0) for t in (r, w, k, v, a, b)))
    return S, jnp.moveaxis(y, 0, 1)


def rwkv7_time_mix(m, shift0, S0, p):
    B, T, _ = m.shape
    f32 = jnp.float32
    m_prev = jnp.concatenate([shift0.astype(m.dtype), m[:, :-1]], axis=1)
    ms = m + (m_prev - m) * p['tokshift_mu']
    r, k, v, wd, ad, gd = jnp.split(ms, RW_SPLITS, axis=-1)
    heads = lambda t: t.reshape(B, T, RW_HEADS, HEAD_DIM)
    w_log = -jax.nn.softplus(-(p['rw_w0'] + jnp.tanh(wd) @ p['rw_w2']).astype(f32)) - 0.5
    decay = jnp.exp(-jnp.exp(w_log))
    a = jax.nn.sigmoid((p['rw_a0'] + ad @ p['rw_a2']).astype(f32))
    g = jax.nn.sigmoid(gd) @ p['rw_g2']
    kk = heads(k.astype(f32) * p['rw_k_k'])
    kk = kk / jnp.maximum(jnp.sqrt(jnp.sum(kk * kk, axis=-1, keepdims=True)), 1e-12)
    k_h = heads(k.astype(f32) * (1.0 + (a - 1.0) * p['rw_k_a']))
    a_h = heads(a)
    r_h = heads(r.astype(f32))
    v_h = heads(v.astype(f32))
    S, y = wkv7_scan(S0.astype(f32), r_h, heads(decay), k_h, v_h, -kk, kk * a_h)
    y = head_group_norm(y, p['rw_lnx_w'].reshape(RW_HEADS, HEAD_DIM), p['rw_lnx_b'].reshape(RW_HEADS, HEAD_DIM))
    y = y + jnp.sum(r_h * k_h * p['rw_r_k'], axis=-1, keepdims=True) * v_h
    o = y.reshape(B, T, RW_WIDTH).astype(m.dtype) * g
    return o, S.astype(S0.dtype), m[:, -1:]


def token_mixing(u, past_k, past_v, S0, shift0, p):
    B, T, _ = u.shape
    proj = u @ p['w_in']
    q, k, v, m_rw, g_a, g_b = jnp.split(proj, IN_SPLITS, axis=-1)
    q = q.reshape(B, T, SB_HEADS, HEAD_DIM)
    k = k.reshape(B, T, SB_HEADS, HEAD_DIM)
    v = v.reshape(B, T, SB_HEADS, HEAD_DIM)
    k_all = jnp.concatenate([past_k.astype(k.dtype), k], axis=1)
    v_all = jnp.concatenate([past_v.astype(v.dtype), v], axis=1)
    o_a = stick_breaking_attention(q, k_all, v_all, past_k.shape[1]).reshape(B, T, SB_WIDTH)
    o_b, S, shift = rwkv7_time_mix(m_rw, shift0, S0, p)
    merged = jax.nn.sigmoid(g_a) * (o_a @ p['w_branch_a']) + jax.nn.sigmoid(g_b) * (o_b @ p['w_branch_b'])
    return merged @ p['w_out'], k, v, S, shift


def routed_experts(xt, expert_idx, weights, w_gate, w_up, w_down):
    N, D = xt.shape
    A = N * TOP_K
    flat_e = expert_idx.reshape(A)
    order = jnp.argsort(flat_e)
    sorted_e = flat_e[order]
    counts = jnp.bincount(flat_e, length=N_EXPERTS)
    padded = (counts + MOE_BLOCK - 1) // MOE_BLOCK * MOE_BLOCK
    start = jnp.cumsum(counts) - counts
    pstart = jnp.cumsum(padded) - padded
    dest = pstart[sorted_e] + jnp.arange(A, dtype=sorted_e.dtype) - start[sorted_e]
    n_blocks = -(-(A + N_EXPERTS * (MOE_BLOCK - 1)) // MOE_BLOCK)
    P = n_blocks * MOE_BLOCK
    src_token = order // TOP_K
    slot_token = jnp.full((P,), N, dtype=src_token.dtype).at[dest].set(src_token)
    x_pad = jnp.concatenate([xt, jnp.zeros((1, D), xt.dtype)], axis=0)
    xb = x_pad[slot_token].reshape(n_blocks, MOE_BLOCK, D)
    block_start = jnp.arange(n_blocks, dtype=padded.dtype) * MOE_BLOCK
    block_expert = jnp.minimum(jnp.searchsorted(jnp.cumsum(padded), block_start, side='right'), N_EXPERTS - 1)

    def expert_block(args):
        xblk, e = args
        h = jax.nn.silu(xblk @ w_gate[e]) * (xblk @ w_up[e])
        return h @ w_down[e]

    yb = lax.map(expert_block, (xb, block_expert)).reshape(P, D)
    contrib = yb[dest] * weights.reshape(A)[order][:, None].astype(yb.dtype)
    return jnp.zeros((N, D), yb.dtype).at[src_token].add(contrib)


def hier_moe(u, p):
    B, T, D = u.shape
    xt = u.reshape(B * T, D)
    N = xt.shape[0]
    rows = jnp.arange(N)
    logit_g = (xt @ p['w_router_group'] + p['b_router_group']).astype(jnp.float32)
    g_idx = jnp.argmax(logit_g, axis=-1)
    p_g = jax.nn.softmax(logit_g, axis=-1)[rows, g_idx][:, None]
    logit_e = (xt @ p['w_router_expert'] + p['b_router_expert']).astype(jnp.float32)
    logit_e = logit_e.reshape(N, N_GROUPS, EXPERTS_PER_GROUP)[rows, g_idx]
    top_p, top_i = lax.top_k(jax.nn.softmax(logit_e, axis=-1), TOP_K)
    weights = p_g * top_p / jnp.sum(top_p, axis=-1, keepdims=True)
    expert_idx = g_idx[:, None].astype(top_i.dtype) * EXPERTS_PER_GROUP + top_i
    y = routed_experts(xt, expert_idx, weights, p['w_exp_gate'], p['w_exp_up'], p['w_exp_down'])
    return y.reshape(B, T, D)


def layer_forward(x, c, past_k, past_v, S0, shift0, p):
    ada = jax.nn.silu(c) @ p['w_ada'] + p['b_ada']
    sh1, sc1, g1, sh2, sc2, g2 = [t[:, None, :] for t in jnp.split(ada, 6, axis=-1)]
    u = x * (1 + sc1) + sh1
    mix, k_new, v_new, S, shift = token_mixing(u, past_k, past_v, S0, shift0, p)
    h = layer_norm(DEEPNORM_ALPHA * x + g1 * mix, p['ln1_g'], p['ln1_b'])
    u2 = h * (1 + sc2) + sh2
    out = layer_norm(DEEPNORM_ALPHA * h + g2 * hier_moe(u2, p), p['ln2_g'], p['ln2_b'])
    return out, k_new, v_new, S, shift


def setup_inputs(seed: int = 0) -> dict:
    key = jax.random.key(seed)
    ks = iter(jax.random.split(key, 48))
    L, D = DEPTH, D_MODEL
    f32 = jnp.float32
    nrm = lambda shape, s=1.0: jax.random.normal(next(ks), shape, f32) * s
    uni = lambda shape, lo, hi: jax.random.uniform(next(ks), shape, f32, lo, hi)
    return {
        'x_prompt': nrm((BATCH, SEQ, D)),
        'x_sample': nrm((DEC_BATCH, DEC_SEQ, D)),
        'cache_sb_k': nrm((L, DEC_BATCH, PAST_LEN, SB_HEADS, HEAD_DIM)),
        'cache_sb_v': nrm((L, DEC_BATCH, PAST_LEN, SB_HEADS, HEAD_DIM)),
        'state_rwkv': nrm((L, DEC_BATCH, RW_HEADS, HEAD_DIM, HEAD_DIM), 0.3),
        'state_rwkv_shift': nrm((L, DEC_BATCH, 1, RW_COLS)),
        'c_prompt': nrm((BATCH, D)),
        'c_sample': nrm((DEC_BATCH, D)),
        'w_ada': nrm((L, D, 6 * D), 0.5 * D ** -0.5),
        'b_ada': nrm((L, 6 * D), 0.02),
        'w_in': nrm((L, D, N_IN), D ** -0.5),
        'tokshift_mu': uni((L, RW_COLS), 0.0, 1.0),
        'rw_w0': uni((L, RW_WIDTH), -3.0, 1.0),
        'rw_w2': nrm((L, RW_DECAY_LORA, RW_WIDTH), 0.5 * RW_DECAY_LORA ** -0.5),
        'rw_a0': nrm((L, RW_WIDTH), 0.1),
        'rw_a2': nrm((L, RW_AAA_LORA, RW_WIDTH), 0.5 * RW_AAA_LORA ** -0.5),
        'rw_g2': nrm((L, RW_GATE_LORA, RW_WIDTH), RW_GATE_LORA ** -0.5),
        'rw_k_k': 1.0 + nrm((L, RW_WIDTH), 0.1),
        'rw_k_a': 1.0 + nrm((L, RW_WIDTH), 0.1),
        'rw_r_k': nrm((L, RW_HEADS, HEAD_DIM), 0.1),
        'rw_lnx_w': 1.0 + nrm((L, RW_WIDTH), 0.1),
        'rw_lnx_b': nrm((L, RW_WIDTH), 0.02),
        'w_branch_a': nrm((L, SB_WIDTH, D), DEEPNORM_BETA * SB_WIDTH ** -0.5),
        'w_branch_b': nrm((L, RW_WIDTH, D), DEEPNORM_BETA * RW_WIDTH ** -0.5),
        'w_out': nrm((L, D, D), DEEPNORM_BETA * D ** -0.5),
        'ln1_g': 1.0 + nrm((L, D), 0.1),
        'ln1_b': nrm((L, D), 0.02),
        'w_router_group': nrm((L, D, N_GROUPS), D ** -0.5),
        'b_router_group': nrm((L, N_GROUPS), 0.01),
        'w_router_expert': nrm((L, D, N_EXPERTS), D ** -0.5),
        'b_router_expert': nrm((L, N_EXPERTS), 0.01),
        'w_exp_gate': nrm((L, N_EXPERTS, D, D_EXPERT), D ** -0.5),
        'w_exp_up': nrm((L, N_EXPERTS, D, D_EXPERT), D ** -0.5),
        'w_exp_down': nrm((L, N_EXPERTS, D_EXPERT, D), DEEPNORM_BETA * D_EXPERT ** -0.5),
        'ln2_g': 1.0 + nrm((L, D), 0.1),
        'ln2_b': nrm((L, D), 0.02),
    }


def reference(x_prompt, x_sample, cache_sb_k, cache_sb_v, state_rwkv, state_rwkv_shift, c_prompt, c_sample,
              w_ada, b_ada, w_in, tokshift_mu, rw_w0, rw_w2, rw_a0, rw_a2, rw_g2, rw_k_k, rw_k_a, rw_r_k,
              rw_lnx_w, rw_lnx_b, w_branch_a, w_branch_b, w_out, ln1_g, ln1_b, w_router_group, b_router_group,
              w_router_expert, b_router_expert, w_exp_gate, w_exp_up, w_exp_down, ln2_g, ln2_b):
    bp = x_prompt.shape[0]
    dt = x_prompt.dtype
    empty_kv = jnp.zeros((bp, 0, SB_HEADS, HEAD_DIM), dt)
    zero_state = jnp.zeros((bp, RW_HEADS, HEAD_DIM, HEAD_DIM), dt)
    zero_shift = jnp.zeros((bp, 1, RW_COLS), dt)
    y_prompt, y_sample = x_prompt, x_sample
    pk, pv, ps, psh, sk, sv, ss, ssh = [], [], [], [], [], [], [], []
    for l in range(DEPTH):
        p = dict(w_ada=w_ada[l], b_ada=b_ada[l], w_in=w_in[l], tokshift_mu=tokshift_mu[l], rw_w0=rw_w0[l],
                 rw_w2=rw_w2[l], rw_a0=rw_a0[l], rw_a2=rw_a2[l], rw_g2=rw_g2[l], rw_k_k=rw_k_k[l],
                 rw_k_a=rw_k_a[l], rw_r_k=rw_r_k[l], rw_lnx_w=rw_lnx_w[l], rw_lnx_b=rw_lnx_b[l],
                 w_branch_a=w_branch_a[l], w_branch_b=w_branch_b[l], w_out=w_out[l], ln1_g=ln1_g[l],
                 ln1_b=ln1_b[l], w_router_group=w_router_group[l], b_router_group=b_router_group[l],
                 w_router_expert=w_router_expert[l], b_router_expert=b_router_expert[l],
                 w_exp_gate=w_exp_gate[l], w_exp_up=w_exp_up[l], w_exp_down=w_exp_down[l],
                 ln2_g=ln2_g[l], ln2_b=ln2_b[l])
        y_prompt, k_p, v_p, S_p, sh_p = layer_forward(y_prompt, c_prompt, empty_kv, empty_kv, zero_state, zero_shift, p)
        y_sample, k_s, v_s, S_s, sh_s = layer_forward(y_sample, c_sample, cache_sb_k[l], cache_sb_v[l],
                                                      state_rwkv[l], state_rwkv_shift[l], p)
        pk.append(k_p); pv.append(v_p); ps.append(S_p); psh.append(sh_p)
        sk.append(k_s); sv.append(v_s); ss.append(S_s); ssh.append(sh_s)
    sb_k_prompt = jnp.stack(pk, axis=0)
    sb_v_prompt = jnp.stack(pv, axis=0)
    rwkv_state_prompt = jnp.stack(ps, axis=0)
    rwkv_shift_prompt = jnp.stack(psh, axis=0)
    sb_k_sample = jnp.stack(sk, axis=0)
    sb_v_sample = jnp.stack(sv, axis=0)
    rwkv_state_sample = jnp.stack(ss, axis=0)
    rwkv_shift_sample = jnp.stack(ssh, axis=0)
    return (y_prompt, y_sample, sb_k_prompt, sb_v_prompt, rwkv_state_prompt, rwkv_shift_prompt,
            sb_k_sample, sb_v_sample, rwkv_state_sample, rwkv_shift_sample)
```

```python
import functools

import jax
import jax.numpy as jnp
import numpy as np
from jax import lax
from jax.experimental import pallas as pl
from jax.experimental.pallas import tpu as pltpu

F32 = jnp.float32
BF16 = jnp.bfloat16
I32 = jnp.int32

D_MODEL = 2048
HEAD_DIM = 64
WIDTH = 1024
LORA_W = 64
LORA_A = 64
LORA_G = 160
RW_COLS = 3 * WIDTH + LORA_W + LORA_A + LORA_G
RW_PAD = 3456
N_GROUPS = 4
PER_GROUP = 8
N_EXPERTS = 32
D_EXPERT = 1024
LN_EPS = 1e-5
GN_EPS = 64e-5
ALPHA = 2.0 ** 0.25
LANES = 128
CHUNK = 64
VMEM_LIMIT = 56 * 1024 * 1024

NN = (((1,), (0,)), ((), ()))
NT = (((1,), (1,)), ((), ()))
TN = (((0,), (0,)), ((), ()))


def _cparams(*sem):
    return pltpu.CompilerParams(dimension_semantics=sem, vmem_limit_bytes=VMEM_LIMIT)


def _mm(a, b, dims=NN):
    return lax.dot_general(a, b, dims, preferred_element_type=F32)


def _split(x):
    hi = x.astype(BF16)
    lo = (x - hi.astype(F32)).astype(BF16)
    return hi, lo


def _split3(x):
    hi = x.astype(BF16)
    r1 = x - hi.astype(F32)
    mid = r1.astype(BF16)
    lo = (r1 - mid.astype(F32)).astype(BF16)
    return hi, mid, lo


def _dot3(a, b, dims=NN):
    ah, al = _split(a)
    bh, bl = _split(b)
    return _mm(ah, bh, dims) + (_mm(al, bh, dims) + _mm(ah, bl, dims))


def _dot_exact_rhs(a, b_bf16, terms=2):
    parts = _split(a) if terms == 2 else _split3(a)
    out = _mm(parts[0], b_bf16)
    for p in parts[1:]:
        out = out + _mm(p, b_bf16)
    return out


def _softplus(z):
    return jnp.maximum(z, 0.0) + jnp.log(1.0 + jnp.exp(-jnp.abs(z)))


def _sigmoid(z):
    return 1.0 / (1.0 + jnp.exp(-z))


def _ada_kernel(c_ref, w_ref, b_ref, o_ref):
    c = c_ref[...]
    o_ref[...] = _dot3(c * _sigmoid(c), w_ref[...]) + b_ref[...]


def _ada(c_all, w_ada, b_ada):
    rows = c_all.shape[0]
    n = w_ada.shape[1]
    tn = 1024
    return pl.pallas_call(
        _ada_kernel,
        grid=(n // tn,),
        in_specs=[pl.BlockSpec((rows, D_MODEL), lambda j: (0, 0)),
                  pl.BlockSpec((D_MODEL, tn), lambda j: (0, j)),
                  pl.BlockSpec((1, tn), lambda j: (0, j))],
        out_specs=pl.BlockSpec((rows, tn), lambda j: (0, j)),
        out_shape=jax.ShapeDtypeStruct((rows, n), F32),
        compiler_params=_cparams("arbitrary"),
        name="ada",
    )(c_all, w_ada, b_ada.reshape(1, n))


def _modmm_kernel(x_ref, sc_ref, sh_ref, w_ref, o_ref, u_ref, *, sigmoid_out):
    @pl.when(pl.program_id(2) == 0)
    def _():
        u_ref[...] = (x_ref[0] * (1.0 + sc_ref[0]) + sh_ref[0]).astype(BF16)

    y = _mm(u_ref[...], w_ref[...])
    if sigmoid_out:
        y = _sigmoid(y)
    o_ref[0] = y.astype(o_ref.dtype)


def _modmm(x, sc, sh, w_bf16, tn, out_dtype=F32, sigmoid_out=False, name="proj"):
    b, t, d = x.shape
    n = w_bf16.shape[1]
    tm = min(512, t)
    return pl.pallas_call(
        functools.partial(_modmm_kernel, sigmoid_out=sigmoid_out),
        grid=(b, t // tm, n // tn),
        in_specs=[pl.BlockSpec((1, tm, d), lambda bi, i, j: (bi, i, 0)),
                  pl.BlockSpec((1, 1, d), lambda bi, i, j: (bi, 0, 0)),
                  pl.BlockSpec((1, 1, d), lambda bi, i, j: (bi, 0, 0)),
                  pl.BlockSpec((d, tn), lambda bi, i, j: (0, j))],
        out_specs=pl.BlockSpec((1, tm, tn), lambda bi, i, j: (bi, i, j)),
        out_shape=jax.ShapeDtypeStruct((b, t, n), out_dtype),
        scratch_shapes=[pltpu.VMEM((tm, d), BF16)],
        compiler_params=_cparams("arbitrary", "arbitrary", "arbitrary"),
        name=name,
    )(x, sc, sh, w_bf16)


def _sb_kernel(qi_ref, kb_ref, first_ref, last_ref, q_ref, k_ref, v_ref, o_ref, acc_ref, car_ref,
               *, tq, tk, past, nl):
    p = pl.program_id(2)
    qi = qi_ref[p]
    kb = kb_ref[p]

    @pl.when(first_ref[p] == 1)
    def _():
        acc_ref[...] = jnp.zeros_like(acc_ref)
        car_ref[...] = jnp.zeros_like(car_ref)

    pos = past + qi * tq + lax.broadcasted_iota(I32, (tq, tk), 0)
    kpos = kb * tk + lax.broadcasted_iota(I32, (tq, tk), 1)
    mask = kpos < pos
    later = (lax.broadcasted_iota(I32, (tk, tk), 0) > lax.broadcasted_iota(I32, (tk, tk), 1)).astype(BF16)
    lane = lax.broadcasted_iota(I32, (tq, LANES), 1)
    for l in range(nl):
        cols = slice(l * LANES, (l + 1) * LANES)
        q = q_ref[0, :, cols] * (HEAD_DIM ** -0.5)
        kbf = k_ref[0, :, cols].astype(BF16)
        vbf = v_ref[0, :, cols].astype(BF16)
        for hh in range(2):
            head = (lane < HEAD_DIM) if hh == 0 else (lane >= HEAD_DIM)
            z = _mm(jnp.where(head, q, 0.0).astype(BF16), kbf, NT)
            sp = _softplus(z)
            log_keep = jnp.where(mask, -sp, 0.0)
            log_beta = z - sp
            after = _dot_exact_rhs(log_keep, later) + car_ref[2 * l + hh]
            w = jnp.where(mask, jnp.exp(log_beta + after), 0.0).astype(BF16)
            acc_ref[2 * l + hh] += _mm(w, vbf)
            car_ref[2 * l + hh] += jnp.sum(log_keep, axis=1, keepdims=True)

    @pl.when(last_ref[p] == 1)
    def _():
        for l in range(nl):
            o_ref[0, :, l * LANES:(l + 1) * LANES] = jnp.where(lane < HEAD_DIM, acc_ref[2 * l], acc_ref[2 * l + 1])


def _sb_attention(q, k_all, v_all, past, tq, tk, nl):
    b, t, _ = q.shape
    nq = t // tq
    qi_l, kb_l, first_l, last_l = [], [], [], []
    for qi in range(nq):
        top = (past + (qi + 1) * tq - 2) // tk
        for kb in range(top, -1, -1):
            qi_l.append(qi)
            kb_l.append(kb)
            first_l.append(int(kb == top))
            last_l.append(int(kb == 0))
    tabs = [jnp.asarray(np.asarray(a, np.int32)) for a in (qi_l, kb_l, first_l, last_l)]
    ng = WIDTH // (nl * LANES)
    grid_spec = pltpu.PrefetchScalarGridSpec(
        num_scalar_prefetch=4,
        grid=(b, ng, len(qi_l)),
        in_specs=[pl.BlockSpec((1, tq, nl * LANES), lambda bi, g, p, qi, kb, f, la: (bi, qi[p], g)),
                  pl.BlockSpec((1, tk, nl * LANES), lambda bi, g, p, qi, kb, f, la: (bi, kb[p], g)),
                  pl.BlockSpec((1, tk, nl * LANES), lambda bi, g, p, qi, kb, f, la: (bi, kb[p], g))],
        out_specs=pl.BlockSpec((1, tq, nl * LANES), lambda bi, g, p, qi, kb, f, la: (bi, qi[p], g)),
        scratch_shapes=[pltpu.VMEM((2 * nl, tq, LANES), F32), pltpu.VMEM((2 * nl, tq, 1), F32)],
    )
    return pl.pallas_call(
        functools.partial(_sb_kernel, tq=tq, tk=tk, past=past, nl=nl),
        grid_spec=grid_spec,
        out_shape=jax.ShapeDtypeStruct((b, t, WIDTH), F32),
        compiler_params=_cparams("arbitrary", "arbitrary", "arbitrary"),
        name="sb_attention",
    )(*tabs, q, k_all, v_all)


def _head_sum(x, same_head):
    return jnp.concatenate(
        [_dot_exact_rhs(x[:, c * LANES:(c + 1) * LANES], same_head) for c in range(WIDTH // LANES)], axis=1)


def _rw_prep_kernel(m_ref, sh0_ref, mu_ref, w0_ref, w2_ref, a0_ref, a2_ref, g2_ref, kk_ref, ka_ref, rk_ref,
                    rt_ref, kt_ref, at_ref, bt_ref, v_ref, ec_ref, bonus_ref, g_ref, prev_ref, *, tr):
    @pl.when(pl.program_id(1) == 0)
    def _():
        prev_ref[...] = sh0_ref[0]

    m = m_ref[0]
    row = lax.broadcasted_iota(I32, m.shape, 0)
    m_prev = jnp.where(row == 0, prev_ref[...], pltpu.roll(m, 1, 0))
    prev_ref[...] = m[tr - 1:tr, :]
    ms = m + (m_prev - m) * mu_ref[...]
    r = ms[:, 0:WIDTH]
    k = ms[:, WIDTH:2 * WIDTH]
    v = ms[:, 2 * WIDTH:3 * WIDTH]
    lora_wa = ms[:, 3 * WIDTH:3 * WIDTH + LANES]
    lora_g = ms[:, 3 * WIDTH + LANES:RW_PAD]

    w_log = -_softplus(-(w0_ref[...] + _dot3(jnp.tanh(lora_wa), w2_ref[...]))) - 0.5
    log_decay = -jnp.exp(w_log)
    a = _sigmoid(a0_ref[...] + _dot3(lora_wa, a2_ref[...]))
    g = _dot3(_sigmoid(lora_g), g2_ref[...])

    same_head = (lax.broadcasted_iota(I32, (LANES, LANES), 0) // HEAD_DIM ==
                 lax.broadcasted_iota(I32, (LANES, LANES), 1) // HEAD_DIM).astype(BF16)
    kk = k * kk_ref[...]
    kk = kk / jnp.maximum(jnp.sqrt(_head_sum(kk * kk, same_head)), 1e-12)
    kh = k * (1.0 + (a - 1.0) * ka_ref[...])
    bonus = _head_sum(r * kh * rk_ref[...], same_head)

    ti = lax.broadcasted_iota(I32, (tr, tr), 0)
    si = lax.broadcasted_iota(I32, (tr, tr), 1)
    tri = ((si <= ti) & (si // CHUNK == ti // CHUNK)).astype(BF16)
    parts = _split3(log_decay)
    cl = _mm(tri, parts[0]) + (_mm(tri, parts[1]) + _mm(tri, parts[2]))
    ec = jnp.exp(cl)
    inv = jnp.exp(-cl)
    rt_ref[0] = r * ec
    kt_ref[0] = kh * inv
    at_ref[0] = -kk * jnp.exp(cl - log_decay)
    bt_ref[0] = kk * a * inv
    v_ref[0] = v
    ec_ref[0] = ec
    bonus_ref[0] = bonus
    g_ref[0] = g


def _rw_prep(m, shift0, rwp):
    b, t, _ = m.shape
    tr = min(128, t)
    row_spec = pl.BlockSpec((1, tr, WIDTH), lambda bi, i: (bi, i, 0))
    vec = lambda n: pl.BlockSpec((1, n), lambda bi, i: (0, 0))
    mat = lambda r: pl.BlockSpec((r, WIDTH), lambda bi, i: (0, 0))
    out = jax.ShapeDtypeStruct((b, t, WIDTH), F32)
    return pl.pallas_call(
        functools.partial(_rw_prep_kernel, tr=tr),
        grid=(b, t // tr),
        in_specs=[pl.BlockSpec((1, tr, RW_PAD), lambda bi, i: (bi, i, 0)),
                  pl.BlockSpec((1, 1, RW_PAD), lambda bi, i: (bi, 0, 0)),
                  vec(RW_PAD), vec(WIDTH), mat(LANES), vec(WIDTH), mat(LANES), mat(RW_PAD - 3 * WIDTH - LANES),
                  vec(WIDTH), vec(WIDTH), vec(WIDTH)],
        out_specs=[row_spec] * 8,
        out_shape=[out] * 8,
        scratch_shapes=[pltpu.VMEM((1, RW_PAD), F32)],
        compiler_params=_cparams("arbitrary", "arbitrary"),
        name="rwkv_prep",
    )(m, shift0, rwp["mu"], rwp["w0"], rwp["w2"], rwp["a0"], rwp["a2"], rwp["g2"], rwp["k_k"], rwp["k_a"],
      rwp["r_k"])


def _rw_scan_kernel(rt_ref, kt_ref, at_ref, bt_ref, v_ref, ec_ref, bonus_ref, g_ref, s0_ref, lnw_ref, lnb_ref,
                    o_ref, sout_ref, s_ref, *, n_chunks):
    ci = pl.program_id(2)

    @pl.when(ci == 0)
    def _():
        s_ref[...] = s0_ref[0, 0]

    c = CHUNK
    lane = lax.broadcasted_iota(I32, (c, LANES), 1)
    ti = lax.broadcasted_iota(I32, (c, c), 0)
    si = lax.broadcasted_iota(I32, (c, c), 1)
    strict = si < ti
    incl = si <= ti
    eye = (si == ti).astype(F32)
    vi = lax.broadcasted_iota(I32, (LANES, LANES), 0)
    kj = lax.broadcasted_iota(I32, (LANES, LANES), 1)
    same_head_b = (vi // HEAD_DIM) == (kj // HEAD_DIM)
    same_head = same_head_b.astype(BF16)

    def chunk(j, carry):
        rows = pl.ds(pl.multiple_of(j * c, c), c)
        rt = rt_ref[0, rows, :]
        kt = kt_ref[0, rows, :]
        at = at_ref[0, rows, :]
        bt = bt_ref[0, rows, :]
        v = v_ref[0, rows, :]
        s = s_ref[...]
        x0 = _dot3(at, s, NT)
        y0 = _dot3(rt, s, NT)
        uhs, yhs = [], []
        for hh in range(2):
            head = (lane < HEAD_DIM) if hh == 0 else (lane >= HEAD_DIM)
            ah = jnp.where(head, at, 0.0)
            rh = jnp.where(head, rt, 0.0)
            l_ab = jnp.where(strict, _dot3(ah, bt, NT), 0.0)
            l_ak = jnp.where(strict, _dot3(ah, kt, NT), 0.0)
            m_rb = jnp.where(incl, _dot3(rh, bt, NT), 0.0)
            m_rk = jnp.where(incl, _dot3(rh, kt, NT), 0.0)
            tinv = eye + l_ab
            pw = l_ab
            for _ in range(5):
                pw = _dot3(pw, pw)
                tinv = tinv + _dot3(tinv, pw)
            uh = _dot3(tinv, x0 + _dot3(l_ak, v))
            yh = y0 + _dot3(jnp.concatenate([m_rb, m_rk], axis=1), jnp.concatenate([uh, v], axis=0))
            uhs.append(uh)
            yhs.append(yh)
        u = jnp.where(lane < HEAD_DIM, uhs[0], uhs[1])
        y = jnp.where(lane < HEAD_DIM, yhs[0], yhs[1])
        ds = _dot3(u, bt, TN) + _dot3(v, kt, TN)
        p_end = ec_ref[0, pl.ds(pl.multiple_of(j * c, c) + (c - 1), 1), :]
        s_ref[...] = (s + jnp.where(same_head_b, ds, 0.0)) * p_end

        mu = _dot_exact_rhs(y, same_head) * (1.0 / HEAD_DIM)
        dlt = y - mu
        var = _dot_exact_rhs(dlt * dlt, same_head) * (1.0 / HEAD_DIM)
        yn = dlt * lax.rsqrt(var + GN_EPS) * lnw_ref[...] + lnb_ref[...]
        o_ref[0, rows, :] = (yn + bonus_ref[0, rows, :] * v) * g_ref[0, rows, :]
        return carry

    lax.fori_loop(0, n_chunks, chunk, 0)

    @pl.when(ci == pl.num_programs(2) - 1)
    def _():
        sout_ref[0, 0] = s_ref[...]


def _rw_scan(prep, s0_bd, lnw, lnb):
    rt = prep[0]
    b, t, _ = rt.shape
    tc = min(512, t)
    n_chunks = tc // CHUNK
    nhp = WIDTH // LANES
    row_spec = pl.BlockSpec((1, tc, LANES), lambda bi, h, i: (bi, i, h))
    st_spec = pl.BlockSpec((1, 1, LANES, LANES), lambda bi, h, i: (bi, h, 0, 0))
    vec_spec = pl.BlockSpec((1, LANES), lambda bi, h, i: (0, h))
    return pl.pallas_call(
        functools.partial(_rw_scan_kernel, n_chunks=n_chunks),
        grid=(b, nhp, t // tc),
        in_specs=[row_spec] * 8 + [st_spec, vec_spec, vec_spec],
        out_specs=[row_spec, st_spec],
        out_shape=[jax.ShapeDtypeStruct((b, t, WIDTH), F32), jax.ShapeDtypeStruct((b, nhp, LANES, LANES), F32)],
        scratch_shapes=[pltpu.VMEM((LANES, LANES), F32)],
        compiler_params=_cparams("arbitrary", "arbitrary", "arbitrary"),
        name="rwkv_scan",
    )(*prep, s0_bd, lnw, lnb)


def _merge_kernel(oa_ref, ob_ref, sg_ref, x_ref, g1_ref, sc2_ref, sh2_ref, pa_ref, pb_ref, wo_ref, lng_ref, lnb_ref,
                  wr_ref, br_ref, h_ref, u2_ref, route_ref):
    sg = sg_ref[0]
    ya = _mm(oa_ref[0].astype(BF16), pa_ref[...])
    yb = _mm(ob_ref[0].astype(BF16), pb_ref[...])
    merged = sg[:, :D_MODEL].astype(F32) * ya + sg[:, D_MODEL:].astype(F32) * yb
    mix = _mm(merged.astype(BF16), wo_ref[...])
    hin = ALPHA * x_ref[0] + g1_ref[0] * mix
    mu = jnp.mean(hin, axis=-1, keepdims=True)
    dlt = hin - mu
    var = jnp.mean(dlt * dlt, axis=-1, keepdims=True)
    h = dlt * lax.rsqrt(var + LN_EPS) * lng_ref[...] + lnb_ref[...]
    h_ref[0] = h
    u2 = h * (1.0 + sc2_ref[0]) + sh2_ref[0]
    u2_ref[0] = u2

    logits = _dot3(u2, wr_ref[...]) + br_ref[...]
    lane_i = lax.broadcasted_iota(I32, logits.shape, 1)
    lane = lane_i.astype(F32)
    neg = -jnp.inf
    first_at = lambda vals, top: jnp.min(jnp.where(vals == top, lane, float(LANES)), axis=1, keepdims=True)
    lg = jnp.where(lane_i < N_GROUPS, logits, neg)
    mg = jnp.max(lg, axis=1, keepdims=True)
    g_idx = first_at(lg, mg)
    p_g = 1.0 / jnp.sum(jnp.exp(lg - mg), axis=1, keepdims=True)
    lo_lane = N_GROUPS + PER_GROUP * g_idx
    le = jnp.where((lane >= lo_lane) & (lane < lo_lane + PER_GROUP), logits, neg)
    m1 = jnp.max(le, axis=1, keepdims=True)
    i1 = first_at(le, m1)
    le2 = jnp.where(lane == i1, neg, le)
    m2 = jnp.max(le2, axis=1, keepdims=True)
    i2 = first_at(le2, m2)
    ratio = jnp.exp(m2 - m1)
    w1 = p_g / (1.0 + ratio)
    w2 = p_g * ratio / (1.0 + ratio)
    route = jnp.where(lane_i == 0, i1 - N_GROUPS,
                      jnp.where(lane_i == 1, i2 - N_GROUPS,
                                jnp.where(lane_i == 2, w1, jnp.where(lane_i == 3, w2, 0.0))))
    route_ref[0] = route


def _merge(o_a, o_b, sg, x, g1, sc2, sh2, wts):
    b, t, d = x.shape
    tm = min(256, t)
    const = lambda shape: pl.BlockSpec(shape, lambda bi, i: (0,) * len(shape), pipeline_mode=pl.Buffered(1))
    seq = pl.BlockSpec((1, 1, d), lambda bi, i: (bi, 0, 0))
    row = lambda n: pl.BlockSpec((1, tm, n), lambda bi, i: (bi, i, 0))
    return pl.pallas_call(
        _merge_kernel,
        grid=(b, t // tm),
        in_specs=[row(WIDTH), row(WIDTH), row(2 * d), row(d), seq, seq, seq,
                  const((WIDTH, d)), const((WIDTH, d)), const((d, d)), const((1, d)), const((1, d)),
                  const((d, LANES)), const((1, LANES))],
        out_specs=[row(d), row(d), row(LANES)],
        out_shape=[jax.ShapeDtypeStruct((b, t, d), F32), jax.ShapeDtypeStruct((b, t, d), F32),
                   jax.ShapeDtypeStruct((b, t, LANES), F32)],
        compiler_params=_cparams("arbitrary", "arbitrary"),
        name="merge_route",
    )(o_a, o_b, sg, x, g1, sc2, sh2, wts["p_a"], wts["p_b"], wts["w_out"], wts["ln1_g"], wts["ln1_b"],
      wts["w_route"], wts["b_route"])


def _moe_kernel(be_ref, nused_ref, tok_ref, x_hbm, wg_ref, wu_ref, wd_ref, y_ref, xbuf, sem, *, tb):
    i = pl.program_id(0)

    @pl.when(i < nused_ref[0])
    def _():
        def row_copy(r):
            return pltpu.make_async_copy(x_hbm.at[pl.ds(tok_ref[i * tb + r], 1)], xbuf.at[pl.ds(r, 1)], sem)

        def start(r, carry):
            row_copy(r).start()
            return carry

        def wait(r, carry):
            row_copy(r).wait()
            return carry

        lax.fori_loop(0, tb, start, 0)
        lax.fori_loop(0, tb, wait, 0)
        x = xbuf[...].astype(BF16)
        hg = _mm(x, wg_ref[0])
        hu = _mm(x, wu_ref[0])
        hid = (hg * _sigmoid(hg) * hu).astype(BF16)
        y_ref[...] = _mm(hid, wd_ref[0])

    @pl.when(i >= nused_ref[0])
    def _():
        y_ref[...] = jnp.zeros_like(y_ref)


def _moe(block_expert, n_used, slot_token, x_all, wg, wu, wd, tb):
    n_blocks = block_expert.shape[0]
    d = x_all.shape[1]
    grid_spec = pltpu.PrefetchScalarGridSpec(
        num_scalar_prefetch=3,
        grid=(n_blocks,),
        in_specs=[pl.BlockSpec(memory_space=pl.ANY),
                  pl.BlockSpec((1, d, D_EXPERT), lambda i, be, nu, tok: (be[i], 0, 0)),
                  pl.BlockSpec((1, d, D_EXPERT), lambda i, be, nu, tok: (be[i], 0, 0)),
                  pl.BlockSpec((1, D_EXPERT, d), lambda i, be, nu, tok: (be[i], 0, 0))],
        out_specs=pl.BlockSpec((tb, d), lambda i, be, nu, tok: (i, 0)),
        scratch_shapes=[pltpu.VMEM((tb, d), F32), pltpu.SemaphoreType.DMA(())],
    )
    return pl.pallas_call(
        functools.partial(_moe_kernel, tb=tb),
        grid_spec=grid_spec,
        out_shape=jax.ShapeDtypeStruct((n_blocks * tb, d), F32),
        compiler_params=_cparams("arbitrary"),
        name="moe_experts",
    )(block_expert, n_used, slot_token, x_all, wg, wu, wd)


def _final_kernel(dest_ref, y_hbm, h_ref, rw_ref, g2_ref, lng_ref, lnb_ref, o_ref, ybuf, sem, *, tm, per_seq):
    base = (pl.program_id(0) * per_seq + pl.program_id(1)) * tm

    def row_copy(r, k):
        return pltpu.make_async_copy(y_hbm.at[pl.ds(dest_ref[2 * (base + r) + k], 1)], ybuf.at[k, pl.ds(r, 1)],
                                     sem.at[k])

    def start(r, carry):
        row_copy(r, 0).start()
        row_copy(r, 1).start()
        return carry

    def wait(r, carry):
        row_copy(r, 0).wait()
        row_copy(r, 1).wait()
        return carry

    lax.fori_loop(0, tm, start, 0)
    lax.fori_loop(0, tm, wait, 0)
    rw = rw_ref[0]
    moe = rw[:, 2:3] * ybuf[0] + rw[:, 3:4] * ybuf[1]
    xin = ALPHA * h_ref[0] + g2_ref[0] * moe
    mu = jnp.mean(xin, axis=-1, keepdims=True)
    dlt = xin - mu
    var = jnp.mean(dlt * dlt, axis=-1, keepdims=True)
    o_ref[0] = dlt * lax.rsqrt(var + LN_EPS) * lng_ref[...] + lnb_ref[...]


def _final(dest, y_sorted, h, route, g2, ln2_g, ln2_b):
    b, t, d = h.shape
    tm = min(256, t)
    per_seq = t // tm
    grid_spec = pltpu.PrefetchScalarGridSpec(
        num_scalar_prefetch=1,
        grid=(b, per_seq),
        in_specs=[pl.BlockSpec(memory_space=pl.ANY),
                  pl.BlockSpec((1, tm, d), lambda bi, i, de: (bi, i, 0)),
                  pl.BlockSpec((1, tm, LANES), lambda bi, i, de: (bi, i, 0)),
                  pl.BlockSpec((1, 1, d), lambda bi, i, de: (bi, 0, 0)),
                  pl.BlockSpec((1, d), lambda bi, i, de: (0, 0)),
                  pl.BlockSpec((1, d), lambda bi, i, de: (0, 0))],
        out_specs=pl.BlockSpec((1, tm, d), lambda bi, i, de: (bi, i, 0)),
        scratch_shapes=[pltpu.VMEM((2, tm, d), F32), pltpu.SemaphoreType.DMA((2,))],
    )
    return pl.pallas_call(
        functools.partial(_final_kernel, tm=tm, per_seq=per_seq),
        grid_spec=grid_spec,
        out_shape=jax.ShapeDtypeStruct((b, t, d), F32),
        compiler_params=_cparams("arbitrary", "arbitrary"),
        name="combine_ln2",
    )(dest, y_sorted, h, route, g2, ln2_g, ln2_b)


def _pad_rows(a, rows):
    return jnp.pad(a, ((0, rows - a.shape[0]), (0, 0)))


def _mixers(x, ada, past_k, past_v, s0, shift0, wts, sb_tiles):
    b, t, d = x.shape
    sh1, sc1, g1, sh2, sc2, g2 = [a[:, None, :] for a in jnp.split(ada, 6, axis=-1)]
    q = _modmm(x, sc1, sh1, wts["w_q"], 1024, name="proj_q")
    k = _modmm(x, sc1, sh1, wts["w_k"], 1024, name="proj_k")
    v = _modmm(x, sc1, sh1, wts["w_v"], 1024, name="proj_v")
    m = _modmm(x, sc1, sh1, wts["w_m"], RW_PAD // 3, name="proj_rwkv")
    sg = _modmm(x, sc1, sh1, wts["w_g"], 1024, out_dtype=BF16, sigmoid_out=True, name="proj_gates")

    tq, tk, nl = sb_tiles
    past = past_k.shape[1]
    s_pad = -(-(past + t) // tk) * tk
    tail = jnp.zeros((b, s_pad - past - t, WIDTH), F32)
    k_all = jnp.concatenate([past_k.reshape(b, past, WIDTH), k, tail], axis=1)
    v_all = jnp.concatenate([past_v.reshape(b, past, WIDTH), v, tail], axis=1)
    o_a = _sb_attention(q, k_all, v_all, past, tq, tk, nl)

    shift0_p = jnp.pad(shift0, ((0, 0), (0, 0), (0, RW_PAD - RW_COLS)))
    prep = _rw_prep(m, shift0_p, wts["rw"])
    nhp = WIDTH // LANES
    s0_pairs = s0.reshape(b, nhp, 2, HEAD_DIM, HEAD_DIM)
    s0_bd = jnp.zeros((b, nhp, LANES, LANES), F32)
    s0_bd = s0_bd.at[:, :, :HEAD_DIM, :HEAD_DIM].set(s0_pairs[:, :, 0]).at[:, :, HEAD_DIM:, HEAD_DIM:].set(s0_pairs[:, :, 1])
    o_b, s_bd = _rw_scan(prep, s0_bd, wts["rw"]["lnx_w"], wts["rw"]["lnx_b"])
    s_new = jnp.stack([s_bd[:, :, :HEAD_DIM, :HEAD_DIM], s_bd[:, :, HEAD_DIM:, HEAD_DIM:]], axis=2)
    s_new = s_new.reshape(b, 2 * nhp, HEAD_DIM, HEAD_DIM)
    shift_new = m[:, t - 1:t, :RW_COLS]

    h, u2, route = _merge(o_a, o_b, sg, x, g1, sc2, sh2, wts)
    return h, u2, route, g2, k, v, s_new, shift_new


def kernel(x_prompt, x_sample, cache_sb_k, cache_sb_v, state_rwkv, state_rwkv_shift, c_prompt, c_sample, w_ada, b_ada, w_in, tokshift_mu, rw_w0, rw_w2, rw_a0, rw_a2, rw_g2, rw_k_k, rw_k_a, rw_r_k, rw_lnx_w, rw_lnx_b, w_branch_a, w_branch_b, w_out, ln1_g, ln1_b, w_router_group, b_router_group, w_router_expert, b_router_expert, w_exp_gate, w_exp_up, w_exp_down, ln2_g, ln2_b):
    depth = w_ada.shape[0]
    assert depth == 1
    bp, tp, d = x_prompt.shape
    bs, ts, _ = x_sample.shape
    l = 0

    w_in_l = w_in[l]
    w_rw = w_in_l[:, 3 * WIDTH:3 * WIDTH + RW_COLS]
    row_vec = lambda a: a.reshape(1, -1)
    pad_cols = lambda a, n: jnp.pad(a, ((0, 0), (0, n - a.shape[1])))
    n_lg = RW_PAD - 3 * WIDTH - LANES
    wts = {
        "w_q": w_in_l[:, 0:WIDTH].astype(BF16),
        "w_k": w_in_l[:, WIDTH:2 * WIDTH].astype(BF16),
        "w_v": w_in_l[:, 2 * WIDTH:3 * WIDTH].astype(BF16),
        "w_m": pad_cols(w_rw, RW_PAD).astype(BF16),
        "w_g": w_in_l[:, 3 * WIDTH + RW_COLS:].astype(BF16),
        "p_a": w_branch_a[l].astype(BF16),
        "p_b": w_branch_b[l].astype(BF16),
        "w_out": w_out[l].astype(BF16),
        "ln1_g": row_vec(ln1_g[l]),
        "ln1_b": row_vec(ln1_b[l]),
        "w_route": pad_cols(jnp.concatenate([w_router_group[l], w_router_expert[l]], axis=1), LANES),
        "b_route": pad_cols(row_vec(jnp.concatenate([b_router_group[l], b_router_expert[l]])), LANES),
        "rw": {
            "mu": pad_cols(row_vec(tokshift_mu[l]), RW_PAD),
            "w0": row_vec(rw_w0[l]),
            "w2": _pad_rows(rw_w2[l], LANES),
            "a0": row_vec(rw_a0[l]),
            "a2": jnp.concatenate([jnp.zeros((LORA_W, WIDTH), F32), rw_a2[l]], axis=0),
            "g2": _pad_rows(rw_g2[l], n_lg),
            "k_k": row_vec(rw_k_k[l]),
            "k_a": row_vec(rw_k_a[l]),
            "r_k": row_vec(rw_r_k[l]),
            "lnx_w": row_vec(rw_lnx_w[l]),
            "lnx_b": row_vec(rw_lnx_b[l]),
        },
    }

    rows = -(-(bp + bs) // 8) * 8
    c_all = _pad_rows(jnp.concatenate([c_prompt, c_sample], axis=0), rows)
    ada = _ada(c_all, w_ada[l], b_ada[l])

    empty = jnp.zeros((bp, 0, WIDTH), F32)
    zero_state = jnp.zeros((bp, WIDTH // HEAD_DIM, HEAD_DIM, HEAD_DIM), F32)
    zero_shift = jnp.zeros((bp, 1, RW_COLS), F32)
    hp, u2p, route_p, g2p, kp, vp, sp_, shp = _mixers(
        x_prompt, ada[:bp], empty, empty, zero_state, zero_shift, wts, (min(256, tp), 256, 2))
    hs, u2s, route_s, g2s, ks, vs, ss_, shs = _mixers(
        x_sample, ada[bp:bp + bs], cache_sb_k[l], cache_sb_v[l], state_rwkv[l], state_rwkv_shift[l], wts,
        (min(256, ts), 256, 2))

    n_p, n_s = bp * tp, bs * ts
    n_tok = n_p + n_s
    tb = 256
    route_all = jnp.concatenate([route_p.reshape(n_p, LANES), route_s.reshape(n_s, LANES)], axis=0)
    flat_e = route_all[:, 0:2].astype(I32).reshape(n_tok * 2)
    onehot = (flat_e[:, None] == jnp.arange(N_EXPERTS, dtype=I32)[None, :]).astype(I32)
    csum = jnp.cumsum(onehot, axis=0)
    rank = jnp.take_along_axis(csum, flat_e[:, None], axis=1)[:, 0] - 1
    counts = csum[-1]
    padded = (counts + tb - 1) // tb * tb
    pend = jnp.cumsum(padded)
    pstart = pend - padded
    dest = (pstart[flat_e] + rank).astype(I32)
    n_blocks = (2 * n_tok + N_EXPERTS * (tb - 1)) // tb + 1
    slot_token = jnp.zeros((n_blocks * tb,), I32).at[dest].set(jnp.arange(2 * n_tok, dtype=I32) // 2)
    block_start = jnp.arange(n_blocks, dtype=I32) * tb
    block_expert = jnp.minimum(jnp.searchsorted(pend, block_start, side="right"), N_EXPERTS - 1).astype(I32)
    n_used = (pend[-1] // tb).astype(I32).reshape(1)

    u2_all = jnp.concatenate([u2p.reshape(n_p, d), u2s.reshape(n_s, d)], axis=0)
    y_sorted = _moe(block_expert, n_used, slot_token, u2_all, w_exp_gate[l].astype(BF16),
                    w_exp_up[l].astype(BF16), w_exp_down[l].astype(BF16), tb)

    ln2g, ln2b = row_vec(ln2_g[l]), row_vec(ln2_b[l])
    y_prompt = _final(dest[:2 * n_p], y_sorted, hp, route_p, g2p, ln2g, ln2b)
    y_sample = _final(dest[2 * n_p:], y_sorted, hs, route_s, g2s, ln2g, ln2b)

    heads = lambda a: a.reshape(1, a.shape[0], a.shape[1], WIDTH // HEAD_DIM, HEAD_DIM)
    return (y_prompt, y_sample, heads(kp), heads(vp), sp_[None], shp[None],
            heads(ks), heads(vs), ss_[None], shs[None])
```

```python
import functools

import jax
import jax.numpy as jnp
import numpy as np
from jax import lax
from jax.experimental import pallas as pl
from jax.experimental.pallas import tpu as pltpu

F32 = jnp.float32
BF16 = jnp.bfloat16
I32 = jnp.int32

D_MODEL = 2048
HEAD_DIM = 64
WIDTH = 1024
LORA_W = 64
LORA_A = 64
LORA_G = 160
RW_COLS = 3 * WIDTH + LORA_W + LORA_A + LORA_G
RW_PAD = 3456
N_GROUPS = 4
PER_GROUP = 8
N_EXPERTS = 32
D_EXPERT = 1024
LN_EPS = 1e-5
GN_EPS = 64e-5
ALPHA = 2.0 ** 0.25
LOG2E = 1.4426950408889634
F32_UNDERFLOW_LOG2 = -150.0
LANES = 128
CHUNK = 64
VMEM_LIMIT = 56 * 1024 * 1024

NN = (((1,), (0,)), ((), ()))
NT = (((1,), (1,)), ((), ()))
TN = (((0,), (0,)), ((), ()))


def _cparams(*sem):
    return pltpu.CompilerParams(dimension_semantics=sem, vmem_limit_bytes=VMEM_LIMIT)


def _mm(a, b, dims=NN):
    return lax.dot_general(a, b, dims, preferred_element_type=F32)


def _split(x):
    hi = x.astype(BF16)
    lo = (x - hi.astype(F32)).astype(BF16)
    return hi, lo


def _split3(x):
    hi = x.astype(BF16)
    r1 = x - hi.astype(F32)
    mid = r1.astype(BF16)
    lo = (r1 - mid.astype(F32)).astype(BF16)
    return hi, mid, lo


def _dot3(a, b, dims=NN):
    ah, al = _split(a)
    bh, bl = _split(b)
    return _mm(ah, bh, dims) + (_mm(al, bh, dims) + _mm(ah, bl, dims))


def _dot_exact_rhs(a, b_bf16, terms=2):
    parts = _split(a) if terms == 2 else _split3(a)
    out = _mm(parts[0], b_bf16)
    for p in parts[1:]:
        out = out + _mm(p, b_bf16)
    return out


def _softplus(z):
    return jnp.maximum(z, 0.0) + jnp.log(1.0 + jnp.exp(-jnp.abs(z)))


def _sigmoid(z):
    return 1.0 / (1.0 + jnp.exp(-z))


def _ada_kernel(c_ref, w_ref, b_ref, o_ref):
    c = c_ref[...]
    o_ref[...] = _dot3(c * _sigmoid(c), w_ref[...]) + b_ref[...]


def _ada(c_all, w_ada, b_ada):
    rows = c_all.shape[0]
    n = w_ada.shape[1]
    tn = 1024
    return pl.pallas_call(
        _ada_kernel,
        grid=(n // tn,),
        in_specs=[pl.BlockSpec((rows, D_MODEL), lambda j: (0, 0)),
                  pl.BlockSpec((D_MODEL, tn), lambda j: (0, j)),
                  pl.BlockSpec((1, tn), lambda j: (0, j))],
        out_specs=pl.BlockSpec((rows, tn), lambda j: (0, j)),
        out_shape=jax.ShapeDtypeStruct((rows, n), F32),
        compiler_params=_cparams("arbitrary"),
        name="ada",
    )(c_all, w_ada, b_ada.reshape(1, n))


def _modmm_kernel(x_ref, sc_ref, sh_ref, w_ref, o_ref, u_ref, *, sigmoid_out):
    @pl.when(pl.program_id(2) == 0)
    def _():
        u_ref[...] = (x_ref[0] * (1.0 + sc_ref[0]) + sh_ref[0]).astype(BF16)

    y = _mm(u_ref[...], w_ref[...])
    if sigmoid_out:
        y = _sigmoid(y)
    o_ref[0] = y.astype(o_ref.dtype)


def _modmm(x, sc, sh, w_bf16, tn, out_dtype=F32, sigmoid_out=False, name="proj"):
    b, t, d = x.shape
    n = w_bf16.shape[1]
    tm = min(512, t)
    return pl.pallas_call(
        functools.partial(_modmm_kernel, sigmoid_out=sigmoid_out),
        grid=(b, t // tm, n // tn),
        in_specs=[pl.BlockSpec((1, tm, d), lambda bi, i, j: (bi, i, 0)),
                  pl.BlockSpec((1, 1, d), lambda bi, i, j: (bi, 0, 0)),
                  pl.BlockSpec((1, 1, d), lambda bi, i, j: (bi, 0, 0)),
                  pl.BlockSpec((d, tn), lambda bi, i, j: (0, j))],
        out_specs=pl.BlockSpec((1, tm, tn), lambda bi, i, j: (bi, i, j)),
        out_shape=jax.ShapeDtypeStruct((b, t, n), out_dtype),
        scratch_shapes=[pltpu.VMEM((tm, d), BF16)],
        compiler_params=_cparams("arbitrary", "arbitrary", "arbitrary"),
        name=name,
    )(x, sc, sh, w_bf16)


def _sb_kernel(qi_ref, kb_ref, first_ref, last_ref, diag_ref, q_ref, k_ref, v_ref, o_ref, *scratch,
               tq, tk, past, nl):
    accs, cars, done_ref = scratch[:2 * nl], scratch[2 * nl:4 * nl], scratch[4 * nl]
    p = pl.program_id(2)
    qi = qi_ref[p]
    kb = kb_ref[p]

    @pl.when(first_ref[p] == 1)
    def _():
        for ref in scratch[:4 * nl]:
            ref[...] = jnp.zeros_like(ref)
        done_ref[0] = 0

    later = (lax.broadcasted_iota(I32, (tk, tk), 0) > lax.broadcasted_iota(I32, (tk, tk), 1)).astype(BF16)
    lane = lax.broadcasted_iota(I32, (tq, LANES), 1)

    def sweep(masked):
        if masked:
            pos = past + qi * tq + lax.broadcasted_iota(I32, (tq, tk), 0)
            kpos = kb * tk + lax.broadcasted_iota(I32, (tq, tk), 1)
            mask = kpos < pos
        for l in range(nl):
            cols = slice(l * LANES, (l + 1) * LANES)
            q = q_ref[0, :, cols] * (HEAD_DIM ** -0.5 * LOG2E)
            kbf = k_ref[0, :, cols].astype(BF16)
            vbf = v_ref[0, :, cols].astype(BF16)
            for hh in range(2):
                head = (lane < HEAD_DIM) if hh == 0 else (lane >= HEAD_DIM)
                z = _mm(jnp.where(head, q, 0.0).astype(BF16), kbf, NT)
                soft = jnp.log(1.0 + jnp.exp2(-jnp.abs(z))) * LOG2E
                log_beta = jnp.minimum(z, 0.0) - soft
                log_keep = log_beta - z
                if masked:
                    log_keep = jnp.where(mask, log_keep, 0.0)
                car = cars[2 * l + hh]
                after = _mm(log_keep.astype(BF16), later) + car[...]
                w = jnp.exp2(log_beta + after)
                if masked:
                    w = jnp.where(mask, w, 0.0)
                accs[2 * l + hh][...] += _mm(w.astype(BF16), vbf)
                car[...] += jnp.sum(log_keep, axis=1, keepdims=True)
        worst = cars[0][...]
        for car in cars[1:]:
            worst = jnp.maximum(worst, car[...])
        done_ref[0] = (jnp.max(worst) < F32_UNDERFLOW_LOG2).astype(I32)

    live = done_ref[0] == 0

    @pl.when(live & (diag_ref[p] == 1))
    def _():
        sweep(True)

    @pl.when(live & (diag_ref[p] == 0))
    def _():
        sweep(False)

    @pl.when(last_ref[p] == 1)
    def _():
        for l in range(nl):
            o_ref[0, :, l * LANES:(l + 1) * LANES] = jnp.where(lane < HEAD_DIM, accs[2 * l][...], accs[2 * l + 1][...])


def _sb_attention(q, k_all, v_all, past, tq, tk, nl):
    b, t, _ = q.shape
    nq = t // tq
    qi_l, kb_l, first_l, last_l, diag_l = [], [], [], [], []
    for qi in range(nq):
        top = (past + (qi + 1) * tq - 2) // tk
        for kb in range(top, -1, -1):
            qi_l.append(qi)
            kb_l.append(kb)
            first_l.append(int(kb == top))
            last_l.append(int(kb == 0))
            diag_l.append(int((kb + 1) * tk - 1 >= past + qi * tq))
    tabs = [jnp.asarray(np.asarray(a, np.int32)) for a in (qi_l, kb_l, first_l, last_l, diag_l)]
    ng = WIDTH // (nl * LANES)
    q_map = lambda bi, g, p, qi, kb, f, la, dg: (bi, qi[p], g)
    k_map = lambda bi, g, p, qi, kb, f, la, dg: (bi, kb[p], g)
    grid_spec = pltpu.PrefetchScalarGridSpec(
        num_scalar_prefetch=5,
        grid=(b, ng, len(qi_l)),
        in_specs=[pl.BlockSpec((1, tq, nl * LANES), q_map),
                  pl.BlockSpec((1, tk, nl * LANES), k_map),
                  pl.BlockSpec((1, tk, nl * LANES), k_map)],
        out_specs=pl.BlockSpec((1, tq, nl * LANES), q_map),
        scratch_shapes=[pltpu.VMEM((tq, LANES), F32)] * (2 * nl) + [pltpu.VMEM((tq, 1), F32)] * (2 * nl)
        + [pltpu.SMEM((1,), I32)],
    )
    return pl.pallas_call(
        functools.partial(_sb_kernel, tq=tq, tk=tk, past=past, nl=nl),
        grid_spec=grid_spec,
        out_shape=jax.ShapeDtypeStruct((b, t, WIDTH), F32),
        compiler_params=_cparams("arbitrary", "arbitrary", "arbitrary"),
        name="sb_attention",
    )(*tabs, q, k_all, v_all)


def _head_sum(x, same_head):
    return jnp.concatenate(
        [_dot_exact_rhs(x[:, c * LANES:(c + 1) * LANES], same_head) for c in range(WIDTH // LANES)], axis=1)


def _rw_prep_kernel(m_ref, sh0_ref, mu_ref, w0_ref, w2_ref, a0_ref, a2_ref, g2_ref, kk_ref, ka_ref, rk_ref,
                    wt_ref, ut_ref, rt_ref, mrb_ref, yv_ref, bt_ref, kt_ref, v_ref, bv_ref, g_ref, pend_ref, prev_ref,
                    *, tr):
    @pl.when(pl.program_id(1) == 0)
    def _():
        prev_ref[...] = sh0_ref[0]

    m = m_ref[0]
    row = lax.broadcasted_iota(I32, m.shape, 0)
    m_prev = jnp.where(row == 0, prev_ref[...], pltpu.roll(m, 1, 0))
    prev_ref[...] = m[tr - 1:tr, :]
    ms = m + (m_prev - m) * mu_ref[...]
    r = ms[:, 0:WIDTH]
    k = ms[:, WIDTH:2 * WIDTH]
    v = ms[:, 2 * WIDTH:3 * WIDTH]
    lora_wa = ms[:, 3 * WIDTH:3 * WIDTH + LANES]
    lora_g = ms[:, 3 * WIDTH + LANES:RW_PAD]

    w_log = -_softplus(-(w0_ref[...] + _dot3(jnp.tanh(lora_wa), w2_ref[...]))) - 0.5
    log_decay = -jnp.exp(w_log)
    a = _sigmoid(a0_ref[...] + _dot3(lora_wa, a2_ref[...]))
    g = _dot3(_sigmoid(lora_g), g2_ref[...])

    same_head = (lax.broadcasted_iota(I32, (LANES, LANES), 0) // HEAD_DIM ==
                 lax.broadcasted_iota(I32, (LANES, LANES), 1) // HEAD_DIM).astype(BF16)
    kk = k * kk_ref[...]
    kk = kk / jnp.maximum(jnp.sqrt(_head_sum(kk * kk, same_head)), 1e-12)
    kh = k * (1.0 + (a - 1.0) * ka_ref[...])
    bonus = _head_sum(r * kh * rk_ref[...], same_head)

    ti = lax.broadcasted_iota(I32, (tr, tr), 0)
    si = lax.broadcasted_iota(I32, (tr, tr), 1)
    tri = (si <= ti).astype(BF16)
    parts = _split3(log_decay)
    cl = _mm(tri, parts[0]) + (_mm(tri, parts[1]) + _mm(tri, parts[2]))
    ec = jnp.exp(cl)
    inv = jnp.exp(-cl)
    rt = r * ec
    kt = kh * inv
    at = -kk * jnp.exp(cl - log_decay)
    bt = kk * a * inv
    rt_ref[0] = rt
    kt_ref[0] = kt
    bt_ref[0] = bt
    v_ref[0] = v
    bv_ref[0] = bonus * v
    g_ref[0] = g
    pend_ref[0, 0] = ec[tr - 1:tr, :]

    lane = lax.broadcasted_iota(I32, (tr, LANES), 1)
    first = lane < HEAD_DIM
    t_row = lax.broadcasted_iota(I32, (tr, LANES), 0)
    s_col = lane % HEAD_DIM
    strict = s_col < t_row
    incl = s_col <= t_row
    eye2 = (s_col == t_row).astype(F32)

    def stack2(x):
        return jnp.concatenate([jnp.where(first, x, 0.0), jnp.where(first, 0.0, x)], axis=0)

    pairs = range(WIDTH // LANES)
    col = lambda hp: slice(hp * LANES, (hp + 1) * LANES)
    a_t = [at[:, col(hp)] for hp in pairs]
    v_s = [stack2(v[:, col(hp)]) for hp in pairs]
    l_ab, l_ak = [], []
    for hp in pairs:
        b_s, k_s, r_t = stack2(bt[:, col(hp)]), stack2(kt[:, col(hp)]), rt[:, col(hp)]
        l_ab.append(jnp.where(strict, _dot3(a_t[hp], b_s, NT), 0.0))
        l_ak.append(jnp.where(strict, _dot3(a_t[hp], k_s, NT), 0.0))
        mrb_ref[0, :, col(hp)] = jnp.where(incl, _dot3(r_t, b_s, NT), 0.0)
        yv_ref[0, :, col(hp)] = _dot3(jnp.where(incl, _dot3(r_t, k_s, NT), 0.0), v_s[hp])
    tinv = [eye2 + l for l in l_ab]
    pw = l_ab
    for _ in range(5):
        pw = [_dot3(x, stack2(x)) for x in pw]
        tinv = [t_ + _dot3(t_, stack2(x)) for t_, x in zip(tinv, pw)]
    g_v = [_dot3(l_ak[hp], v_s[hp]) for hp in pairs]
    for hp in pairs:
        wt_ref[0, :, col(hp)] = _dot3(tinv[hp], stack2(a_t[hp]))
        ut_ref[0, :, col(hp)] = _dot3(tinv[hp], stack2(g_v[hp]))


def _rw_prep(m, shift0, rwp):
    b, t, _ = m.shape
    tr = CHUNK
    row_spec = pl.BlockSpec((1, tr, WIDTH), lambda bi, i: (bi, i, 0))
    vec = lambda n: pl.BlockSpec((1, n), lambda bi, i: (0, 0))
    mat = lambda r: pl.BlockSpec((r, WIDTH), lambda bi, i: (0, 0))
    out = jax.ShapeDtypeStruct((b, t, WIDTH), F32)
    return pl.pallas_call(
        functools.partial(_rw_prep_kernel, tr=tr),
        grid=(b, t // tr),
        in_specs=[pl.BlockSpec((1, tr, RW_PAD), lambda bi, i: (bi, i, 0)),
                  pl.BlockSpec((1, 1, RW_PAD), lambda bi, i: (bi, 0, 0)),
                  vec(RW_PAD), vec(WIDTH), mat(LANES), vec(WIDTH), mat(LANES), mat(RW_PAD - 3 * WIDTH - LANES),
                  vec(WIDTH), vec(WIDTH), vec(WIDTH)],
        out_specs=[row_spec] * 10 + [pl.BlockSpec((1, 1, 1, WIDTH), lambda bi, i: (bi, i, 0, 0))],
        out_shape=[out] * 10 + [jax.ShapeDtypeStruct((b, t // tr, 1, WIDTH), F32)],
        scratch_shapes=[pltpu.VMEM((1, RW_PAD), F32)],
        compiler_params=_cparams("arbitrary", "arbitrary"),
        name="rwkv_prep",
    )(m, shift0, rwp["mu"], rwp["w0"], rwp["w2"], rwp["a0"], rwp["a2"], rwp["g2"], rwp["k_k"], rwp["k_a"],
      rwp["r_k"])


def _rw_scan_kernel(wt_ref, ut_ref, rt_ref, mrb_ref, yv_ref, bt_ref, kt_ref, v_ref, bv_ref, g_ref, pend_ref,
                    s0_ref, lnw_ref, lnb_ref, o_ref, sout_ref, *s_refs, n_chunks, npair):
    ci = pl.program_id(2)

    @pl.when(ci == 0)
    def _():
        for pi in range(npair):
            s_refs[pi][...] = s0_ref[0, pi]

    c = CHUNK
    first = lax.broadcasted_iota(I32, (c, LANES), 1) < HEAD_DIM
    vi = lax.broadcasted_iota(I32, (LANES, LANES), 0)
    kj = lax.broadcasted_iota(I32, (LANES, LANES), 1)
    same_head_b = (vi // HEAD_DIM) == (kj // HEAD_DIM)
    same_head = same_head_b.astype(BF16)

    def stack2(x):
        return jnp.concatenate([jnp.where(first, x, 0.0), jnp.where(first, 0.0, x)], axis=0)

    def chunk(j, carry):
        rows = pl.ds(pl.multiple_of(j * c, c), c)
        p_all = pend_ref[0, j]
        pairs = range(npair)
        col = lambda pi: slice(pi * LANES, (pi + 1) * LANES)
        s = [s_refs[pi][...] for pi in pairs]
        u = [_dot3(wt_ref[0, rows, col(pi)], s[pi], NT) + ut_ref[0, rows, col(pi)] for pi in pairs]
        for pi in pairs:
            ds = _dot3(jnp.concatenate([u[pi], v_ref[0, rows, col(pi)]], axis=0),
                       jnp.concatenate([bt_ref[0, rows, col(pi)], kt_ref[0, rows, col(pi)]], axis=0), TN)
            s_refs[pi][...] = (s[pi] + jnp.where(same_head_b, ds, 0.0)) * p_all[:, col(pi)]
        y = [_dot3(rt_ref[0, rows, col(pi)], s[pi], NT) + (_dot3(mrb_ref[0, rows, col(pi)], stack2(u[pi]))
                                                           + yv_ref[0, rows, col(pi)]) for pi in pairs]
        for pi in pairs:
            mu = _dot_exact_rhs(y[pi], same_head) * (1.0 / HEAD_DIM)
            dlt = y[pi] - mu
            var = _dot_exact_rhs(dlt * dlt, same_head) * (1.0 / HEAD_DIM)
            yn = dlt * lax.rsqrt(var + GN_EPS) * lnw_ref[:, col(pi)] + lnb_ref[:, col(pi)]
            o_ref[0, rows, col(pi)] = (yn + bv_ref[0, rows, col(pi)]) * g_ref[0, rows, col(pi)]
        return carry

    lax.fori_loop(0, n_chunks, chunk, 0)

    @pl.when(ci == pl.num_programs(2) - 1)
    def _():
        for pi in range(npair):
            sout_ref[0, pi] = s_refs[pi][...]


def _rw_scan(prep, s0_bd, lnw, lnb, npair=4):
    b, t, _ = prep[0].shape
    tc = min(512, t)
    n_chunks = tc // CHUNK
    ng = WIDTH // (npair * LANES)
    row_spec = pl.BlockSpec((1, tc, npair * LANES), lambda bi, h, i: (bi, i, h))
    pend_spec = pl.BlockSpec((1, n_chunks, 1, npair * LANES), lambda bi, h, i: (bi, i, 0, h))
    st_spec = pl.BlockSpec((1, npair, LANES, LANES), lambda bi, h, i: (bi, h, 0, 0))
    vec_spec = pl.BlockSpec((1, npair * LANES), lambda bi, h, i: (0, h))
    return pl.pallas_call(
        functools.partial(_rw_scan_kernel, n_chunks=n_chunks, npair=npair),
        grid=(b, ng, t // tc),
        in_specs=[row_spec] * 10 + [pend_spec, st_spec, vec_spec, vec_spec],
        out_specs=[row_spec, st_spec],
        out_shape=[jax.ShapeDtypeStruct((b, t, WIDTH), F32),
                   jax.ShapeDtypeStruct((b, WIDTH // LANES, LANES, LANES), F32)],
        scratch_shapes=[pltpu.VMEM((LANES, LANES), F32)] * npair,
        compiler_params=_cparams("arbitrary", "arbitrary", "arbitrary"),
        name="rwkv_scan",
    )(*prep, s0_bd, lnw, lnb)


def _merge_kernel(oa_ref, ob_ref, sg_ref, x_ref, g1_ref, sc2_ref, sh2_ref, pa_ref, pb_ref, wo_ref, lng_ref, lnb_ref,
                  wr_ref, br_ref, h_ref, u2_ref, route_ref):
    sg = sg_ref[0]
    ya = _mm(oa_ref[0].astype(BF16), pa_ref[...])
    yb = _mm(ob_ref[0].astype(BF16), pb_ref[...])
    merged = sg[:, :D_MODEL].astype(F32) * ya + sg[:, D_MODEL:].astype(F32) * yb
    mix = _mm(merged.astype(BF16), wo_ref[...])
    hin = ALPHA * x_ref[0] + g1_ref[0] * mix
    mu = jnp.mean(hin, axis=-1, keepdims=True)
    dlt = hin - mu
    var = jnp.mean(dlt * dlt, axis=-1, keepdims=True)
    h = dlt * lax.rsqrt(var + LN_EPS) * lng_ref[...] + lnb_ref[...]
    h_ref[0] = h
    u2 = h * (1.0 + sc2_ref[0]) + sh2_ref[0]
    u2_ref[0] = u2

    logits = _dot3(u2, wr_ref[...]) + br_ref[...]
    lane_i = lax.broadcasted_iota(I32, logits.shape, 1)
    lane = lane_i.astype(F32)
    neg = -jnp.inf
    first_at = lambda vals, top: jnp.min(jnp.where(vals == top, lane, float(LANES)), axis=1, keepdims=True)
    lg = jnp.where(lane_i < N_GROUPS, logits, neg)
    mg = jnp.max(lg, axis=1, keepdims=True)
    g_idx = first_at(lg, mg)
    p_g = 1.0 / jnp.sum(jnp.exp(lg - mg), axis=1, keepdims=True)
    lo_lane = N_GROUPS + PER_GROUP * g_idx
    le = jnp.where((lane >= lo_lane) & (lane < lo_lane + PER_GROUP), logits, neg)
    m1 = jnp.max(le, axis=1, keepdims=True)
    i1 = first_at(le, m1)
    le2 = jnp.where(lane == i1, neg, le)
    m2 = jnp.max(le2, axis=1, keepdims=True)
    i2 = first_at(le2, m2)
    ratio = jnp.exp(m2 - m1)
    w1 = p_g / (1.0 + ratio)
    w2 = p_g * ratio / (1.0 + ratio)
    route = jnp.where(lane_i == 0, i1 - N_GROUPS,
                      jnp.where(lane_i == 1, i2 - N_GROUPS,
                                jnp.where(lane_i == 2, w1, jnp.where(lane_i == 3, w2, 0.0))))
    route_ref[0] = route


def _merge(o_a, o_b, sg, x, g1, sc2, sh2, wts):
    b, t, d = x.shape
    tm = min(256, t)
    const = lambda shape: pl.BlockSpec(shape, lambda bi, i: (0,) * len(shape), pipeline_mode=pl.Buffered(1))
    seq = pl.BlockSpec((1, 1, d), lambda bi, i: (bi, 0, 0))
    row = lambda n: pl.BlockSpec((1, tm, n), lambda bi, i: (bi, i, 0))
    return pl.pallas_call(
        _merge_kernel,
        grid=(b, t // tm),
        in_specs=[row(WIDTH), row(WIDTH), row(2 * d), row(d), seq, seq, seq,
                  const((WIDTH, d)), const((WIDTH, d)), const((d, d)), const((1, d)), const((1, d)),
                  const((d, LANES)), const((1, LANES))],
        out_specs=[row(d), row(d), row(LANES)],
        out_shape=[jax.ShapeDtypeStruct((b, t, d), F32), jax.ShapeDtypeStruct((b, t, d), F32),
                   jax.ShapeDtypeStruct((b, t, LANES), F32)],
        compiler_params=_cparams("arbitrary", "arbitrary"),
        name="merge_route",
    )(o_a, o_b, sg, x, g1, sc2, sh2, wts["p_a"], wts["p_b"], wts["w_out"], wts["ln1_g"], wts["ln1_b"],
      wts["w_route"], wts["b_route"])


def _moe_kernel(be_ref, nused_ref, tok_ref, x_hbm, wg_ref, wu_ref, wd_ref, y_ref, xbuf, sem, *, tb):
    i = pl.program_id(0)

    @pl.when(i < nused_ref[0])
    def _():
        def row_copy(r):
            return pltpu.make_async_copy(x_hbm.at[pl.ds(tok_ref[i * tb + r], 1)], xbuf.at[pl.ds(r, 1)], sem)

        def start(r, carry):
            row_copy(r).start()
            return carry

        def wait(r, carry):
            row_copy(r).wait()
            return carry

        lax.fori_loop(0, tb, start, 0)
        lax.fori_loop(0, tb, wait, 0)
        x = xbuf[...].astype(BF16)
        hg = _mm(x, wg_ref[0])
        hu = _mm(x, wu_ref[0])
        hid = (hg * _sigmoid(hg) * hu).astype(BF16)
        y_ref[...] = _mm(hid, wd_ref[0])

    @pl.when(i >= nused_ref[0])
    def _():
        y_ref[...] = jnp.zeros_like(y_ref)


def _moe(block_expert, n_used, slot_token, x_all, wg, wu, wd, tb):
    n_blocks = block_expert.shape[0]
    d = x_all.shape[1]
    grid_spec = pltpu.PrefetchScalarGridSpec(
        num_scalar_prefetch=3,
        grid=(n_blocks,),
        in_specs=[pl.BlockSpec(memory_space=pl.ANY),
                  pl.BlockSpec((1, d, D_EXPERT), lambda i, be, nu, tok: (be[i], 0, 0)),
                  pl.BlockSpec((1, d, D_EXPERT), lambda i, be, nu, tok: (be[i], 0, 0)),
                  pl.BlockSpec((1, D_EXPERT, d), lambda i, be, nu, tok: (be[i], 0, 0))],
        out_specs=pl.BlockSpec((tb, d), lambda i, be, nu, tok: (i, 0)),
        scratch_shapes=[pltpu.VMEM((tb, d), F32), pltpu.SemaphoreType.DMA(())],
    )
    return pl.pallas_call(
        functools.partial(_moe_kernel, tb=tb),
        grid_spec=grid_spec,
        out_shape=jax.ShapeDtypeStruct((n_blocks * tb, d), F32),
        compiler_params=_cparams("arbitrary"),
        name="moe_experts",
    )(block_expert, n_used, slot_token, x_all, wg, wu, wd)


def _final_kernel(dest_ref, y_hbm, h_ref, rw_ref, g2_ref, lng_ref, lnb_ref, o_ref, ybuf, sem, *, tm, per_seq):
    base = (pl.program_id(0) * per_seq + pl.program_id(1)) * tm

    def row_copy(r, k):
        return pltpu.make_async_copy(y_hbm.at[pl.ds(dest_ref[2 * (base + r) + k], 1)], ybuf.at[k, pl.ds(r, 1)],
                                     sem.at[k])

    def start(r, carry):
        row_copy(r, 0).start()
        row_copy(r, 1).start()
        return carry

    def wait(r, carry):
        row_copy(r, 0).wait()
        row_copy(r, 1).wait()
        return carry

    lax.fori_loop(0, tm, start, 0)
    lax.fori_loop(0, tm, wait, 0)
    rw = rw_ref[0]
    moe = rw[:, 2:3] * ybuf[0] + rw[:, 3:4] * ybuf[1]
    xin = ALPHA * h_ref[0] + g2_ref[0] * moe
    mu = jnp.mean(xin, axis=-1, keepdims=True)
    dlt = xin - mu
    var = jnp.mean(dlt * dlt, axis=-1, keepdims=True)
    o_ref[0] = dlt * lax.rsqrt(var + LN_EPS) * lng_ref[...] + lnb_ref[...]


def _final(dest, y_sorted, h, route, g2, ln2_g, ln2_b):
    b, t, d = h.shape
    tm = min(256, t)
    per_seq = t // tm
    grid_spec = pltpu.PrefetchScalarGridSpec(
        num_scalar_prefetch=1,
        grid=(b, per_seq),
        in_specs=[pl.BlockSpec(memory_space=pl.ANY),
                  pl.BlockSpec((1, tm, d), lambda bi, i, de: (bi, i, 0)),
                  pl.BlockSpec((1, tm, LANES), lambda bi, i, de: (bi, i, 0)),
                  pl.BlockSpec((1, 1, d), lambda bi, i, de: (bi, 0, 0)),
                  pl.BlockSpec((1, d), lambda bi, i, de: (0, 0)),
                  pl.BlockSpec((1, d), lambda bi, i, de: (0, 0))],
        out_specs=pl.BlockSpec((1, tm, d), lambda bi, i, de: (bi, i, 0)),
        scratch_shapes=[pltpu.VMEM((2, tm, d), F32), pltpu.SemaphoreType.DMA((2,))],
    )
    return pl.pallas_call(
        functools.partial(_final_kernel, tm=tm, per_seq=per_seq),
        grid_spec=grid_spec,
        out_shape=jax.ShapeDtypeStruct((b, t, d), F32),
        compiler_params=_cparams("arbitrary", "arbitrary"),
        name="combine_ln2",
    )(dest, y_sorted, h, route, g2, ln2_g, ln2_b)


def _pad_rows(a, rows):
    return jnp.pad(a, ((0, rows - a.shape[0]), (0, 0)))


def _mixers(x, ada, past_k, past_v, s0, shift0, wts, sb_tiles):
    b, t, d = x.shape
    sh1, sc1, g1, sh2, sc2, g2 = [a[:, None, :] for a in jnp.split(ada, 6, axis=-1)]
    q = _modmm(x, sc1, sh1, wts["w_q"], 1024, name="proj_q")
    k = _modmm(x, sc1, sh1, wts["w_k"], 1024, name="proj_k")
    v = _modmm(x, sc1, sh1, wts["w_v"], 1024, name="proj_v")
    m = _modmm(x, sc1, sh1, wts["w_m"], RW_PAD // 3, name="proj_rwkv")
    sg = _modmm(x, sc1, sh1, wts["w_g"], 1024, out_dtype=BF16, sigmoid_out=True, name="proj_gates")

    tq, tk, nl = sb_tiles
    past = past_k.shape[1]
    s_pad = -(-(past + t) // tk) * tk
    tail = jnp.zeros((b, s_pad - past - t, WIDTH), F32)
    k_all = jnp.concatenate([past_k.reshape(b, past, WIDTH), k, tail], axis=1)
    v_all = jnp.concatenate([past_v.reshape(b, past, WIDTH), v, tail], axis=1)
    o_a = _sb_attention(q, k_all, v_all, past, tq, tk, nl)

    shift0_p = jnp.pad(shift0, ((0, 0), (0, 0), (0, RW_PAD - RW_COLS)))
    prep = _rw_prep(m, shift0_p, wts["rw"])
    nhp = WIDTH // LANES
    s0_pairs = s0.reshape(b, nhp, 2, HEAD_DIM, HEAD_DIM)
    s0_bd = jnp.zeros((b, nhp, LANES, LANES), F32)
    s0_bd = s0_bd.at[:, :, :HEAD_DIM, :HEAD_DIM].set(s0_pairs[:, :, 0]).at[:, :, HEAD_DIM:, HEAD_DIM:].set(s0_pairs[:, :, 1])
    o_b, s_bd = _rw_scan(prep, s0_bd, wts["rw"]["lnx_w"], wts["rw"]["lnx_b"])
    s_new = jnp.stack([s_bd[:, :, :HEAD_DIM, :HEAD_DIM], s_bd[:, :, HEAD_DIM:, HEAD_DIM:]], axis=2)
    s_new = s_new.reshape(b, 2 * nhp, HEAD_DIM, HEAD_DIM)
    shift_new = m[:, t - 1:t, :RW_COLS]

    h, u2, route = _merge(o_a, o_b, sg, x, g1, sc2, sh2, wts)
    return h, u2, route, g2, k, v, s_new, shift_new


def kernel(x_prompt, x_sample, cache_sb_k, cache_sb_v, state_rwkv, state_rwkv_shift, c_prompt, c_sample, w_ada, b_ada, w_in, tokshift_mu, rw_w0, rw_w2, rw_a0, rw_a2, rw_g2, rw_k_k, rw_k_a, rw_r_k, rw_lnx_w, rw_lnx_b, w_branch_a, w_branch_b, w_out, ln1_g, ln1_b, w_router_group, b_router_group, w_router_expert, b_router_expert, w_exp_gate, w_exp_up, w_exp_down, ln2_g, ln2_b):
    depth = w_ada.shape[0]
    assert depth == 1
    bp, tp, d = x_prompt.shape
    bs, ts, _ = x_sample.shape
    l = 0

    w_in_l = w_in[l]
    w_rw = w_in_l[:, 3 * WIDTH:3 * WIDTH + RW_COLS]
    row_vec = lambda a: a.reshape(1, -1)
    pad_cols = lambda a, n: jnp.pad(a, ((0, 0), (0, n - a.shape[1])))
    n_lg = RW_PAD - 3 * WIDTH - LANES
    wts = {
        "w_q": w_in_l[:, 0:WIDTH].astype(BF16),
        "w_k": w_in_l[:, WIDTH:2 * WIDTH].astype(BF16),
        "w_v": w_in_l[:, 2 * WIDTH:3 * WIDTH].astype(BF16),
        "w_m": pad_cols(w_rw, RW_PAD).astype(BF16),
        "w_g": w_in_l[:, 3 * WIDTH + RW_COLS:].astype(BF16),
        "p_a": w_branch_a[l].astype(BF16),
        "p_b": w_branch_b[l].astype(BF16),
        "w_out": w_out[l].astype(BF16),
        "ln1_g": row_vec(ln1_g[l]),
        "ln1_b": row_vec(ln1_b[l]),
        "w_route": pad_cols(jnp.concatenate([w_router_group[l], w_router_expert[l]], axis=1), LANES),
        "b_route": pad_cols(row_vec(jnp.concatenate([b_router_group[l], b_router_expert[l]])), LANES),
        "rw": {
            "mu": pad_cols(row_vec(tokshift_mu[l]), RW_PAD),
            "w0": row_vec(rw_w0[l]),
            "w2": _pad_rows(rw_w2[l], LANES),
            "a0": row_vec(rw_a0[l]),
            "a2": jnp.concatenate([jnp.zeros((LORA_W, WIDTH), F32), rw_a2[l]], axis=0),
            "g2": _pad_rows(rw_g2[l], n_lg),
            "k_k": row_vec(rw_k_k[l]),
            "k_a": row_vec(rw_k_a[l]),
            "r_k": row_vec(rw_r_k[l]),
            "lnx_w": row_vec(rw_lnx_w[l]),
            "lnx_b": row_vec(rw_lnx_b[l]),
        },
    }

    rows = -(-(bp + bs) // 8) * 8
    c_all = _pad_rows(jnp.concatenate([c_prompt, c_sample], axis=0), rows)
    ada = _ada(c_all, w_ada[l], b_ada[l])

    empty = jnp.zeros((bp, 0, WIDTH), F32)
    zero_state = jnp.zeros((bp, WIDTH // HEAD_DIM, HEAD_DIM, HEAD_DIM), F32)
    zero_shift = jnp.zeros((bp, 1, RW_COLS), F32)
    hp, u2p, route_p, g2p, kp, vp, sp_, shp = _mixers(
        x_prompt, ada[:bp], empty, empty, zero_state, zero_shift, wts, (min(256, tp), 256, 4))
    hs, u2s, route_s, g2s, ks, vs, ss_, shs = _mixers(
        x_sample, ada[bp:bp + bs], cache_sb_k[l], cache_sb_v[l], state_rwkv[l], state_rwkv_shift[l], wts,
        (min(256, ts), 256, 4))

    n_p, n_s = bp * tp, bs * ts
    n_tok = n_p + n_s
    tb = 256
    route_all = jnp.concatenate([route_p.reshape(n_p, LANES), route_s.reshape(n_s, LANES)], axis=0)
    flat_e = route_all[:, 0:2].astype(I32).reshape(n_tok * 2)
    onehot = (flat_e[:, None] == jnp.arange(N_EXPERTS, dtype=I32)[None, :]).astype(I32)
    csum = jnp.cumsum(onehot, axis=0)
    rank = jnp.take_along_axis(csum, flat_e[:, None], axis=1)[:, 0] - 1
    counts = csum[-1]
    padded = (counts + tb - 1) // tb * tb
    pend = jnp.cumsum(padded)
    pstart = pend - padded
    dest = (pstart[flat_e] + rank).astype(I32)
    n_blocks = (2 * n_tok + N_EXPERTS * (tb - 1)) // tb + 1
    slot_token = jnp.zeros((n_blocks * tb,), I32).at[dest].set(jnp.arange(2 * n_tok, dtype=I32) // 2)
    block_start = jnp.arange(n_blocks, dtype=I32) * tb
    block_expert = jnp.minimum(jnp.searchsorted(pend, block_start, side="right"), N_EXPERTS - 1).astype(I32)
    n_used = (pend[-1] // tb).astype(I32).reshape(1)

    u2_all = jnp.concatenate([u2p.reshape(n_p, d), u2s.reshape(n_s, d)], axis=0)
    y_sorted = _moe(block_expert, n_used, slot_token, u2_all, w_exp_gate[l].astype(BF16),
                    w_exp_up[l].astype(BF16), w_exp_down[l].astype(BF16), tb)

    ln2g, ln2b = row_vec(ln2_g[l]), row_vec(ln2_b[l])
    y_prompt = _final(dest[:2 * n_p], y_sorted, hp, route_p, g2p, ln2g, ln2b)
    y_sample = _final(dest[2 * n_p:], y_sorted, hs, route_s, g2s, ln2g, ln2b)

    heads = lambda a: a.reshape(1, a.shape[0], a.shape[1], WIDTH // HEAD_DIM, HEAD_DIM)
    return (y_prompt, y_sample, heads(kp), heads(vp), sp_[None], shp[None],
            heads(ks), heads(vs), ss_[None], shs[None])
```

```python
import functools

import jax
import jax.numpy as jnp
import numpy as np
from jax import lax
from jax.experimental import pallas as pl
from jax.experimental.pallas import tpu as pltpu

F32 = jnp.float32
BF16 = jnp.bfloat16
I32 = jnp.int32

D_MODEL = 2048
HEAD_DIM = 64
WIDTH = 1024
LORA_W = 64
LORA_A = 64
LORA_G = 160
RW_COLS = 3 * WIDTH + LORA_W + LORA_A + LORA_G
RW_PAD = 3456
N_GROUPS = 4
PER_GROUP = 8
N_EXPERTS = 32
D_EXPERT = 1024
LN_EPS = 1e-5
GN_EPS = 64e-5
ALPHA = 2.0 ** 0.25
LOG2E = 1.4426950408889634
F32_UNDERFLOW_LOG2 = -150.0
LANES = 128
CHUNK = 64
VMEM_LIMIT = 56 * 1024 * 1024

NN = (((1,), (0,)), ((), ()))
NT = (((1,), (1,)), ((), ()))
TN = (((0,), (0,)), ((), ()))


def _cparams(*sem):
    return pltpu.CompilerParams(dimension_semantics=sem, vmem_limit_bytes=VMEM_LIMIT)


def _mm(a, b, dims=NN):
    return lax.dot_general(a, b, dims, preferred_element_type=F32)


def _split(x):
    hi = x.astype(BF16)
    lo = (x - hi.astype(F32)).astype(BF16)
    return hi, lo


def _split3(x):
    hi = x.astype(BF16)
    r1 = x - hi.astype(F32)
    mid = r1.astype(BF16)
    lo = (r1 - mid.astype(F32)).astype(BF16)
    return hi, mid, lo


def _dot3(a, b, dims=NN):
    ah, al = _split(a)
    bh, bl = _split(b)
    return _mm(ah, bh, dims) + (_mm(al, bh, dims) + _mm(ah, bl, dims))


def _dot1(a, b, dims=NN):
    return _mm(a.astype(BF16), b.astype(BF16), dims)


def _dot_exact_rhs(a, b_bf16, terms=2):
    parts = _split(a) if terms == 2 else _split3(a)
    out = _mm(parts[0], b_bf16)
    for p in parts[1:]:
        out = out + _mm(p, b_bf16)
    return out


def _softplus(z):
    return jnp.maximum(z, 0.0) + jnp.log(1.0 + jnp.exp(-jnp.abs(z)))


def _sigmoid(z):
    return 1.0 / (1.0 + jnp.exp(-z))


def _ada_kernel(c_ref, w_ref, b_ref, o_ref):
    c = c_ref[...]
    o_ref[...] = _dot3(c * _sigmoid(c), w_ref[...]) + b_ref[...]


def _ada(c_all, w_ada, b_ada):
    rows = c_all.shape[0]
    n = w_ada.shape[1]
    tn = 1024
    return pl.pallas_call(
        _ada_kernel,
        grid=(n // tn,),
        in_specs=[pl.BlockSpec((rows, D_MODEL), lambda j: (0, 0)),
                  pl.BlockSpec((D_MODEL, tn), lambda j: (0, j)),
                  pl.BlockSpec((1, tn), lambda j: (0, j))],
        out_specs=pl.BlockSpec((rows, tn), lambda j: (0, j)),
        out_shape=jax.ShapeDtypeStruct((rows, n), F32),
        compiler_params=_cparams("arbitrary"),
        name="ada",
    )(c_all, w_ada, b_ada.reshape(1, n))


def _modmm_kernel(x_ref, sc_ref, sh_ref, w_ref, o_ref, u_ref, *, sigmoid_out):
    @pl.when(pl.program_id(2) == 0)
    def _():
        u_ref[...] = (x_ref[0] * (1.0 + sc_ref[0]) + sh_ref[0]).astype(BF16)

    y = _mm(u_ref[...], w_ref[...])
    if sigmoid_out:
        y = _sigmoid(y)
    o_ref[0] = y.astype(o_ref.dtype)


def _modmm(x, sc, sh, w_bf16, tn, out_dtype=F32, sigmoid_out=False, name="proj"):
    b, t, d = x.shape
    n = w_bf16.shape[1]
    tm = min(512, t)
    return pl.pallas_call(
        functools.partial(_modmm_kernel, sigmoid_out=sigmoid_out),
        grid=(b, t // tm, n // tn),
        in_specs=[pl.BlockSpec((1, tm, d), lambda bi, i, j: (bi, i, 0)),
                  pl.BlockSpec((1, 1, d), lambda bi, i, j: (bi, 0, 0)),
                  pl.BlockSpec((1, 1, d), lambda bi, i, j: (bi, 0, 0)),
                  pl.BlockSpec((d, tn), lambda bi, i, j: (0, j))],
        out_specs=pl.BlockSpec((1, tm, tn), lambda bi, i, j: (bi, i, j)),
        out_shape=jax.ShapeDtypeStruct((b, t, n), out_dtype),
        scratch_shapes=[pltpu.VMEM((tm, d), BF16)],
        compiler_params=_cparams("arbitrary", "arbitrary", "arbitrary"),
        name=name,
    )(x, sc, sh, w_bf16)


def _sb_kernel(qi_ref, kb_ref, first_ref, last_ref, diag_ref, q_ref, k_ref, v_ref, o_ref, *scratch,
               tq, tk, past, nl):
    accs, cars, done_ref = scratch[:2 * nl], scratch[2 * nl:4 * nl], scratch[4 * nl]
    p = pl.program_id(2)
    qi = qi_ref[p]
    kb = kb_ref[p]

    @pl.when(first_ref[p] == 1)
    def _():
        for ref in scratch[:4 * nl]:
            ref[...] = jnp.zeros_like(ref)
        done_ref[0] = 0

    later = (lax.broadcasted_iota(I32, (tk, tk), 0) > lax.broadcasted_iota(I32, (tk, tk), 1)).astype(BF16)
    lane = lax.broadcasted_iota(I32, (tq, LANES), 1)

    def sweep(masked):
        if masked:
            pos = past + qi * tq + lax.broadcasted_iota(I32, (tq, tk), 0)
            kpos = kb * tk + lax.broadcasted_iota(I32, (tq, tk), 1)
            mask = kpos < pos
        for l in range(nl):
            cols = slice(l * LANES, (l + 1) * LANES)
            q = q_ref[0, :, cols] * (HEAD_DIM ** -0.5 * LOG2E)
            kbf = k_ref[0, :, cols].astype(BF16)
            vbf = v_ref[0, :, cols].astype(BF16)
            for hh in range(2):
                head = (lane < HEAD_DIM) if hh == 0 else (lane >= HEAD_DIM)
                z = _mm(jnp.where(head, q, 0.0).astype(BF16), kbf, NT)
                soft = jnp.log(1.0 + jnp.exp2(-jnp.abs(z))) * LOG2E
                log_beta = jnp.minimum(z, 0.0) - soft
                log_keep = log_beta - z
                if masked:
                    log_keep = jnp.where(mask, log_keep, 0.0)
                car = cars[2 * l + hh]
                after = _mm(log_keep.astype(BF16), later) + car[...]
                w = jnp.exp2(log_beta + after)
                if masked:
                    w = jnp.where(mask, w, 0.0)
                accs[2 * l + hh][...] += _mm(w.astype(BF16), vbf)
                car[...] += jnp.sum(log_keep, axis=1, keepdims=True)
        worst = cars[0][...]
        for car in cars[1:]:
            worst = jnp.maximum(worst, car[...])
        done_ref[0] = (jnp.max(worst) < F32_UNDERFLOW_LOG2).astype(I32)

    live = done_ref[0] == 0

    @pl.when(live & (diag_ref[p] == 1))
    def _():
        sweep(True)

    @pl.when(live & (diag_ref[p] == 0))
    def _():
        sweep(False)

    @pl.when(last_ref[p] == 1)
    def _():
        for l in range(nl):
            o_ref[0, :, l * LANES:(l + 1) * LANES] = jnp.where(lane < HEAD_DIM, accs[2 * l][...], accs[2 * l + 1][...])


def _sb_attention(q, k_all, v_all, past, tq, tk, nl):
    b, t, _ = q.shape
    nq = t // tq
    qi_l, kb_l, first_l, last_l, diag_l = [], [], [], [], []
    for qi in range(nq):
        top = (past + (qi + 1) * tq - 2) // tk
        for kb in range(top, -1, -1):
            qi_l.append(qi)
            kb_l.append(kb)
            first_l.append(int(kb == top))
            last_l.append(int(kb == 0))
            diag_l.append(int((kb + 1) * tk - 1 >= past + qi * tq))
    tabs = [jnp.asarray(np.asarray(a, np.int32)) for a in (qi_l, kb_l, first_l, last_l, diag_l)]
    ng = WIDTH // (nl * LANES)
    q_map = lambda bi, g, p, qi, kb, f, la, dg: (bi, qi[p], g)
    k_map = lambda bi, g, p, qi, kb, f, la, dg: (bi, kb[p], g)
    grid_spec = pltpu.PrefetchScalarGridSpec(
        num_scalar_prefetch=5,
        grid=(b, ng, len(qi_l)),
        in_specs=[pl.BlockSpec((1, tq, nl * LANES), q_map),
                  pl.BlockSpec((1, tk, nl * LANES), k_map),
                  pl.BlockSpec((1, tk, nl * LANES), k_map)],
        out_specs=pl.BlockSpec((1, tq, nl * LANES), q_map),
        scratch_shapes=[pltpu.VMEM((tq, LANES), F32)] * (2 * nl) + [pltpu.VMEM((tq, 1), F32)] * (2 * nl)
        + [pltpu.SMEM((1,), I32)],
    )
    return pl.pallas_call(
        functools.partial(_sb_kernel, tq=tq, tk=tk, past=past, nl=nl),
        grid_spec=grid_spec,
        out_shape=jax.ShapeDtypeStruct((b, t, WIDTH), F32),
        compiler_params=_cparams("arbitrary", "arbitrary", "arbitrary"),
        name="sb_attention",
    )(*tabs, q, k_all, v_all)


def _head_sum(x, same_head):
    return jnp.concatenate(
        [_dot_exact_rhs(x[:, c * LANES:(c + 1) * LANES], same_head) for c in range(WIDTH // LANES)], axis=1)


def _rw_prep_kernel(m_ref, sh0_ref, mu_ref, w0_ref, w2_ref, a0_ref, a2_ref, g2_ref, kk_ref, ka_ref, rk_ref,
                    wt_ref, ut_ref, rt_ref, mrb_ref, yv_ref, bt_ref, kt_ref, v_ref, bv_ref, g_ref, pend_ref, prev_ref,
                    *, tr):
    @pl.when(pl.program_id(1) == 0)
    def _():
        prev_ref[...] = sh0_ref[0]

    m = m_ref[0]
    row = lax.broadcasted_iota(I32, m.shape, 0)
    m_prev = jnp.where(row == 0, prev_ref[...], pltpu.roll(m, 1, 0))
    prev_ref[...] = m[tr - 1:tr, :]
    ms = m + (m_prev - m) * mu_ref[...]
    r = ms[:, 0:WIDTH]
    k = ms[:, WIDTH:2 * WIDTH]
    v = ms[:, 2 * WIDTH:3 * WIDTH]
    lora_wa = ms[:, 3 * WIDTH:3 * WIDTH + LANES]
    lora_g = ms[:, 3 * WIDTH + LANES:RW_PAD]

    w_log = -_softplus(-(w0_ref[...] + _dot3(jnp.tanh(lora_wa), w2_ref[...]))) - 0.5
    log_decay = -jnp.exp(w_log)
    a = _sigmoid(a0_ref[...] + _dot3(lora_wa, a2_ref[...]))
    g = _dot3(_sigmoid(lora_g), g2_ref[...])

    same_head = (lax.broadcasted_iota(I32, (LANES, LANES), 0) // HEAD_DIM ==
                 lax.broadcasted_iota(I32, (LANES, LANES), 1) // HEAD_DIM).astype(BF16)
    kk = k * kk_ref[...]
    kk = kk / jnp.maximum(jnp.sqrt(_head_sum(kk * kk, same_head)), 1e-12)
    kh = k * (1.0 + (a - 1.0) * ka_ref[...])
    bonus = _head_sum(r * kh * rk_ref[...], same_head)

    ti = lax.broadcasted_iota(I32, (tr, tr), 0)
    si = lax.broadcasted_iota(I32, (tr, tr), 1)
    tri = (si <= ti).astype(BF16)
    parts = _split3(log_decay)
    cl = _mm(tri, parts[0]) + (_mm(tri, parts[1]) + _mm(tri, parts[2]))
    ec = jnp.exp(cl)
    inv = jnp.exp(-cl)
    rt = r * ec
    kt = kh * inv
    at = -kk * jnp.exp(cl - log_decay)
    bt = kk * a * inv
    rt_ref[0] = rt.astype(BF16)
    kt_ref[0] = kt.astype(BF16)
    bt_ref[0] = bt.astype(BF16)
    v_ref[0] = v.astype(BF16)
    bv_ref[0] = bonus * v
    g_ref[0] = g
    pend_ref[0, 0] = ec[tr - 1:tr, :]

    lane = lax.broadcasted_iota(I32, (tr, LANES), 1)
    first = lane < HEAD_DIM
    t_row = lax.broadcasted_iota(I32, (tr, LANES), 0)
    s_col = lane % HEAD_DIM
    strict = s_col < t_row
    incl = s_col <= t_row
    eye2 = (s_col == t_row).astype(F32)

    def stack2(x):
        return jnp.concatenate([jnp.where(first, x, 0.0), jnp.where(first, 0.0, x)], axis=0)

    pairs = range(WIDTH // LANES)
    col = lambda hp: slice(hp * LANES, (hp + 1) * LANES)
    a_t = [at[:, col(hp)] for hp in pairs]
    v_s = [stack2(v[:, col(hp)]) for hp in pairs]
    l_ab, l_ak = [], []
    for hp in pairs:
        b_s, k_s, r_t = stack2(bt[:, col(hp)]), stack2(kt[:, col(hp)]), rt[:, col(hp)]
        l_ab.append(jnp.where(strict, _dot1(a_t[hp], b_s, NT), 0.0))
        l_ak.append(jnp.where(strict, _dot1(a_t[hp], k_s, NT), 0.0))
        mrb_ref[0, :, col(hp)] = jnp.where(incl, _dot1(r_t, b_s, NT), 0.0).astype(BF16)
        yv_ref[0, :, col(hp)] = _dot1(jnp.where(incl, _dot1(r_t, k_s, NT), 0.0), v_s[hp])
    tinv = [eye2 + l for l in l_ab]
    pw = l_ab
    for _ in range(5):
        pw = [_dot1(x, stack2(x)) for x in pw]
        tinv = [t_ + _dot1(t_, stack2(x)) for t_, x in zip(tinv, pw)]
    g_v = [_dot1(l_ak[hp], v_s[hp]) for hp in pairs]
    for hp in pairs:
        wt_ref[0, :, col(hp)] = _dot1(tinv[hp], stack2(a_t[hp])).astype(BF16)
        ut_ref[0, :, col(hp)] = _dot1(tinv[hp], stack2(g_v[hp]))


def _rw_prep(m, shift0, rwp):
    b, t, _ = m.shape
    tr = CHUNK
    row_spec = pl.BlockSpec((1, tr, WIDTH), lambda bi, i: (bi, i, 0))
    vec = lambda n: pl.BlockSpec((1, n), lambda bi, i: (0, 0))
    mat = lambda r: pl.BlockSpec((r, WIDTH), lambda bi, i: (0, 0))
    out = jax.ShapeDtypeStruct((b, t, WIDTH), F32)
    out_mxu = jax.ShapeDtypeStruct((b, t, WIDTH), BF16)
    return pl.pallas_call(
        functools.partial(_rw_prep_kernel, tr=tr),
        grid=(b, t // tr),
        in_specs=[pl.BlockSpec((1, tr, RW_PAD), lambda bi, i: (bi, i, 0)),
                  pl.BlockSpec((1, 1, RW_PAD), lambda bi, i: (bi, 0, 0)),
                  vec(RW_PAD), vec(WIDTH), mat(LANES), vec(WIDTH), mat(LANES), mat(RW_PAD - 3 * WIDTH - LANES),
                  vec(WIDTH), vec(WIDTH), vec(WIDTH)],
        out_specs=[row_spec] * 10 + [pl.BlockSpec((1, 1, 1, WIDTH), lambda bi, i: (bi, i, 0, 0))],
        out_shape=[out_mxu, out, out_mxu, out_mxu, out, out_mxu, out_mxu, out_mxu, out, out,
                   jax.ShapeDtypeStruct((b, t // tr, 1, WIDTH), F32)],
        scratch_shapes=[pltpu.VMEM((1, RW_PAD), F32)],
        compiler_params=_cparams("arbitrary", "arbitrary"),
        name="rwkv_prep",
    )(m, shift0, rwp["mu"], rwp["w0"], rwp["w2"], rwp["a0"], rwp["a2"], rwp["g2"], rwp["k_k"], rwp["k_a"],
      rwp["r_k"])


def _rw_scan_kernel(wt_ref, ut_ref, rt_ref, mrb_ref, yv_ref, bt_ref, kt_ref, v_ref, bv_ref, g_ref, pend_ref,
                    s0_ref, lnw_ref, lnb_ref, o_ref, sout_ref, *s_refs, n_chunks, npair):
    ci = pl.program_id(2)

    @pl.when(ci == 0)
    def _():
        for pi in range(npair):
            s_refs[pi][...] = s0_ref[0, pi]

    c = CHUNK
    first = lax.broadcasted_iota(I32, (c, LANES), 1) < HEAD_DIM
    vi = lax.broadcasted_iota(I32, (LANES, LANES), 0)
    kj = lax.broadcasted_iota(I32, (LANES, LANES), 1)
    same_head_b = (vi // HEAD_DIM) == (kj // HEAD_DIM)
    same_head = same_head_b.astype(BF16)

    def stack2(x):
        return jnp.concatenate([jnp.where(first, x, 0.0), jnp.where(first, 0.0, x)], axis=0)

    def chunk(j, carry):
        rows = pl.ds(pl.multiple_of(j * c, c), c)
        p_all = pend_ref[0, j]
        pairs = range(npair)
        col = lambda pi: slice(pi * LANES, (pi + 1) * LANES)
        s = [s_refs[pi][...] for pi in pairs]
        u = [_dot1(wt_ref[0, rows, col(pi)], s[pi], NT) + ut_ref[0, rows, col(pi)] for pi in pairs]
        for pi in pairs:
            ds = _dot1(jnp.concatenate([u[pi].astype(BF16), v_ref[0, rows, col(pi)]], axis=0),
                       jnp.concatenate([bt_ref[0, rows, col(pi)], kt_ref[0, rows, col(pi)]], axis=0), TN)
            s_refs[pi][...] = (s[pi] + jnp.where(same_head_b, ds, 0.0)) * p_all[:, col(pi)]
        y = [_dot1(rt_ref[0, rows, col(pi)], s[pi], NT) + (_dot1(mrb_ref[0, rows, col(pi)], stack2(u[pi]))
                                                           + yv_ref[0, rows, col(pi)]) for pi in pairs]
        for pi in pairs:
            mu = _dot_exact_rhs(y[pi], same_head) * (1.0 / HEAD_DIM)
            dlt = y[pi] - mu
            var = _dot_exact_rhs(dlt * dlt, same_head) * (1.0 / HEAD_DIM)
            yn = dlt * lax.rsqrt(var + GN_EPS) * lnw_ref[:, col(pi)] + lnb_ref[:, col(pi)]
            o_ref[0, rows, col(pi)] = (yn + bv_ref[0, rows, col(pi)]) * g_ref[0, rows, col(pi)]
        return carry

    lax.fori_loop(0, n_chunks, chunk, 0)

    @pl.when(ci == pl.num_programs(2) - 1)
    def _():
        for pi in range(npair):
            sout_ref[0, pi] = s_refs[pi][...]


def _rw_scan(prep, s0_bd, lnw, lnb, npair=4):
    b, t, _ = prep[0].shape
    tc = min(512, t)
    n_chunks = tc // CHUNK
    ng = WIDTH // (npair * LANES)
    row_spec = pl.BlockSpec((1, tc, npair * LANES), lambda bi, h, i: (bi, i, h))
    pend_spec = pl.BlockSpec((1, n_chunks, 1, npair * LANES), lambda bi, h, i: (bi, i, 0, h))
    st_spec = pl.BlockSpec((1, npair, LANES, LANES), lambda bi, h, i: (bi, h, 0, 0))
    vec_spec = pl.BlockSpec((1, npair * LANES), lambda bi, h, i: (0, h))
    return pl.pallas_call(
        functools.partial(_rw_scan_kernel, n_chunks=n_chunks, npair=npair),
        grid=(b, ng, t // tc),
        in_specs=[row_spec] * 10 + [pend_spec, st_spec, vec_spec, vec_spec],
        out_specs=[row_spec, st_spec],
        out_shape=[jax.ShapeDtypeStruct((b, t, WIDTH), F32),
                   jax.ShapeDtypeStruct((b, WIDTH // LANES, LANES, LANES), F32)],
        scratch_shapes=[pltpu.VMEM((LANES, LANES), F32)] * npair,
        compiler_params=_cparams("arbitrary", "arbitrary", "arbitrary"),
        name="rwkv_scan",
    )(*prep, s0_bd, lnw, lnb)


def _merge_kernel(oa_ref, ob_ref, sg_ref, x_ref, g1_ref, sc2_ref, sh2_ref, pa_ref, pb_ref, wo_ref, lng_ref, lnb_ref,
                  wr_ref, br_ref, h_ref, u2_ref, route_ref):
    sg = sg_ref[0]
    ya = _mm(oa_ref[0].astype(BF16), pa_ref[...])
    yb = _mm(ob_ref[0].astype(BF16), pb_ref[...])
    merged = sg[:, :D_MODEL].astype(F32) * ya + sg[:, D_MODEL:].astype(F32) * yb
    mix = _mm(merged.astype(BF16), wo_ref[...])
    hin = ALPHA * x_ref[0] + g1_ref[0] * mix
    mu = jnp.mean(hin, axis=-1, keepdims=True)
    dlt = hin - mu
    var = jnp.mean(dlt * dlt, axis=-1, keepdims=True)
    h = dlt * lax.rsqrt(var + LN_EPS) * lng_ref[...] + lnb_ref[...]
    h_ref[0] = h
    u2 = h * (1.0 + sc2_ref[0]) + sh2_ref[0]
    u2_ref[0] = u2

    logits = _dot3(u2, wr_ref[...]) + br_ref[...]
    lane_i = lax.broadcasted_iota(I32, logits.shape, 1)
    lane = lane_i.astype(F32)
    neg = -jnp.inf
    first_at = lambda vals, top: jnp.min(jnp.where(vals == top, lane, float(LANES)), axis=1, keepdims=True)
    lg = jnp.where(lane_i < N_GROUPS, logits, neg)
    mg = jnp.max(lg, axis=1, keepdims=True)
    g_idx = first_at(lg, mg)
    p_g = 1.0 / jnp.sum(jnp.exp(lg - mg), axis=1, keepdims=True)
    lo_lane = N_GROUPS + PER_GROUP * g_idx
    le = jnp.where((lane >= lo_lane) & (lane < lo_lane + PER_GROUP), logits, neg)
    m1 = jnp.max(le, axis=1, keepdims=True)
    i1 = first_at(le, m1)
    le2 = jnp.where(lane == i1, neg, le)
    m2 = jnp.max(le2, axis=1, keepdims=True)
    i2 = first_at(le2, m2)
    ratio = jnp.exp(m2 - m1)
    w1 = p_g / (1.0 + ratio)
    w2 = p_g * ratio / (1.0 + ratio)
    route = jnp.where(lane_i == 0, i1 - N_GROUPS,
                      jnp.where(lane_i == 1, i2 - N_GROUPS,
                                jnp.where(lane_i == 2, w1, jnp.where(lane_i == 3, w2, 0.0))))
    route_ref[0] = route


def _merge(o_a, o_b, sg, x, g1, sc2, sh2, wts):
    b, t, d = x.shape
    tm = min(256, t)
    const = lambda shape: pl.BlockSpec(shape, lambda bi, i: (0,) * len(shape), pipeline_mode=pl.Buffered(1))
    seq = pl.BlockSpec((1, 1, d), lambda bi, i: (bi, 0, 0))
    row = lambda n: pl.BlockSpec((1, tm, n), lambda bi, i: (bi, i, 0))
    return pl.pallas_call(
        _merge_kernel,
        grid=(b, t // tm),
        in_specs=[row(WIDTH), row(WIDTH), row(2 * d), row(d), seq, seq, seq,
                  const((WIDTH, d)), const((WIDTH, d)), const((d, d)), const((1, d)), const((1, d)),
                  const((d, LANES)), const((1, LANES))],
        out_specs=[row(d), row(d), row(LANES)],
        out_shape=[jax.ShapeDtypeStruct((b, t, d), F32), jax.ShapeDtypeStruct((b, t, d), F32),
                   jax.ShapeDtypeStruct((b, t, LANES), F32)],
        compiler_params=_cparams("arbitrary", "arbitrary"),
        name="merge_route",
    )(o_a, o_b, sg, x, g1, sc2, sh2, wts["p_a"], wts["p_b"], wts["w_out"], wts["ln1_g"], wts["ln1_b"],
      wts["w_route"], wts["b_route"])


def _moe_kernel(be_ref, nused_ref, tok_ref, x_hbm, wg_ref, wu_ref, wd_ref, y_ref, xbuf, sem, *, tb):
    i = pl.program_id(0)
    n_used = nused_ref[0]

    def start_gather(block, slot):
        def start(r, carry):
            pltpu.make_async_copy(x_hbm.at[pl.ds(tok_ref[block * tb + r], 1)], xbuf.at[slot, pl.ds(r, 1)],
                                  sem.at[slot]).start()
            return carry

        lax.fori_loop(0, tb, start, 0, unroll=8)

    @pl.when((i == 0) & (n_used > 0))
    def _():
        start_gather(0, 0)

    @pl.when(i + 1 < n_used)
    def _():
        start_gather(i + 1, (i + 1) % 2)

    @pl.when(i < n_used)
    def _():
        slot = i % 2
        pltpu.make_async_copy(x_hbm.at[pl.ds(0, tb)], xbuf.at[slot], sem.at[slot]).wait()
        x = xbuf[slot].astype(BF16)
        hg = _mm(x, wg_ref[0])
        hu = _mm(x, wu_ref[0])
        hid = (hg * _sigmoid(hg) * hu).astype(BF16)
        y_ref[...] = _mm(hid, wd_ref[0])

    @pl.when(i >= n_used)
    def _():
        y_ref[...] = jnp.zeros_like(y_ref)


def _moe(block_expert, n_used, slot_token, x_all, wg, wu, wd, tb):
    n_blocks = block_expert.shape[0]
    d = x_all.shape[1]
    grid_spec = pltpu.PrefetchScalarGridSpec(
        num_scalar_prefetch=3,
        grid=(n_blocks,),
        in_specs=[pl.BlockSpec(memory_space=pl.ANY),
                  pl.BlockSpec((1, d, D_EXPERT), lambda i, be, nu, tok: (be[i], 0, 0), pipeline_mode=pl.Buffered(1)),
                  pl.BlockSpec((1, d, D_EXPERT), lambda i, be, nu, tok: (be[i], 0, 0), pipeline_mode=pl.Buffered(1)),
                  pl.BlockSpec((1, D_EXPERT, d), lambda i, be, nu, tok: (be[i], 0, 0), pipeline_mode=pl.Buffered(1))],
        out_specs=pl.BlockSpec((tb, d), lambda i, be, nu, tok: (i, 0)),
        scratch_shapes=[pltpu.VMEM((2, tb, d), F32), pltpu.SemaphoreType.DMA((2,))],
    )
    return pl.pallas_call(
        functools.partial(_moe_kernel, tb=tb),
        grid_spec=grid_spec,
        out_shape=jax.ShapeDtypeStruct((n_blocks * tb, d), F32),
        compiler_params=_cparams("arbitrary"),
        name="moe_experts",
    )(block_expert, n_used, slot_token, x_all, wg, wu, wd)


def _final_kernel(dest_ref, y_hbm, h_ref, rw_ref, g2_ref, lng_ref, lnb_ref, o_ref, ybuf, sem, *, tm, per_seq):
    step = pl.program_id(0) * per_seq + pl.program_id(1)
    n_steps = pl.num_programs(0) * per_seq

    def start_gather(blk, slot):
        def start(r, carry):
            for k in range(2):
                pltpu.make_async_copy(y_hbm.at[pl.ds(dest_ref[2 * (blk * tm + r) + k], 1)],
                                      ybuf.at[slot, k, pl.ds(r, 1)], sem.at[slot]).start()
            return carry

        lax.fori_loop(0, tm, start, 0, unroll=4)

    @pl.when(step == 0)
    def _():
        start_gather(0, 0)

    @pl.when(step + 1 < n_steps)
    def _():
        start_gather(step + 1, (step + 1) % 2)

    slot = step % 2
    for k in range(2):
        pltpu.make_async_copy(y_hbm.at[pl.ds(0, tm)], ybuf.at[slot, k], sem.at[slot]).wait()
    rw = rw_ref[0]
    moe = rw[:, 2:3] * ybuf[slot, 0] + rw[:, 3:4] * ybuf[slot, 1]
    xin = ALPHA * h_ref[0] + g2_ref[0] * moe
    mu = jnp.mean(xin, axis=-1, keepdims=True)
    dlt = xin - mu
    var = jnp.mean(dlt * dlt, axis=-1, keepdims=True)
    o_ref[0] = dlt * lax.rsqrt(var + LN_EPS) * lng_ref[...] + lnb_ref[...]


def _final(dest, y_sorted, h, route, g2, ln2_g, ln2_b):
    b, t, d = h.shape
    tm = min(256, t)
    per_seq = t // tm
    grid_spec = pltpu.PrefetchScalarGridSpec(
        num_scalar_prefetch=1,
        grid=(b, per_seq),
        in_specs=[pl.BlockSpec(memory_space=pl.ANY),
                  pl.BlockSpec((1, tm, d), lambda bi, i, de: (bi, i, 0)),
                  pl.BlockSpec((1, tm, LANES), lambda bi, i, de: (bi, i, 0)),
                  pl.BlockSpec((1, 1, d), lambda bi, i, de: (bi, 0, 0)),
                  pl.BlockSpec((1, d), lambda bi, i, de: (0, 0)),
                  pl.BlockSpec((1, d), lambda bi, i, de: (0, 0))],
        out_specs=pl.BlockSpec((1, tm, d), lambda bi, i, de: (bi, i, 0)),
        scratch_shapes=[pltpu.VMEM((2, 2, tm, d), F32), pltpu.SemaphoreType.DMA((2,))],
    )
    return pl.pallas_call(
        functools.partial(_final_kernel, tm=tm, per_seq=per_seq),
        grid_spec=grid_spec,
        out_shape=jax.ShapeDtypeStruct((b, t, d), F32),
        compiler_params=_cparams("arbitrary", "arbitrary"),
        name="combine_ln2",
    )(dest, y_sorted, h, route, g2, ln2_g, ln2_b)


def _pad_rows(a, rows):
    return jnp.pad(a, ((0, rows - a.shape[0]), (0, 0)))


def _mixers(x, ada, past_k, past_v, s0, shift0, wts, sb_tiles):
    b, t, d = x.shape
    sh1, sc1, g1, sh2, sc2, g2 = [a[:, None, :] for a in jnp.split(ada, 6, axis=-1)]
    q = _modmm(x, sc1, sh1, wts["w_q"], 1024, name="proj_q")
    k = _modmm(x, sc1, sh1, wts["w_k"], 1024, name="proj_k")
    v = _modmm(x, sc1, sh1, wts["w_v"], 1024, name="proj_v")
    m = _modmm(x, sc1, sh1, wts["w_m"], RW_PAD // 3, name="proj_rwkv")
    sg = _modmm(x, sc1, sh1, wts["w_g"], 1024, out_dtype=BF16, sigmoid_out=True, name="proj_gates")

    tq, tk, nl = sb_tiles
    past = past_k.shape[1]
    s_pad = -(-(past + t) // tk) * tk
    tail = jnp.zeros((b, s_pad - past - t, WIDTH), F32)
    k_all = jnp.concatenate([past_k.reshape(b, past, WIDTH), k, tail], axis=1)
    v_all = jnp.concatenate([past_v.reshape(b, past, WIDTH), v, tail], axis=1)
    o_a = _sb_attention(q, k_all, v_all, past, tq, tk, nl)

    shift0_p = jnp.pad(shift0, ((0, 0), (0, 0), (0, RW_PAD - RW_COLS)))
    prep = _rw_prep(m, shift0_p, wts["rw"])
    nhp = WIDTH // LANES
    s0_pairs = s0.reshape(b, nhp, 2, HEAD_DIM, HEAD_DIM)
    s0_bd = jnp.zeros((b, nhp, LANES, LANES), F32)
    s0_bd = s0_bd.at[:, :, :HEAD_DIM, :HEAD_DIM].set(s0_pairs[:, :, 0]).at[:, :, HEAD_DIM:, HEAD_DIM:].set(s0_pairs[:, :, 1])
    o_b, s_bd = _rw_scan(prep, s0_bd, wts["rw"]["lnx_w"], wts["rw"]["lnx_b"])
    s_new = jnp.stack([s_bd[:, :, :HEAD_DIM, :HEAD_DIM], s_bd[:, :, HEAD_DIM:, HEAD_DIM:]], axis=2)
    s_new = s_new.reshape(b, 2 * nhp, HEAD_DIM, HEAD_DIM)
    shift_new = m[:, t - 1:t, :RW_COLS]

    h, u2, route = _merge(o_a, o_b, sg, x, g1, sc2, sh2, wts)
    return h, u2, route, g2, k, v, s_new, shift_new


def kernel(x_prompt, x_sample, cache_sb_k, cache_sb_v, state_rwkv, state_rwkv_shift, c_prompt, c_sample, w_ada, b_ada, w_in, tokshift_mu, rw_w0, rw_w2, rw_a0, rw_a2, rw_g2, rw_k_k, rw_k_a, rw_r_k, rw_lnx_w, rw_lnx_b, w_branch_a, w_branch_b, w_out, ln1_g, ln1_b, w_router_group, b_router_group, w_router_expert, b_router_expert, w_exp_gate, w_exp_up, w_exp_down, ln2_g, ln2_b):
    depth = w_ada.shape[0]
    assert depth == 1
    bp, tp, d = x_prompt.shape
    bs, ts, _ = x_sample.shape
    l = 0

    w_in_l = w_in[l]
    w_rw = w_in_l[:, 3 * WIDTH:3 * WIDTH + RW_COLS]
    row_vec = lambda a: a.reshape(1, -1)
    pad_cols = lambda a, n: jnp.pad(a, ((0, 0), (0, n - a.shape[1])))
    n_lg = RW_PAD - 3 * WIDTH - LANES
    wts = {
        "w_q": w_in_l[:, 0:WIDTH].astype(BF16),
        "w_k": w_in_l[:, WIDTH:2 * WIDTH].astype(BF16),
        "w_v": w_in_l[:, 2 * WIDTH:3 * WIDTH].astype(BF16),
        "w_m": pad_cols(w_rw, RW_PAD).astype(BF16),
        "w_g": w_in_l[:, 3 * WIDTH + RW_COLS:].astype(BF16),
        "p_a": w_branch_a[l].astype(BF16),
        "p_b": w_branch_b[l].astype(BF16),
        "w_out": w_out[l].astype(BF16),
        "ln1_g": row_vec(ln1_g[l]),
        "ln1_b": row_vec(ln1_b[l]),
        "w_route": pad_cols(jnp.concatenate([w_router_group[l], w_router_expert[l]], axis=1), LANES),
        "b_route": pad_cols(row_vec(jnp.concatenate([b_router_group[l], b_router_expert[l]])), LANES),
        "rw": {
            "mu": pad_cols(row_vec(tokshift_mu[l]), RW_PAD),
            "w0": row_vec(rw_w0[l]),
            "w2": _pad_rows(rw_w2[l], LANES),
            "a0": row_vec(rw_a0[l]),
            "a2": jnp.concatenate([jnp.zeros((LORA_W, WIDTH), F32), rw_a2[l]], axis=0),
            "g2": _pad_rows(rw_g2[l], n_lg),
            "k_k": row_vec(rw_k_k[l]),
            "k_a": row_vec(rw_k_a[l]),
            "r_k": row_vec(rw_r_k[l]),
            "lnx_w": row_vec(rw_lnx_w[l]),
            "lnx_b": row_vec(rw_lnx_b[l]),
        },
    }

    rows = -(-(bp + bs) // 8) * 8
    c_all = _pad_rows(jnp.concatenate([c_prompt, c_sample], axis=0), rows)
    ada = _ada(c_all, w_ada[l], b_ada[l])

    empty = jnp.zeros((bp, 0, WIDTH), F32)
    zero_state = jnp.zeros((bp, WIDTH // HEAD_DIM, HEAD_DIM, HEAD_DIM), F32)
    zero_shift = jnp.zeros((bp, 1, RW_COLS), F32)
    hp, u2p, route_p, g2p, kp, vp, sp_, shp = _mixers(
        x_prompt, ada[:bp], empty, empty, zero_state, zero_shift, wts, (min(256, tp), 256, 4))
    hs, u2s, route_s, g2s, ks, vs, ss_, shs = _mixers(
        x_sample, ada[bp:bp + bs], cache_sb_k[l], cache_sb_v[l], state_rwkv[l], state_rwkv_shift[l], wts,
        (min(256, ts), 256, 4))

    n_p, n_s = bp * tp, bs * ts
    n_tok = n_p + n_s
    tb = 512
    route_all = jnp.concatenate([route_p.reshape(n_p, LANES), route_s.reshape(n_s, LANES)], axis=0)
    flat_e = route_all[:, 0:2].astype(I32).reshape(n_tok * 2)
    onehot = (flat_e[:, None] == jnp.arange(N_EXPERTS, dtype=I32)[None, :]).astype(I32)
    csum = jnp.cumsum(onehot, axis=0)
    rank = jnp.take_along_axis(csum, flat_e[:, None], axis=1)[:, 0] - 1
    counts = csum[-1]
    padded = (counts + tb - 1) // tb * tb
    pend = jnp.cumsum(padded)
    pstart = pend - padded
    dest = (pstart[flat_e] + rank).astype(I32)
    n_blocks = (2 * n_tok + N_EXPERTS * (tb - 1)) // tb + 1
    slot_token = jnp.zeros((n_blocks * tb,), I32).at[dest].set(jnp.arange(2 * n_tok, dtype=I32) // 2)
    block_start = jnp.arange(n_blocks, dtype=I32) * tb
    block_expert = jnp.minimum(jnp.searchsorted(pend, block_start, side="right"), N_EXPERTS - 1).astype(I32)
    n_used = (pend[-1] // tb).astype(I32).reshape(1)

    u2_all = jnp.concatenate([u2p.reshape(n_p, d), u2s.reshape(n_s, d)], axis=0)
    y_sorted = _moe(block_expert, n_used, slot_token, u2_all, w_exp_gate[l].astype(BF16),
                    w_exp_up[l].astype(BF16), w_exp_down[l].astype(BF16), tb)

    ln2g, ln2b = row_vec(ln2_g[l]), row_vec(ln2_b[l])
    y_prompt = _final(dest[:2 * n_p], y_sorted, hp, route_p, g2p, ln2g, ln2b)
    y_sample = _final(dest[2 * n_p:], y_sorted, hs, route_s, g2s, ln2g, ln2b)

    heads = lambda a: a.reshape(1, a.shape[0], a.shape[1], WIDTH // HEAD_DIM, HEAD_DIM)
    return (y_prompt, y_sample, heads(kp), heads(vp), sp_[None], shp[None],
            heads(ks), heads(vs), ss_[None], shs[None])
```

```python
import functools

import jax
import jax.numpy as jnp
import numpy as np
from jax import lax
from jax.experimental import pallas as pl
from jax.experimental.pallas import tpu as pltpu

F32 = jnp.float32
BF16 = jnp.bfloat16
I32 = jnp.int32

D_MODEL = 2048
HEAD_DIM = 64
WIDTH = 1024
LORA_W = 64
LORA_A = 64
LORA_G = 160
RW_COLS = 3 * WIDTH + LORA_W + LORA_A + LORA_G
RW_PAD = 3456
N_GROUPS = 4
PER_GROUP = 8
N_EXPERTS = 32
D_EXPERT = 1024
LN_EPS = 1e-5
GN_EPS = 64e-5
ALPHA = 2.0 ** 0.25
LOG2E = 1.4426950408889634
F32_UNDERFLOW_LOG2 = -150.0
LANES = 128
CHUNK = 64
VMEM_LIMIT = 56 * 1024 * 1024

NN = (((1,), (0,)), ((), ()))
NT = (((1,), (1,)), ((), ()))
TN = (((0,), (0,)), ((), ()))


def _cparams(*sem):
    return pltpu.CompilerParams(dimension_semantics=sem, vmem_limit_bytes=VMEM_LIMIT)


def _mm(a, b, dims=NN):
    return lax.dot_general(a, b, dims, preferred_element_type=F32)


def _split(x):
    hi = x.astype(BF16)
    lo = (x - hi.astype(F32)).astype(BF16)
    return hi, lo


def _split3(x):
    hi = x.astype(BF16)
    r1 = x - hi.astype(F32)
    mid = r1.astype(BF16)
    lo = (r1 - mid.astype(F32)).astype(BF16)
    return hi, mid, lo


def _dot3(a, b, dims=NN):
    ah, al = _split(a)
    bh, bl = _split(b)
    return _mm(ah, bh, dims) + (_mm(al, bh, dims) + _mm(ah, bl, dims))


def _dot1(a, b, dims=NN):
    return _mm(a.astype(BF16), b.astype(BF16), dims)


def _dot_exact_rhs(a, b_bf16, terms=2):
    parts = _split(a) if terms == 2 else _split3(a)
    out = _mm(parts[0], b_bf16)
    for p in parts[1:]:
        out = out + _mm(p, b_bf16)
    return out


def _softplus(z):
    return jnp.maximum(z, 0.0) + jnp.log(1.0 + jnp.exp(-jnp.abs(z)))


def _sigmoid(z):
    return 1.0 / (1.0 + jnp.exp(-z))


def _ada_kernel(c_ref, w_ref, b_ref, o_ref):
    c = c_ref[...]
    o_ref[...] = _dot3(c * _sigmoid(c), w_ref[...]) + b_ref[...]


def _ada(c_all, w_ada, b_ada):
    rows = c_all.shape[0]
    n = w_ada.shape[1]
    tn = 1024
    return pl.pallas_call(
        _ada_kernel,
        grid=(n // tn,),
        in_specs=[pl.BlockSpec((rows, D_MODEL), lambda j: (0, 0)),
                  pl.BlockSpec((D_MODEL, tn), lambda j: (0, j)),
                  pl.BlockSpec((1, tn), lambda j: (0, j))],
        out_specs=pl.BlockSpec((rows, tn), lambda j: (0, j)),
        out_shape=jax.ShapeDtypeStruct((rows, n), F32),
        compiler_params=_cparams("arbitrary"),
        name="ada",
    )(c_all, w_ada, b_ada.reshape(1, n))


def _modmm_kernel(x_ref, sc_ref, sh_ref, w_ref, o_ref, u_ref, *, sigmoid_out):
    @pl.when(pl.program_id(2) == 0)
    def _():
        u_ref[...] = (x_ref[0] * (1.0 + sc_ref[0]) + sh_ref[0]).astype(BF16)

    y = _mm(u_ref[...], w_ref[...])
    if sigmoid_out:
        y = _sigmoid(y)
    o_ref[0] = y.astype(o_ref.dtype)


def _modmm(x, sc, sh, w_bf16, tn, out_dtype=F32, sigmoid_out=False, name="proj"):
    b, t, d = x.shape
    n = w_bf16.shape[1]
    tm = min(512, t)
    return pl.pallas_call(
        functools.partial(_modmm_kernel, sigmoid_out=sigmoid_out),
        grid=(b, t // tm, n // tn),
        in_specs=[pl.BlockSpec((1, tm, d), lambda bi, i, j: (bi, i, 0)),
                  pl.BlockSpec((1, 1, d), lambda bi, i, j: (bi, 0, 0)),
                  pl.BlockSpec((1, 1, d), lambda bi, i, j: (bi, 0, 0)),
                  pl.BlockSpec((d, tn), lambda bi, i, j: (0, j))],
        out_specs=pl.BlockSpec((1, tm, tn), lambda bi, i, j: (bi, i, j)),
        out_shape=jax.ShapeDtypeStruct((b, t, n), out_dtype),
        scratch_shapes=[pltpu.VMEM((tm, d), BF16)],
        compiler_params=_cparams("arbitrary", "arbitrary", "arbitrary"),
        name=name,
    )(x, sc, sh, w_bf16)


def _sb_kernel(q_ref, k_hbm, v_hbm, o_ref, kbuf, vbuf, sem, done_ref, *scratch, tq, tk, past, nl):
    accs, cars = scratch[:2 * nl], scratch[2 * nl:]
    bi, g, qi = pl.program_id(0), pl.program_id(1), pl.program_id(2)
    width = nl * LANES
    top = (past + (qi + 1) * tq - 2) // tk

    def kv_copies(kb, slot):
        src = (bi, pl.ds(kb * tk, tk), pl.ds(g * width, width))
        return (pltpu.make_async_copy(k_hbm.at[src], kbuf.at[slot], sem.at[0, slot]),
                pltpu.make_async_copy(v_hbm.at[src], vbuf.at[slot], sem.at[1, slot]))

    for cp in kv_copies(top, 0):
        cp.start()
    for ref in scratch:
        ref[...] = jnp.zeros_like(ref)
    done_ref[0] = 0

    later = (lax.broadcasted_iota(I32, (tk, tk), 0) > lax.broadcasted_iota(I32, (tk, tk), 1)).astype(BF16)
    lane = lax.broadcasted_iota(I32, (tq, LANES), 1)

    def sweep(kb, slot, masked):
        if masked:
            pos = past + qi * tq + lax.broadcasted_iota(I32, (tq, tk), 0)
            kpos = kb * tk + lax.broadcasted_iota(I32, (tq, tk), 1)
            mask = kpos < pos
        for l in range(nl):
            cols = slice(l * LANES, (l + 1) * LANES)
            q = q_ref[0, :, cols] * (HEAD_DIM ** -0.5 * LOG2E)
            kbf = kbuf[slot, :, cols].astype(BF16)
            vbf = vbuf[slot, :, cols].astype(BF16)
            for hh in range(2):
                head = (lane < HEAD_DIM) if hh == 0 else (lane >= HEAD_DIM)
                z = _mm(jnp.where(head, q, 0.0).astype(BF16), kbf, NT)
                soft = jnp.log(1.0 + jnp.exp2(-jnp.abs(z))) * LOG2E
                log_beta = jnp.minimum(z, 0.0) - soft
                log_keep = log_beta - z
                if masked:
                    log_keep = jnp.where(mask, log_keep, 0.0)
                car = cars[2 * l + hh]
                after = _mm(log_keep.astype(BF16), later) + car[...]
                w = jnp.exp2(log_beta + after)
                if masked:
                    w = jnp.where(mask, w, 0.0)
                accs[2 * l + hh][...] += _mm(w.astype(BF16), vbf)
                car[...] += jnp.sum(log_keep, axis=1, keepdims=True)
        worst = cars[0][...]
        for car in cars[1:]:
            worst = jnp.maximum(worst, car[...])
        done_ref[0] = (jnp.max(worst) < F32_UNDERFLOW_LOG2).astype(I32)

    def body(kb):
        slot = (top - kb) % 2
        for cp in kv_copies(kb, slot):
            cp.wait()

        @pl.when(kb > 0)
        def _():
            for cp in kv_copies(kb - 1, 1 - slot):
                cp.start()

        diag = (kb + 1) * tk - 1 >= past + qi * tq

        @pl.when(diag)
        def _():
            sweep(kb, slot, True)

        @pl.when(jnp.logical_not(diag))
        def _():
            sweep(kb, slot, False)

        return kb - 1

    kb_end = lax.while_loop(lambda kb: (kb >= 0) & (done_ref[0] == 0), body, top)

    @pl.when(kb_end >= 0)
    def _():
        for cp in kv_copies(kb_end, (top - kb_end) % 2):
            cp.wait()

    for l in range(nl):
        o_ref[0, :, l * LANES:(l + 1) * LANES] = jnp.where(lane < HEAD_DIM, accs[2 * l][...], accs[2 * l + 1][...])


def _sb_attention(q, k_all, v_all, past, tq, tk, nl):
    b, t, _ = q.shape
    width = nl * LANES
    q_spec = pl.BlockSpec((1, tq, width), lambda bi, g, qi: (bi, qi, g))
    return pl.pallas_call(
        functools.partial(_sb_kernel, tq=tq, tk=tk, past=past, nl=nl),
        grid=(b, WIDTH // width, t // tq),
        in_specs=[q_spec, pl.BlockSpec(memory_space=pl.ANY), pl.BlockSpec(memory_space=pl.ANY)],
        out_specs=q_spec,
        out_shape=jax.ShapeDtypeStruct((b, t, WIDTH), F32),
        scratch_shapes=[pltpu.VMEM((2, tk, width), F32), pltpu.VMEM((2, tk, width), F32),
                        pltpu.SemaphoreType.DMA((2, 2)), pltpu.SMEM((1,), I32)]
        + [pltpu.VMEM((tq, LANES), F32)] * (2 * nl) + [pltpu.VMEM((tq, 1), F32)] * (2 * nl),
        compiler_params=_cparams("arbitrary", "arbitrary", "arbitrary"),
        name="sb_attention",
    )(q, k_all, v_all)


def _head_sum(x, same_head):
    return jnp.concatenate(
        [_dot_exact_rhs(x[:, c * LANES:(c + 1) * LANES], same_head) for c in range(WIDTH // LANES)], axis=1)


def _rw_prep_kernel(m_ref, sh0_ref, mu_ref, w0_ref, w2_ref, a0_ref, a2_ref, g2_ref, kk_ref, ka_ref, rk_ref,
                    wt_ref, ut_ref, rt_ref, mrb_ref, yv_ref, bt_ref, kt_ref, v_ref, bv_ref, g_ref, pend_ref, prev_ref,
                    *, tr):
    @pl.when(pl.program_id(1) == 0)
    def _():
        prev_ref[...] = sh0_ref[0]

    m = m_ref[0]
    row = lax.broadcasted_iota(I32, m.shape, 0)
    m_prev = jnp.where(row == 0, prev_ref[...], pltpu.roll(m, 1, 0))
    prev_ref[...] = m[tr - 1:tr, :]
    ms = m + (m_prev - m) * mu_ref[...]
    r = ms[:, 0:WIDTH]
    k = ms[:, WIDTH:2 * WIDTH]
    v = ms[:, 2 * WIDTH:3 * WIDTH]
    lora_wa = ms[:, 3 * WIDTH:3 * WIDTH + LANES]
    lora_g = ms[:, 3 * WIDTH + LANES:RW_PAD]

    w_log = -_softplus(-(w0_ref[...] + _dot3(jnp.tanh(lora_wa), w2_ref[...]))) - 0.5
    log_decay = -jnp.exp(w_log)
    a = _sigmoid(a0_ref[...] + _dot3(lora_wa, a2_ref[...]))
    g = _dot3(_sigmoid(lora_g), g2_ref[...])

    same_head = (lax.broadcasted_iota(I32, (LANES, LANES), 0) // HEAD_DIM ==
                 lax.broadcasted_iota(I32, (LANES, LANES), 1) // HEAD_DIM).astype(BF16)
    kk = k * kk_ref[...]
    kk = kk / jnp.maximum(jnp.sqrt(_head_sum(kk * kk, same_head)), 1e-12)
    kh = k * (1.0 + (a - 1.0) * ka_ref[...])
    bonus = _head_sum(r * kh * rk_ref[...], same_head)

    ti = lax.broadcasted_iota(I32, (tr, tr), 0)
    si = lax.broadcasted_iota(I32, (tr, tr), 1)
    tri = (si <= ti).astype(BF16)
    parts = _split3(log_decay)
    cl = _mm(tri, parts[0]) + (_mm(tri, parts[1]) + _mm(tri, parts[2]))
    ec = jnp.exp(cl)
    inv = jnp.exp(-cl)
    rt = r * ec
    kt = kh * inv
    at = -kk * jnp.exp(cl - log_decay)
    bt = kk * a * inv
    rt_ref[0] = rt.astype(BF16)
    kt_ref[0] = kt.astype(BF16)
    bt_ref[0] = bt.astype(BF16)
    v_ref[0] = v.astype(BF16)
    bv_ref[0] = bonus * v
    g_ref[0] = g
    pend_ref[0, 0] = ec[tr - 1:tr, :]

    lane = lax.broadcasted_iota(I32, (tr, LANES), 1)
    first = lane < HEAD_DIM
    t_row = lax.broadcasted_iota(I32, (tr, LANES), 0)
    s_col = lane % HEAD_DIM
    strict = s_col < t_row
    incl = s_col <= t_row
    eye2 = (s_col == t_row).astype(F32)

    def stack2(x):
        return jnp.concatenate([jnp.where(first, x, 0.0), jnp.where(first, 0.0, x)], axis=0)

    pairs = range(WIDTH // LANES)
    col = lambda hp: slice(hp * LANES, (hp + 1) * LANES)
    a_t = [at[:, col(hp)] for hp in pairs]
    v_s = [stack2(v[:, col(hp)]) for hp in pairs]
    l_ab, l_ak = [], []
    for hp in pairs:
        b_s, k_s, r_t = stack2(bt[:, col(hp)]), stack2(kt[:, col(hp)]), rt[:, col(hp)]
        l_ab.append(jnp.where(strict, _dot1(a_t[hp], b_s, NT), 0.0))
        l_ak.append(jnp.where(strict, _dot1(a_t[hp], k_s, NT), 0.0))
        mrb_ref[0, :, col(hp)] = jnp.where(incl, _dot1(r_t, b_s, NT), 0.0).astype(BF16)
        yv_ref[0, :, col(hp)] = _dot1(jnp.where(incl, _dot1(r_t, k_s, NT), 0.0), v_s[hp])
    tinv = [eye2 + l for l in l_ab]
    pw = l_ab
    for _ in range(5):
        pw = [_dot1(x, stack2(x)) for x in pw]
        tinv = [t_ + _dot1(t_, stack2(x)) for t_, x in zip(tinv, pw)]
    g_v = [_dot1(l_ak[hp], v_s[hp]) for hp in pairs]
    for hp in pairs:
        wt_ref[0, :, col(hp)] = _dot1(tinv[hp], stack2(a_t[hp])).astype(BF16)
        ut_ref[0, :, col(hp)] = _dot1(tinv[hp], stack2(g_v[hp]))


def _rw_prep(m, shift0, rwp):
    b, t, _ = m.shape
    tr = CHUNK
    row_spec = pl.BlockSpec((1, tr, WIDTH), lambda bi, i: (bi, i, 0))
    vec = lambda n: pl.BlockSpec((1, n), lambda bi, i: (0, 0))
    mat = lambda r: pl.BlockSpec((r, WIDTH), lambda bi, i: (0, 0))
    out = jax.ShapeDtypeStruct((b, t, WIDTH), F32)
    out_mxu = jax.ShapeDtypeStruct((b, t, WIDTH), BF16)
    return pl.pallas_call(
        functools.partial(_rw_prep_kernel, tr=tr),
        grid=(b, t // tr),
        in_specs=[pl.BlockSpec((1, tr, RW_PAD), lambda bi, i: (bi, i, 0)),
                  pl.BlockSpec((1, 1, RW_PAD), lambda bi, i: (bi, 0, 0)),
                  vec(RW_PAD), vec(WIDTH), mat(LANES), vec(WIDTH), mat(LANES), mat(RW_PAD - 3 * WIDTH - LANES),
                  vec(WIDTH), vec(WIDTH), vec(WIDTH)],
        out_specs=[row_spec] * 10 + [pl.BlockSpec((1, 1, 1, WIDTH), lambda bi, i: (bi, i, 0, 0))],
        out_shape=[out_mxu, out, out_mxu, out_mxu, out, out_mxu, out_mxu, out_mxu, out, out,
                   jax.ShapeDtypeStruct((b, t // tr, 1, WIDTH), F32)],
        scratch_shapes=[pltpu.VMEM((1, RW_PAD), F32)],
        compiler_params=_cparams("arbitrary", "arbitrary"),
        name="rwkv_prep",
    )(m, shift0, rwp["mu"], rwp["w0"], rwp["w2"], rwp["a0"], rwp["a2"], rwp["g2"], rwp["k_k"], rwp["k_a"],
      rwp["r_k"])


def _rw_scan_kernel(wt_ref, ut_ref, rt_ref, mrb_ref, yv_ref, bt_ref, kt_ref, v_ref, bv_ref, g_ref, pend_ref,
                    s0_ref, lnw_ref, lnb_ref, o_ref, sout_ref, *s_refs, n_chunks, npair):
    ci = pl.program_id(2)

    @pl.when(ci == 0)
    def _():
        for pi in range(npair):
            s_refs[pi][...] = s0_ref[0, pi]

    c = CHUNK
    first = lax.broadcasted_iota(I32, (c, LANES), 1) < HEAD_DIM
    vi = lax.broadcasted_iota(I32, (LANES, LANES), 0)
    kj = lax.broadcasted_iota(I32, (LANES, LANES), 1)
    same_head_b = (vi // HEAD_DIM) == (kj // HEAD_DIM)
    same_head = same_head_b.astype(BF16)

    def stack2(x):
        return jnp.concatenate([jnp.where(first, x, 0.0), jnp.where(first, 0.0, x)], axis=0)

    def chunk(j, carry):
        rows = pl.ds(pl.multiple_of(j * c, c), c)
        p_all = pend_ref[0, j]
        pairs = range(npair)
        col = lambda pi: slice(pi * LANES, (pi + 1) * LANES)
        s = [s_refs[pi][...] for pi in pairs]
        u = [_dot1(wt_ref[0, rows, col(pi)], s[pi], NT) + ut_ref[0, rows, col(pi)] for pi in pairs]
        for pi in pairs:
            ds = _dot1(jnp.concatenate([u[pi].astype(BF16), v_ref[0, rows, col(pi)]], axis=0),
                       jnp.concatenate([bt_ref[0, rows, col(pi)], kt_ref[0, rows, col(pi)]], axis=0), TN)
            s_refs[pi][...] = (s[pi] + jnp.where(same_head_b, ds, 0.0)) * p_all[:, col(pi)]
        y = [_dot1(rt_ref[0, rows, col(pi)], s[pi], NT) + (_dot1(mrb_ref[0, rows, col(pi)], stack2(u[pi]))
                                                           + yv_ref[0, rows, col(pi)]) for pi in pairs]
        for pi in pairs:
            mu = _dot_exact_rhs(y[pi], same_head) * (1.0 / HEAD_DIM)
            dlt = y[pi] - mu
            var = _dot_exact_rhs(dlt * dlt, same_head) * (1.0 / HEAD_DIM)
            yn = dlt * lax.rsqrt(var + GN_EPS) * lnw_ref[:, col(pi)] + lnb_ref[:, col(pi)]
            o_ref[0, rows, col(pi)] = (yn + bv_ref[0, rows, col(pi)]) * g_ref[0, rows, col(pi)]
        return carry

    lax.fori_loop(0, n_chunks, chunk, 0)

    @pl.when(ci == pl.num_programs(2) - 1)
    def _():
        for pi in range(npair):
            sout_ref[0, pi] = s_refs[pi][...]


def _rw_scan(prep, s0_bd, lnw, lnb, npair=4):
    b, t, _ = prep[0].shape
    tc = min(512, t)
    n_chunks = tc // CHUNK
    ng = WIDTH // (npair * LANES)
    row_spec = pl.BlockSpec((1, tc, npair * LANES), lambda bi, h, i: (bi, i, h))
    pend_spec = pl.BlockSpec((1, n_chunks, 1, npair * LANES), lambda bi, h, i: (bi, i, 0, h))
    st_spec = pl.BlockSpec((1, npair, LANES, LANES), lambda bi, h, i: (bi, h, 0, 0))
    vec_spec = pl.BlockSpec((1, npair * LANES), lambda bi, h, i: (0, h))
    return pl.pallas_call(
        functools.partial(_rw_scan_kernel, n_chunks=n_chunks, npair=npair),
        grid=(b, ng, t // tc),
        in_specs=[row_spec] * 10 + [pend_spec, st_spec, vec_spec, vec_spec],
        out_specs=[row_spec, st_spec],
        out_shape=[jax.ShapeDtypeStruct((b, t, WIDTH), F32),
                   jax.ShapeDtypeStruct((b, WIDTH // LANES, LANES, LANES), F32)],
        scratch_shapes=[pltpu.VMEM((LANES, LANES), F32)] * npair,
        compiler_params=_cparams("arbitrary", "arbitrary", "arbitrary"),
        name="rwkv_scan",
    )(*prep, s0_bd, lnw, lnb)


def _merge_kernel(oa_ref, ob_ref, sg_ref, x_ref, g1_ref, sc2_ref, sh2_ref, pa_ref, pb_ref, wo_ref, lng_ref, lnb_ref,
                  wr_ref, br_ref, cnt0_ref, h_ref, u2_ref, route_ref, cnt_out_ref, cnt_ref):
    sg = sg_ref[0]
    ya = _mm(oa_ref[0].astype(BF16), pa_ref[...])
    yb = _mm(ob_ref[0].astype(BF16), pb_ref[...])
    merged = sg[:, :D_MODEL].astype(F32) * ya + sg[:, D_MODEL:].astype(F32) * yb
    mix = _mm(merged.astype(BF16), wo_ref[...])
    hin = ALPHA * x_ref[0] + g1_ref[0] * mix
    mu = jnp.mean(hin, axis=-1, keepdims=True)
    dlt = hin - mu
    var = jnp.mean(dlt * dlt, axis=-1, keepdims=True)
    h = dlt * lax.rsqrt(var + LN_EPS) * lng_ref[...] + lnb_ref[...]
    h_ref[0] = h
    u2 = h * (1.0 + sc2_ref[0]) + sh2_ref[0]
    u2_ref[0] = u2

    logits = _dot3(u2, wr_ref[...]) + br_ref[...]
    lane_i = lax.broadcasted_iota(I32, logits.shape, 1)
    lane = lane_i.astype(F32)
    neg = -jnp.inf
    first_at = lambda vals, top: jnp.min(jnp.where(vals == top, lane, float(LANES)), axis=1, keepdims=True)
    lg = jnp.where(lane_i < N_GROUPS, logits, neg)
    mg = jnp.max(lg, axis=1, keepdims=True)
    g_idx = first_at(lg, mg)
    p_g = 1.0 / jnp.sum(jnp.exp(lg - mg), axis=1, keepdims=True)
    lo_lane = N_GROUPS + PER_GROUP * g_idx
    le = jnp.where((lane >= lo_lane) & (lane < lo_lane + PER_GROUP), logits, neg)
    m1 = jnp.max(le, axis=1, keepdims=True)
    i1 = first_at(le, m1)
    le2 = jnp.where(lane == i1, neg, le)
    m2 = jnp.max(le2, axis=1, keepdims=True)
    i2 = first_at(le2, m2)
    ratio = jnp.exp(m2 - m1)
    w1 = p_g / (1.0 + ratio)
    w2 = p_g * ratio / (1.0 + ratio)
    @pl.when((pl.program_id(0) == 0) & (pl.program_id(1) == 0))
    def _():
        cnt_ref[...] = cnt0_ref[...]

    tm = logits.shape[0]
    chosen = ((lane == i1) | (lane == i2)).astype(BF16)
    earlier = (lax.broadcasted_iota(I32, (tm, tm), 1) < lax.broadcasted_iota(I32, (tm, tm), 0)).astype(BF16)
    before = _mm(earlier, chosen) + cnt_ref[...]
    rank1 = jnp.sum(jnp.where(lane == i1, before, 0.0), axis=1, keepdims=True)
    rank2 = jnp.sum(jnp.where(lane == i2, before, 0.0), axis=1, keepdims=True)
    cnt_ref[...] = before[tm - 1:tm, :] + chosen[tm - 1:tm, :].astype(F32)
    cnt_out_ref[...] = cnt_ref[...]

    route = jnp.where(lane_i == 0, i1 - N_GROUPS, 0.0)
    for k, val in enumerate((i2 - N_GROUPS, w1, w2, rank1, rank2), start=1):
        route = jnp.where(lane_i == k, val, route)
    route_ref[0] = route


def _merge(o_a, o_b, sg, x, g1, sc2, sh2, wts, cnt0):
    b, t, d = x.shape
    tm = min(256, t)
    const = lambda shape: pl.BlockSpec(shape, lambda bi, i: (0,) * len(shape), pipeline_mode=pl.Buffered(1))
    seq = pl.BlockSpec((1, 1, d), lambda bi, i: (bi, 0, 0))
    row = lambda n: pl.BlockSpec((1, tm, n), lambda bi, i: (bi, i, 0))
    return pl.pallas_call(
        _merge_kernel,
        grid=(b, t // tm),
        in_specs=[row(WIDTH), row(WIDTH), row(2 * d), row(d), seq, seq, seq,
                  const((WIDTH, d)), const((WIDTH, d)), const((d, d)), const((1, d)), const((1, d)),
                  const((d, LANES)), const((1, LANES)), const((1, LANES))],
        out_specs=[row(d), row(d), row(LANES), pl.BlockSpec((1, LANES), lambda bi, i: (0, 0))],
        out_shape=[jax.ShapeDtypeStruct((b, t, d), F32), jax.ShapeDtypeStruct((b, t, d), F32),
                   jax.ShapeDtypeStruct((b, t, LANES), F32), jax.ShapeDtypeStruct((1, LANES), F32)],
        scratch_shapes=[pltpu.VMEM((1, LANES), F32)],
        compiler_params=_cparams("arbitrary", "arbitrary"),
        name="merge_route",
    )(o_a, o_b, sg, x, g1, sc2, sh2, wts["p_a"], wts["p_b"], wts["w_out"], wts["ln1_g"], wts["ln1_b"],
      wts["w_route"], wts["b_route"], cnt0)


def _moe_kernel(be_ref, nused_ref, tok_ref, x_hbm, wg_ref, wu_ref, wd_ref, y_ref, xbuf, sem, *, tb):
    i = pl.program_id(0)
    n_used = nused_ref[0]

    def start_gather(block, slot):
        def start(r, carry):
            pltpu.make_async_copy(x_hbm.at[pl.ds(tok_ref[block * tb + r], 1)], xbuf.at[slot, pl.ds(r, 1)],
                                  sem.at[slot]).start()
            return carry

        lax.fori_loop(0, tb, start, 0, unroll=8)

    @pl.when((i == 0) & (n_used > 0))
    def _():
        start_gather(0, 0)

    @pl.when(i + 1 < n_used)
    def _():
        start_gather(i + 1, (i + 1) % 2)

    @pl.when(i < n_used)
    def _():
        slot = i % 2
        pltpu.make_async_copy(x_hbm.at[pl.ds(0, tb)], xbuf.at[slot], sem.at[slot]).wait()
        x = xbuf[slot].astype(BF16)
        hg = _mm(x, wg_ref[0])
        hu = _mm(x, wu_ref[0])
        hid = (hg * _sigmoid(hg) * hu).astype(BF16)
        y_ref[...] = _mm(hid, wd_ref[0])

    @pl.when(i >= n_used)
    def _():
        y_ref[...] = jnp.zeros_like(y_ref)


def _moe(block_expert, n_used, slot_token, x_all, wg, wu, wd, tb):
    n_blocks = block_expert.shape[0]
    d = x_all.shape[1]
    grid_spec = pltpu.PrefetchScalarGridSpec(
        num_scalar_prefetch=3,
        grid=(n_blocks,),
        in_specs=[pl.BlockSpec(memory_space=pl.ANY),
                  pl.BlockSpec((1, d, D_EXPERT), lambda i, be, nu, tok: (be[i], 0, 0), pipeline_mode=pl.Buffered(1)),
                  pl.BlockSpec((1, d, D_EXPERT), lambda i, be, nu, tok: (be[i], 0, 0), pipeline_mode=pl.Buffered(1)),
                  pl.BlockSpec((1, D_EXPERT, d), lambda i, be, nu, tok: (be[i], 0, 0), pipeline_mode=pl.Buffered(1))],
        out_specs=pl.BlockSpec((tb, d), lambda i, be, nu, tok: (i, 0)),
        scratch_shapes=[pltpu.VMEM((2, tb, d), F32), pltpu.SemaphoreType.DMA((2,))],
    )
    return pl.pallas_call(
        functools.partial(_moe_kernel, tb=tb),
        grid_spec=grid_spec,
        out_shape=jax.ShapeDtypeStruct((n_blocks * tb, d), F32),
        compiler_params=_cparams("arbitrary"),
        name="moe_experts",
    )(block_expert, n_used, slot_token, x_all, wg, wu, wd)


def _final_kernel(dest_ref, y_hbm, h_ref, rw_ref, g2_ref, lng_ref, lnb_ref, o_ref, ybuf, sem, *, tm, per_seq):
    step = pl.program_id(0) * per_seq + pl.program_id(1)
    n_steps = pl.num_programs(0) * per_seq

    def start_gather(blk, slot):
        def start(r, carry):
            for k in range(2):
                pltpu.make_async_copy(y_hbm.at[pl.ds(dest_ref[2 * (blk * tm + r) + k], 1)],
                                      ybuf.at[slot, k, pl.ds(r, 1)], sem.at[slot]).start()
            return carry

        lax.fori_loop(0, tm, start, 0, unroll=4)

    @pl.when(step == 0)
    def _():
        start_gather(0, 0)

    @pl.when(step + 1 < n_steps)
    def _():
        start_gather(step + 1, (step + 1) % 2)

    slot = step % 2
    for k in range(2):
        pltpu.make_async_copy(y_hbm.at[pl.ds(0, tm)], ybuf.at[slot, k], sem.at[slot]).wait()
    rw = rw_ref[0]
    moe = rw[:, 2:3] * ybuf[slot, 0] + rw[:, 3:4] * ybuf[slot, 1]
    xin = ALPHA * h_ref[0] + g2_ref[0] * moe
    mu = jnp.mean(xin, axis=-1, keepdims=True)
    dlt = xin - mu
    var = jnp.mean(dlt * dlt, axis=-1, keepdims=True)
    o_ref[0] = dlt * lax.rsqrt(var + LN_EPS) * lng_ref[...] + lnb_ref[...]


def _final(dest, y_sorted, h, route, g2, ln2_g, ln2_b):
    b, t, d = h.shape
    tm = min(256, t)
    per_seq = t // tm
    grid_spec = pltpu.PrefetchScalarGridSpec(
        num_scalar_prefetch=1,
        grid=(b, per_seq),
        in_specs=[pl.BlockSpec(memory_space=pl.ANY),
                  pl.BlockSpec((1, tm, d), lambda bi, i, de: (bi, i, 0)),
                  pl.BlockSpec((1, tm, LANES), lambda bi, i, de: (bi, i, 0)),
                  pl.BlockSpec((1, 1, d), lambda bi, i, de: (bi, 0, 0)),
                  pl.BlockSpec((1, d), lambda bi, i, de: (0, 0)),
                  pl.BlockSpec((1, d), lambda bi, i, de: (0, 0))],
        out_specs=pl.BlockSpec((1, tm, d), lambda bi, i, de: (bi, i, 0)),
        scratch_shapes=[pltpu.VMEM((2, 2, tm, d), F32), pltpu.SemaphoreType.DMA((2,))],
    )
    return pl.pallas_call(
        functools.partial(_final_kernel, tm=tm, per_seq=per_seq),
        grid_spec=grid_spec,
        out_shape=jax.ShapeDtypeStruct((b, t, d), F32),
        compiler_params=_cparams("arbitrary", "arbitrary"),
        name="combine_ln2",
    )(dest, y_sorted, h, route, g2, ln2_g, ln2_b)


def _pad_rows(a, rows):
    return jnp.pad(a, ((0, rows - a.shape[0]), (0, 0)))


def _mixers(x, ada, past_k, past_v, s0, shift0, wts, sb_tiles, cnt0):
    b, t, d = x.shape
    sh1, sc1, g1, sh2, sc2, g2 = [a[:, None, :] for a in jnp.split(ada, 6, axis=-1)]
    q = _modmm(x, sc1, sh1, wts["w_q"], 1024, name="proj_q")
    k = _modmm(x, sc1, sh1, wts["w_k"], 1024, name="proj_k")
    v = _modmm(x, sc1, sh1, wts["w_v"], 1024, name="proj_v")
    m = _modmm(x, sc1, sh1, wts["w_m"], RW_PAD // 3, name="proj_rwkv")
    sg = _modmm(x, sc1, sh1, wts["w_g"], 1024, out_dtype=BF16, sigmoid_out=True, name="proj_gates")

    tq, tk, nl = sb_tiles
    past = past_k.shape[1]
    s_pad = -(-(past + t) // tk) * tk
    tail = jnp.zeros((b, s_pad - past - t, WIDTH), F32)
    k_all = jnp.concatenate([past_k.reshape(b, past, WIDTH), k, tail], axis=1)
    v_all = jnp.concatenate([past_v.reshape(b, past, WIDTH), v, tail], axis=1)
    o_a = _sb_attention(q, k_all, v_all, past, tq, tk, nl)

    shift0_p = jnp.pad(shift0, ((0, 0), (0, 0), (0, RW_PAD - RW_COLS)))
    prep = _rw_prep(m, shift0_p, wts["rw"])
    nhp = WIDTH // LANES
    s0_pairs = s0.reshape(b, nhp, 2, HEAD_DIM, HEAD_DIM)
    s0_bd = jnp.zeros((b, nhp, LANES, LANES), F32)
    s0_bd = s0_bd.at[:, :, :HEAD_DIM, :HEAD_DIM].set(s0_pairs[:, :, 0]).at[:, :, HEAD_DIM:, HEAD_DIM:].set(s0_pairs[:, :, 1])
    o_b, s_bd = _rw_scan(prep, s0_bd, wts["rw"]["lnx_w"], wts["rw"]["lnx_b"])
    s_new = jnp.stack([s_bd[:, :, :HEAD_DIM, :HEAD_DIM], s_bd[:, :, HEAD_DIM:, HEAD_DIM:]], axis=2)
    s_new = s_new.reshape(b, 2 * nhp, HEAD_DIM, HEAD_DIM)
    shift_new = m[:, t - 1:t, :RW_COLS]

    h, u2, route, cnt = _merge(o_a, o_b, sg, x, g1, sc2, sh2, wts, cnt0)
    return h, u2, route, cnt, g2, k, v, s_new, shift_new


def kernel(x_prompt, x_sample, cache_sb_k, cache_sb_v, state_rwkv, state_rwkv_shift, c_prompt, c_sample, w_ada, b_ada, w_in, tokshift_mu, rw_w0, rw_w2, rw_a0, rw_a2, rw_g2, rw_k_k, rw_k_a, rw_r_k, rw_lnx_w, rw_lnx_b, w_branch_a, w_branch_b, w_out, ln1_g, ln1_b, w_router_group, b_router_group, w_router_expert, b_router_expert, w_exp_gate, w_exp_up, w_exp_down, ln2_g, ln2_b):
    depth = w_ada.shape[0]
    assert depth == 1
    bp, tp, d = x_prompt.shape
    bs, ts, _ = x_sample.shape
    l = 0

    w_in_l = w_in[l]
    w_rw = w_in_l[:, 3 * WIDTH:3 * WIDTH + RW_COLS]
    row_vec = lambda a: a.reshape(1, -1)
    pad_cols = lambda a, n: jnp.pad(a, ((0, 0), (0, n - a.shape[1])))
    n_lg = RW_PAD - 3 * WIDTH - LANES
    wts = {
        "w_q": w_in_l[:, 0:WIDTH].astype(BF16),
        "w_k": w_in_l[:, WIDTH:2 * WIDTH].astype(BF16),
        "w_v": w_in_l[:, 2 * WIDTH:3 * WIDTH].astype(BF16),
        "w_m": pad_cols(w_rw, RW_PAD).astype(BF16),
        "w_g": w_in_l[:, 3 * WIDTH + RW_COLS:].astype(BF16),
        "p_a": w_branch_a[l].astype(BF16),
        "p_b": w_branch_b[l].astype(BF16),
        "w_out": w_out[l].astype(BF16),
        "ln1_g": row_vec(ln1_g[l]),
        "ln1_b": row_vec(ln1_b[l]),
        "w_route": pad_cols(jnp.concatenate([w_router_group[l], w_router_expert[l]], axis=1), LANES),
        "b_route": pad_cols(row_vec(jnp.concatenate([b_router_group[l], b_router_expert[l]])), LANES),
        "rw": {
            "mu": pad_cols(row_vec(tokshift_mu[l]), RW_PAD),
            "w0": row_vec(rw_w0[l]),
            "w2": _pad_rows(rw_w2[l], LANES),
            "a0": row_vec(rw_a0[l]),
            "a2": jnp.concatenate([jnp.zeros((LORA_W, WIDTH), F32), rw_a2[l]], axis=0),
            "g2": _pad_rows(rw_g2[l], n_lg),
            "k_k": row_vec(rw_k_k[l]),
            "k_a": row_vec(rw_k_a[l]),
            "r_k": row_vec(rw_r_k[l]),
            "lnx_w": row_vec(rw_lnx_w[l]),
            "lnx_b": row_vec(rw_lnx_b[l]),
        },
    }

    rows = -(-(bp + bs) // 8) * 8
    c_all = _pad_rows(jnp.concatenate([c_prompt, c_sample], axis=0), rows)
    ada = _ada(c_all, w_ada[l], b_ada[l])

    empty = jnp.zeros((bp, 0, WIDTH), F32)
    zero_state = jnp.zeros((bp, WIDTH // HEAD_DIM, HEAD_DIM, HEAD_DIM), F32)
    zero_shift = jnp.zeros((bp, 1, RW_COLS), F32)
    hp, u2p, route_p, cnt_p, g2p, kp, vp, sp_, shp = _mixers(
        x_prompt, ada[:bp], empty, empty, zero_state, zero_shift, wts, (min(256, tp), 256, 4),
        jnp.zeros((1, LANES), F32))
    hs, u2s, route_s, cnt_all, g2s, ks, vs, ss_, shs = _mixers(
        x_sample, ada[bp:bp + bs], cache_sb_k[l], cache_sb_v[l], state_rwkv[l], state_rwkv_shift[l], wts,
        (min(256, ts), 256, 4), cnt_p)

    n_p, n_s = bp * tp, bs * ts
    n_tok = n_p + n_s
    tb = 512
    route_all = jnp.concatenate([route_p.reshape(n_p, LANES), route_s.reshape(n_s, LANES)], axis=0)
    flat_e = route_all[:, 0:2].astype(I32).reshape(n_tok * 2)
    rank = route_all[:, 4:6].astype(I32).reshape(n_tok * 2)
    counts = cnt_all[0, N_GROUPS:N_GROUPS + N_EXPERTS].astype(I32)
    padded = (counts + tb - 1) // tb * tb
    pend = jnp.cumsum(padded)
    pstart = pend - padded
    dest = (pstart[flat_e] + rank).astype(I32)
    n_blocks = (2 * n_tok + N_EXPERTS * (tb - 1)) // tb + 1
    slot_token = jnp.zeros((n_blocks * tb,), I32).at[dest].set(jnp.arange(2 * n_tok, dtype=I32) // 2)
    block_start = jnp.arange(n_blocks, dtype=I32) * tb
    block_expert = jnp.minimum(jnp.searchsorted(pend, block_start, side="right"), N_EXPERTS - 1).astype(I32)
    n_used = (pend[-1] // tb).astype(I32).reshape(1)

    u2_all = jnp.concatenate([u2p.reshape(n_p, d), u2s.reshape(n_s, d)], axis=0)
    y_sorted = _moe(block_expert, n_used, slot_token, u2_all, w_exp_gate[l].astype(BF16),
                    w_exp_up[l].astype(BF16), w_exp_down[l].astype(BF16), tb)

    ln2g, ln2b = row_vec(ln2_g[l]), row_vec(ln2_b[l])
    y_prompt = _final(dest[:2 * n_p], y_sorted, hp, route_p, g2p, ln2g, ln2b)
    y_sample = _final(dest[2 * n_p:], y_sorted, hs, route_s, g2s, ln2g, ln2b)

    heads = lambda a: a.reshape(1, a.shape[0], a.shape[1], WIDTH // HEAD_DIM, HEAD_DIM)
    return (y_prompt, y_sample, heads(kp), heads(vp), sp_[None], shp[None],
            heads(ks), heads(vs), ss_[None], shs[None])
```

```python
import functools

import jax
import jax.numpy as jnp
import numpy as np
from jax import lax
from jax.experimental import pallas as pl
from jax.experimental.pallas import tpu as pltpu

F32 = jnp.float32
BF16 = jnp.bfloat16
I32 = jnp.int32

D_MODEL = 2048
HEAD_DIM = 64
WIDTH = 1024
LORA_W = 64
LORA_A = 64
LORA_G = 160
RW_COLS = 3 * WIDTH + LORA_W + LORA_A + LORA_G
RW_PAD = 3456
N_GROUPS = 4
PER_GROUP = 8
N_EXPERTS = 32
D_EXPERT = 1024
LN_EPS = 1e-5
GN_EPS = 64e-5
ALPHA = 2.0 ** 0.25
LOG2E = 1.4426950408889634
F32_UNDERFLOW_LOG2 = -150.0
LANES = 128
CHUNK = 64
VMEM_LIMIT = 56 * 1024 * 1024

NN = (((1,), (0,)), ((), ()))
NT = (((1,), (1,)), ((), ()))
TN = (((0,), (0,)), ((), ()))


def _cparams(*sem):
    return pltpu.CompilerParams(dimension_semantics=sem, vmem_limit_bytes=VMEM_LIMIT)


def _mm(a, b, dims=NN):
    return lax.dot_general(a, b, dims, preferred_element_type=F32)


def _split(x):
    hi = x.astype(BF16)
    lo = (x - hi.astype(F32)).astype(BF16)
    return hi, lo


def _split3(x):
    hi = x.astype(BF16)
    r1 = x - hi.astype(F32)
    mid = r1.astype(BF16)
    lo = (r1 - mid.astype(F32)).astype(BF16)
    return hi, mid, lo


def _dot3(a, b, dims=NN):
    ah, al = _split(a)
    bh, bl = _split(b)
    return _mm(ah, bh, dims) + (_mm(al, bh, dims) + _mm(ah, bl, dims))


def _dot1(a, b, dims=NN):
    return _mm(a.astype(BF16), b.astype(BF16), dims)


def _dot_exact_rhs(a, b_bf16, terms=2):
    parts = _split(a) if terms == 2 else _split3(a)
    out = _mm(parts[0], b_bf16)
    for p in parts[1:]:
        out = out + _mm(p, b_bf16)
    return out


def _softplus(z):
    return jnp.maximum(z, 0.0) + jnp.log(1.0 + jnp.exp(-jnp.abs(z)))


def _sigmoid(z):
    return 1.0 / (1.0 + jnp.exp(-z))


def _ada_kernel(c_ref, w_ref, b_ref, o_ref):
    c = c_ref[...]
    o_ref[...] = _dot3(c * _sigmoid(c), w_ref[...]) + b_ref[...]


def _ada(c_all, w_ada, b_ada):
    rows = c_all.shape[0]
    n = w_ada.shape[1]
    tn = 1024
    return pl.pallas_call(
        _ada_kernel,
        grid=(n // tn,),
        in_specs=[pl.BlockSpec((rows, D_MODEL), lambda j: (0, 0)),
                  pl.BlockSpec((D_MODEL, tn), lambda j: (0, j)),
                  pl.BlockSpec((1, tn), lambda j: (0, j))],
        out_specs=pl.BlockSpec((rows, tn), lambda j: (0, j)),
        out_shape=jax.ShapeDtypeStruct((rows, n), F32),
        compiler_params=_cparams("arbitrary"),
        name="ada",
    )(c_all, w_ada, b_ada.reshape(1, n))


def _modmm_kernel(x_ref, sc_ref, sh_ref, w_ref, o_ref, u_ref, *, sigmoid_out):
    @pl.when(pl.program_id(2) == 0)
    def _():
        u_ref[...] = (x_ref[0] * (1.0 + sc_ref[0]) + sh_ref[0]).astype(BF16)

    y = _mm(u_ref[...], w_ref[...])
    if sigmoid_out:
        y = _sigmoid(y)
    o_ref[0] = y.astype(o_ref.dtype)


def _modmm(x, sc, sh, w_bf16, tn, out_dtype=F32, sigmoid_out=False, name="proj"):
    b, t, d = x.shape
    n = w_bf16.shape[1]
    tm = min(1024, t)
    return pl.pallas_call(
        functools.partial(_modmm_kernel, sigmoid_out=sigmoid_out),
        grid=(b, t // tm, n // tn),
        in_specs=[pl.BlockSpec((1, tm, d), lambda bi, i, j: (bi, i, 0)),
                  pl.BlockSpec((1, 1, d), lambda bi, i, j: (bi, 0, 0)),
                  pl.BlockSpec((1, 1, d), lambda bi, i, j: (bi, 0, 0)),
                  pl.BlockSpec((d, tn), lambda bi, i, j: (0, j))],
        out_specs=pl.BlockSpec((1, tm, tn), lambda bi, i, j: (bi, i, j)),
        out_shape=jax.ShapeDtypeStruct((b, t, n), out_dtype),
        scratch_shapes=[pltpu.VMEM((tm, d), BF16)],
        compiler_params=_cparams("arbitrary", "arbitrary", "arbitrary"),
        name=name,
    )(x, sc, sh, w_bf16)


def _sb_kernel(q_ref, k_hbm, v_hbm, o_ref, kbuf, vbuf, sem, done_ref, *scratch, tq, tk, past, nl):
    accs, cars = scratch[:2 * nl], scratch[2 * nl:]
    bi, g, qi = pl.program_id(0), pl.program_id(1), pl.program_id(2)
    width = nl * LANES
    top = (past + (qi + 1) * tq - 2) // tk

    def kv_copies(kb, slot):
        src = (bi, pl.ds(kb * tk, tk), pl.ds(g * width, width))
        return (pltpu.make_async_copy(k_hbm.at[src], kbuf.at[slot], sem.at[0, slot]),
                pltpu.make_async_copy(v_hbm.at[src], vbuf.at[slot], sem.at[1, slot]))

    for cp in kv_copies(top, 0):
        cp.start()
    for ref in scratch:
        ref[...] = jnp.zeros_like(ref)
    done_ref[0] = 0

    later = (lax.broadcasted_iota(I32, (tk, tk), 0) > lax.broadcasted_iota(I32, (tk, tk), 1)).astype(BF16)
    lane = lax.broadcasted_iota(I32, (tq, LANES), 1)

    def sweep(kb, slot, masked):
        if masked:
            pos = past + qi * tq + lax.broadcasted_iota(I32, (tq, tk), 0)
            kpos = kb * tk + lax.broadcasted_iota(I32, (tq, tk), 1)
            mask = kpos < pos
        for l in range(nl):
            cols = slice(l * LANES, (l + 1) * LANES)
            q = q_ref[0, :, cols] * (HEAD_DIM ** -0.5 * LOG2E)
            kbf = kbuf[slot, :, cols].astype(BF16)
            vbf = vbuf[slot, :, cols].astype(BF16)
            for hh in range(2):
                head = (lane < HEAD_DIM) if hh == 0 else (lane >= HEAD_DIM)
                z = _mm(jnp.where(head, q, 0.0).astype(BF16), kbf, NT)
                soft = jnp.log(1.0 + jnp.exp2(-jnp.abs(z))) * LOG2E
                log_beta = jnp.minimum(z, 0.0) - soft
                log_keep = log_beta - z
                if masked:
                    log_keep = jnp.where(mask, log_keep, 0.0)
                car = cars[2 * l + hh]
                after = _mm(log_keep.astype(BF16), later) + car[...]
                w = jnp.exp2(log_beta + after)
                if masked:
                    w = jnp.where(mask, w, 0.0)
                accs[2 * l + hh][...] += _mm(w.astype(BF16), vbf)
                car[...] += jnp.sum(log_keep, axis=1, keepdims=True)
        worst = cars[0][...]
        for car in cars[1:]:
            worst = jnp.maximum(worst, car[...])
        done_ref[0] = (jnp.max(worst) < F32_UNDERFLOW_LOG2).astype(I32)

    def body(kb):
        slot = (top - kb) % 2
        for cp in kv_copies(kb, slot):
            cp.wait()

        @pl.when(kb > 0)
        def _():
            for cp in kv_copies(kb - 1, 1 - slot):
                cp.start()

        diag = (kb + 1) * tk - 1 >= past + qi * tq

        @pl.when(diag)
        def _():
            sweep(kb, slot, True)

        @pl.when(jnp.logical_not(diag))
        def _():
            sweep(kb, slot, False)

        return kb - 1

    kb_end = lax.while_loop(lambda kb: (kb >= 0) & (done_ref[0] == 0), body, top)

    @pl.when(kb_end >= 0)
    def _():
        for cp in kv_copies(kb_end, (top - kb_end) % 2):
            cp.wait()

    for l in range(nl):
        o_ref[0, :, l * LANES:(l + 1) * LANES] = jnp.where(lane < HEAD_DIM, accs[2 * l][...], accs[2 * l + 1][...])


def _sb_attention(q, k_all, v_all, past, tq, tk, nl):
    b, t, _ = q.shape
    width = nl * LANES
    q_spec = pl.BlockSpec((1, tq, width), lambda bi, g, qi: (bi, qi, g))
    return pl.pallas_call(
        functools.partial(_sb_kernel, tq=tq, tk=tk, past=past, nl=nl),
        grid=(b, WIDTH // width, t // tq),
        in_specs=[q_spec, pl.BlockSpec(memory_space=pl.ANY), pl.BlockSpec(memory_space=pl.ANY)],
        out_specs=q_spec,
        out_shape=jax.ShapeDtypeStruct((b, t, WIDTH), F32),
        scratch_shapes=[pltpu.VMEM((2, tk, width), F32), pltpu.VMEM((2, tk, width), F32),
                        pltpu.SemaphoreType.DMA((2, 2)), pltpu.SMEM((1,), I32)]
        + [pltpu.VMEM((tq, LANES), F32)] * (2 * nl) + [pltpu.VMEM((tq, 1), F32)] * (2 * nl),
        compiler_params=_cparams("arbitrary", "arbitrary", "arbitrary"),
        name="sb_attention",
    )(q, k_all, v_all)


def _head_sum(x, same_head):
    return jnp.concatenate(
        [_dot_exact_rhs(x[:, c * LANES:(c + 1) * LANES], same_head) for c in range(WIDTH // LANES)], axis=1)


def _rw_prep_kernel(m_ref, sh0_ref, mu_ref, w0_ref, w2_ref, a0_ref, a2_ref, g2_ref, kk_ref, ka_ref, rk_ref,
                    wt_ref, ut_ref, rt_ref, mrb_ref, yv_ref, bt_ref, kt_ref, v_ref, bv_ref, g_ref, pend_ref, prev_ref,
                    *, tr):
    @pl.when(pl.program_id(1) == 0)
    def _():
        prev_ref[...] = sh0_ref[0]

    m = m_ref[0]
    row = lax.broadcasted_iota(I32, m.shape, 0)
    m_prev = jnp.where(row == 0, prev_ref[...], pltpu.roll(m, 1, 0))
    prev_ref[...] = m[tr - 1:tr, :]
    ms = m + (m_prev - m) * mu_ref[...]
    r = ms[:, 0:WIDTH]
    k = ms[:, WIDTH:2 * WIDTH]
    v = ms[:, 2 * WIDTH:3 * WIDTH]
    lora_wa = ms[:, 3 * WIDTH:3 * WIDTH + LANES]
    lora_g = ms[:, 3 * WIDTH + LANES:RW_PAD]

    w_log = -_softplus(-(w0_ref[...] + _dot3(jnp.tanh(lora_wa), w2_ref[...]))) - 0.5
    log_decay = -jnp.exp(w_log)
    a = _sigmoid(a0_ref[...] + _dot3(lora_wa, a2_ref[...]))
    g = _dot3(_sigmoid(lora_g), g2_ref[...])

    same_head = (lax.broadcasted_iota(I32, (LANES, LANES), 0) // HEAD_DIM ==
                 lax.broadcasted_iota(I32, (LANES, LANES), 1) // HEAD_DIM).astype(BF16)
    kk = k * kk_ref[...]
    kk = kk / jnp.maximum(jnp.sqrt(_head_sum(kk * kk, same_head)), 1e-12)
    kh = k * (1.0 + (a - 1.0) * ka_ref[...])
    bonus = _head_sum(r * kh * rk_ref[...], same_head)

    ti = lax.broadcasted_iota(I32, (tr, tr), 0)
    si = lax.broadcasted_iota(I32, (tr, tr), 1)
    tri = (si <= ti).astype(BF16)
    parts = _split3(log_decay)
    cl = _mm(tri, parts[0]) + (_mm(tri, parts[1]) + _mm(tri, parts[2]))
    ec = jnp.exp(cl)
    inv = jnp.exp(-cl)
    rt = r * ec
    kt = kh * inv
    at = -kk * jnp.exp(cl - log_decay)
    bt = kk * a * inv
    rt_ref[0] = rt.astype(BF16)
    kt_ref[0] = kt.astype(BF16)
    bt_ref[0] = bt.astype(BF16)
    v_ref[0] = v.astype(BF16)
    bv_ref[0] = bonus * v
    g_ref[0] = g
    pend_ref[0, 0] = ec[tr - 1:tr, :]

    lane = lax.broadcasted_iota(I32, (tr, LANES), 1)
    first = lane < HEAD_DIM
    t_row = lax.broadcasted_iota(I32, (tr, LANES), 0)
    s_col = lane % HEAD_DIM
    strict = s_col < t_row
    incl = s_col <= t_row
    eye2 = (s_col == t_row).astype(F32)

    def stack2(x):
        return jnp.concatenate([jnp.where(first, x, 0.0), jnp.where(first, 0.0, x)], axis=0)

    pairs = range(WIDTH // LANES)
    col = lambda hp: slice(hp * LANES, (hp + 1) * LANES)
    rows2 = lambda top, bottom: jnp.concatenate([top, bottom], axis=0)
    a_t = [at[:, col(hp)] for hp in pairs]
    l_ab, g_v = [], []
    for hp in pairs:
        b_s, k_s, v_s = stack2(bt[:, col(hp)]), stack2(kt[:, col(hp)]), stack2(v[:, col(hp)])
        ar = rows2(a_t[hp], rt[:, col(hp)])
        on_b = _dot1(ar, b_s, NT)
        on_k = _dot1(ar, k_s, NT)
        l_ab.append(jnp.where(strict, on_b[:tr], 0.0))
        mrb_ref[0, :, col(hp)] = jnp.where(incl, on_b[tr:], 0.0).astype(BF16)
        on_v = _dot1(rows2(jnp.where(strict, on_k[:tr], 0.0), jnp.where(incl, on_k[tr:], 0.0)), v_s)
        g_v.append(on_v[:tr])
        yv_ref[0, :, col(hp)] = on_v[tr:]
    tinv = [eye2 + l for l in l_ab]
    pw = [_dot1(x, stack2(x)) for x in l_ab]
    for _ in range(4):
        both = [_dot1(rows2(p_, t_), stack2(p_)) for p_, t_ in zip(pw, tinv)]
        tinv = [t_ + r_[tr:] for t_, r_ in zip(tinv, both)]
        pw = [r_[:tr] for r_ in both]
    tinv = [t_ + _dot1(t_, stack2(p_)) for t_, p_ in zip(tinv, pw)]
    for hp in pairs:
        wu = _dot1(tinv[hp], jnp.concatenate([stack2(a_t[hp]), stack2(g_v[hp])], axis=1))
        wt_ref[0, :, col(hp)] = wu[:, :LANES].astype(BF16)
        ut_ref[0, :, col(hp)] = wu[:, LANES:]


def _rw_prep(m, shift0, rwp):
    b, t, _ = m.shape
    tr = CHUNK
    row_spec = pl.BlockSpec((1, tr, WIDTH), lambda bi, i: (bi, i, 0))
    vec = lambda n: pl.BlockSpec((1, n), lambda bi, i: (0, 0))
    mat = lambda r: pl.BlockSpec((r, WIDTH), lambda bi, i: (0, 0))
    out = jax.ShapeDtypeStruct((b, t, WIDTH), F32)
    out_mxu = jax.ShapeDtypeStruct((b, t, WIDTH), BF16)
    return pl.pallas_call(
        functools.partial(_rw_prep_kernel, tr=tr),
        grid=(b, t // tr),
        in_specs=[pl.BlockSpec((1, tr, RW_PAD), lambda bi, i: (bi, i, 0)),
                  pl.BlockSpec((1, 1, RW_PAD), lambda bi, i: (bi, 0, 0)),
                  vec(RW_PAD), vec(WIDTH), mat(LANES), vec(WIDTH), mat(LANES), mat(RW_PAD - 3 * WIDTH - LANES),
                  vec(WIDTH), vec(WIDTH), vec(WIDTH)],
        out_specs=[row_spec] * 10 + [pl.BlockSpec((1, 1, 1, WIDTH), lambda bi, i: (bi, i, 0, 0))],
        out_shape=[out_mxu, out, out_mxu, out_mxu, out, out_mxu, out_mxu, out_mxu, out, out,
                   jax.ShapeDtypeStruct((b, t // tr, 1, WIDTH), F32)],
        scratch_shapes=[pltpu.VMEM((1, RW_PAD), F32)],
        compiler_params=_cparams("arbitrary", "arbitrary"),
        name="rwkv_prep",
    )(m, shift0, rwp["mu"], rwp["w0"], rwp["w2"], rwp["a0"], rwp["a2"], rwp["g2"], rwp["k_k"], rwp["k_a"],
      rwp["r_k"])


def _rw_scan_kernel(wt_ref, ut_ref, rt_ref, mrb_ref, yv_ref, bt_ref, kt_ref, v_ref, bv_ref, g_ref, pend_ref,
                    s0_ref, lnw_ref, lnb_ref, o_ref, sout_ref, *s_refs, n_chunks, npair):
    ci = pl.program_id(2)

    @pl.when(ci == 0)
    def _():
        for pi in range(npair):
            s_refs[pi][...] = s0_ref[0, pi]

    c = CHUNK
    first = lax.broadcasted_iota(I32, (c, LANES), 1) < HEAD_DIM
    vi = lax.broadcasted_iota(I32, (LANES, LANES), 0)
    kj = lax.broadcasted_iota(I32, (LANES, LANES), 1)
    same_head_b = (vi // HEAD_DIM) == (kj // HEAD_DIM)
    same_head = same_head_b.astype(BF16)

    def stack2(x):
        return jnp.concatenate([jnp.where(first, x, 0.0), jnp.where(first, 0.0, x)], axis=0)

    def chunk(j, carry):
        rows = pl.ds(pl.multiple_of(j * c, c), c)
        p_all = pend_ref[0, j]
        pairs = range(npair)
        col = lambda pi: slice(pi * LANES, (pi + 1) * LANES)
        s = [s_refs[pi][...] for pi in pairs]
        u = [_dot1(wt_ref[0, rows, col(pi)], s[pi], NT) + ut_ref[0, rows, col(pi)] for pi in pairs]
        for pi in pairs:
            ds = _dot1(jnp.concatenate([u[pi].astype(BF16), v_ref[0, rows, col(pi)]], axis=0),
                       jnp.concatenate([bt_ref[0, rows, col(pi)], kt_ref[0, rows, col(pi)]], axis=0), TN)
            s_refs[pi][...] = (s[pi] + jnp.where(same_head_b, ds, 0.0)) * p_all[:, col(pi)]
        y = [_dot1(rt_ref[0, rows, col(pi)], s[pi], NT) + (_dot1(mrb_ref[0, rows, col(pi)], stack2(u[pi]))
                                                           + yv_ref[0, rows, col(pi)]) for pi in pairs]
        for pi in pairs:
            mu = _dot_exact_rhs(y[pi], same_head) * (1.0 / HEAD_DIM)
            dlt = y[pi] - mu
            var = _dot_exact_rhs(dlt * dlt, same_head) * (1.0 / HEAD_DIM)
            yn = dlt * lax.rsqrt(var + GN_EPS) * lnw_ref[:, col(pi)] + lnb_ref[:, col(pi)]
            o_ref[0, rows, col(pi)] = (yn + bv_ref[0, rows, col(pi)]) * g_ref[0, rows, col(pi)]
        return carry

    lax.fori_loop(0, n_chunks, chunk, 0)

    @pl.when(ci == pl.num_programs(2) - 1)
    def _():
        for pi in range(npair):
            sout_ref[0, pi] = s_refs[pi][...]


def _rw_scan(prep, s0_bd, lnw, lnb, npair=4):
    b, t, _ = prep[0].shape
    tc = min(512, t)
    n_chunks = tc // CHUNK
    ng = WIDTH // (npair * LANES)
    row_spec = pl.BlockSpec((1, tc, npair * LANES), lambda bi, h, i: (bi, i, h))
    pend_spec = pl.BlockSpec((1, n_chunks, 1, npair * LANES), lambda bi, h, i: (bi, i, 0, h))
    st_spec = pl.BlockSpec((1, npair, LANES, LANES), lambda bi, h, i: (bi, h, 0, 0))
    vec_spec = pl.BlockSpec((1, npair * LANES), lambda bi, h, i: (0, h))
    return pl.pallas_call(
        functools.partial(_rw_scan_kernel, n_chunks=n_chunks, npair=npair),
        grid=(b, ng, t // tc),
        in_specs=[row_spec] * 10 + [pend_spec, st_spec, vec_spec, vec_spec],
        out_specs=[row_spec, st_spec],
        out_shape=[jax.ShapeDtypeStruct((b, t, WIDTH), F32),
                   jax.ShapeDtypeStruct((b, WIDTH // LANES, LANES, LANES), F32)],
        scratch_shapes=[pltpu.VMEM((LANES, LANES), F32)] * npair,
        compiler_params=_cparams("arbitrary", "arbitrary", "arbitrary"),
        name="rwkv_scan",
    )(*prep, s0_bd, lnw, lnb)


def _merge_kernel(oa_ref, ob_ref, sg_ref, x_ref, g1_ref, sc2_ref, sh2_ref, pa_ref, pb_ref, wo_ref, lng_ref, lnb_ref,
                  wr_ref, br_ref, cnt0_ref, h_ref, u2_ref, route_ref, cnt_out_ref, cnt_ref):
    sg = sg_ref[0]
    ya = _mm(oa_ref[0].astype(BF16), pa_ref[...])
    yb = _mm(ob_ref[0].astype(BF16), pb_ref[...])
    merged = sg[:, :D_MODEL].astype(F32) * ya + sg[:, D_MODEL:].astype(F32) * yb
    mix = _mm(merged.astype(BF16), wo_ref[...])
    hin = ALPHA * x_ref[0] + g1_ref[0] * mix
    mu = jnp.mean(hin, axis=-1, keepdims=True)
    dlt = hin - mu
    var = jnp.mean(dlt * dlt, axis=-1, keepdims=True)
    h = dlt * lax.rsqrt(var + LN_EPS) * lng_ref[...] + lnb_ref[...]
    h_ref[0] = h
    u2 = h * (1.0 + sc2_ref[0]) + sh2_ref[0]
    u2_ref[0] = u2

    logits = _dot3(u2, wr_ref[...]) + br_ref[...]
    lane_i = lax.broadcasted_iota(I32, logits.shape, 1)
    lane = lane_i.astype(F32)
    neg = -jnp.inf
    first_at = lambda vals, top: jnp.min(jnp.where(vals == top, lane, float(LANES)), axis=1, keepdims=True)
    lg = jnp.where(lane_i < N_GROUPS, logits, neg)
    mg = jnp.max(lg, axis=1, keepdims=True)
    g_idx = first_at(lg, mg)
    p_g = 1.0 / jnp.sum(jnp.exp(lg - mg), axis=1, keepdims=True)
    lo_lane = N_GROUPS + PER_GROUP * g_idx
    le = jnp.where((lane >= lo_lane) & (lane < lo_lane + PER_GROUP), logits, neg)
    m1 = jnp.max(le, axis=1, keepdims=True)
    i1 = first_at(le, m1)
    le2 = jnp.where(lane == i1, neg, le)
    m2 = jnp.max(le2, axis=1, keepdims=True)
    i2 = first_at(le2, m2)
    ratio = jnp.exp(m2 - m1)
    w1 = p_g / (1.0 + ratio)
    w2 = p_g * ratio / (1.0 + ratio)
    @pl.when((pl.program_id(0) == 0) & (pl.program_id(1) == 0))
    def _():
        cnt_ref[...] = cnt0_ref[...]

    tm = logits.shape[0]
    chosen = ((lane == i1) | (lane == i2)).astype(BF16)
    earlier = (lax.broadcasted_iota(I32, (tm, tm), 1) < lax.broadcasted_iota(I32, (tm, tm), 0)).astype(BF16)
    before = _mm(earlier, chosen) + cnt_ref[...]
    rank1 = jnp.sum(jnp.where(lane == i1, before, 0.0), axis=1, keepdims=True)
    rank2 = jnp.sum(jnp.where(lane == i2, before, 0.0), axis=1, keepdims=True)
    cnt_ref[...] = before[tm - 1:tm, :] + chosen[tm - 1:tm, :].astype(F32)
    cnt_out_ref[...] = cnt_ref[...]

    route = jnp.where(lane_i == 0, i1 - N_GROUPS, 0.0)
    for k, val in enumerate((i2 - N_GROUPS, w1, w2, rank1, rank2), start=1):
        route = jnp.where(lane_i == k, val, route)
    route_ref[0] = route


def _merge(o_a, o_b, sg, x, g1, sc2, sh2, wts, cnt0):
    b, t, d = x.shape
    tm = min(256, t)
    const = lambda shape: pl.BlockSpec(shape, lambda bi, i: (0,) * len(shape), pipeline_mode=pl.Buffered(1))
    seq = pl.BlockSpec((1, 1, d), lambda bi, i: (bi, 0, 0))
    row = lambda n: pl.BlockSpec((1, tm, n), lambda bi, i: (bi, i, 0))
    return pl.pallas_call(
        _merge_kernel,
        grid=(b, t // tm),
        in_specs=[row(WIDTH), row(WIDTH), row(2 * d), row(d), seq, seq, seq,
                  const((WIDTH, d)), const((WIDTH, d)), const((d, d)), const((1, d)), const((1, d)),
                  const((d, LANES)), const((1, LANES)), const((1, LANES))],
        out_specs=[row(d), row(d), row(LANES), pl.BlockSpec((1, LANES), lambda bi, i: (0, 0))],
        out_shape=[jax.ShapeDtypeStruct((b, t, d), F32), jax.ShapeDtypeStruct((b, t, d), F32),
                   jax.ShapeDtypeStruct((b, t, LANES), F32), jax.ShapeDtypeStruct((1, LANES), F32)],
        scratch_shapes=[pltpu.VMEM((1, LANES), F32)],
        compiler_params=_cparams("arbitrary", "arbitrary"),
        name="merge_route",
    )(o_a, o_b, sg, x, g1, sc2, sh2, wts["p_a"], wts["p_b"], wts["w_out"], wts["ln1_g"], wts["ln1_b"],
      wts["w_route"], wts["b_route"], cnt0)


def _moe_kernel(be_ref, nused_ref, tok_ref, x_hbm, wg_ref, wu_ref, wd_ref, y_ref, xbuf, sem, *, tb):
    i = pl.program_id(0)
    n_used = nused_ref[0]

    def start_gather(block, slot):
        def start(r, carry):
            pltpu.make_async_copy(x_hbm.at[pl.ds(tok_ref[block * tb + r], 1)], xbuf.at[slot, pl.ds(r, 1)],
                                  sem.at[slot]).start()
            return carry

        lax.fori_loop(0, tb, start, 0, unroll=8)

    @pl.when((i == 0) & (n_used > 0))
    def _():
        start_gather(0, 0)

    @pl.when(i + 1 < n_used)
    def _():
        start_gather(i + 1, (i + 1) % 2)

    @pl.when(i < n_used)
    def _():
        slot = i % 2
        pltpu.make_async_copy(x_hbm.at[pl.ds(0, tb)], xbuf.at[slot], sem.at[slot]).wait()
        x = xbuf[slot].astype(BF16)
        hg = _mm(x, wg_ref[0])
        hu = _mm(x, wu_ref[0])
        hid = (hg * _sigmoid(hg) * hu).astype(BF16)
        y_ref[...] = _mm(hid, wd_ref[0])

    @pl.when(i >= n_used)
    def _():
        y_ref[...] = jnp.zeros_like(y_ref)


def _moe(block_expert, n_used, slot_token, x_all, wg, wu, wd, tb):
    n_blocks = block_expert.shape[0]
    d = x_all.shape[1]
    grid_spec = pltpu.PrefetchScalarGridSpec(
        num_scalar_prefetch=3,
        grid=(n_blocks,),
        in_specs=[pl.BlockSpec(memory_space=pl.ANY),
                  pl.BlockSpec((1, d, D_EXPERT), lambda i, be, nu, tok: (be[i], 0, 0), pipeline_mode=pl.Buffered(1)),
                  pl.BlockSpec((1, d, D_EXPERT), lambda i, be, nu, tok: (be[i], 0, 0), pipeline_mode=pl.Buffered(1)),
                  pl.BlockSpec((1, D_EXPERT, d), lambda i, be, nu, tok: (be[i], 0, 0), pipeline_mode=pl.Buffered(1))],
        out_specs=pl.BlockSpec((tb, d), lambda i, be, nu, tok: (i, 0)),
        scratch_shapes=[pltpu.VMEM((2, tb, d), F32), pltpu.SemaphoreType.DMA((2,))],
    )
    return pl.pallas_call(
        functools.partial(_moe_kernel, tb=tb),
        grid_spec=grid_spec,
        out_shape=jax.ShapeDtypeStruct((n_blocks * tb, d), F32),
        compiler_params=_cparams("arbitrary"),
        name="moe_experts",
    )(block_expert, n_used, slot_token, x_all, wg, wu, wd)


def _final_kernel(dest_ref, y_hbm, h_ref, rw_ref, g2_ref, lng_ref, lnb_ref, o_ref, ybuf, sem, *, tm, per_seq):
    step = pl.program_id(0) * per_seq + pl.program_id(1)
    n_steps = pl.num_programs(0) * per_seq

    def start_gather(blk, slot):
        def start(r, carry):
            for k in range(2):
                pltpu.make_async_copy(y_hbm.at[pl.ds(dest_ref[2 * (blk * tm + r) + k], 1)],
                                      ybuf.at[slot, k, pl.ds(r, 1)], sem.at[slot]).start()
            return carry

        lax.fori_loop(0, tm, start, 0, unroll=4)

    @pl.when(step == 0)
    def _():
        start_gather(0, 0)

    @pl.when(step + 1 < n_steps)
    def _():
        start_gather(step + 1, (step + 1) % 2)

    slot = step % 2
    for k in range(2):
        pltpu.make_async_copy(y_hbm.at[pl.ds(0, tm)], ybuf.at[slot, k], sem.at[slot]).wait()
    rw = rw_ref[0]
    moe = rw[:, 2:3] * ybuf[slot, 0] + rw[:, 3:4] * ybuf[slot, 1]
    xin = ALPHA * h_ref[0] + g2_ref[0] * moe
    mu = jnp.mean(xin, axis=-1, keepdims=True)
    dlt = xin - mu
    var = jnp.mean(dlt * dlt, axis=-1, keepdims=True)
    o_ref[0] = dlt * lax.rsqrt(var + LN_EPS) * lng_ref[...] + lnb_ref[...]


def _final(dest, y_sorted, h, route, g2, ln2_g, ln2_b):
    b, t, d = h.shape
    tm = min(256, t)
    per_seq = t // tm
    grid_spec = pltpu.PrefetchScalarGridSpec(
        num_scalar_prefetch=1,
        grid=(b, per_seq),
        in_specs=[pl.BlockSpec(memory_space=pl.ANY),
                  pl.BlockSpec((1, tm, d), lambda bi, i, de: (bi, i, 0)),
                  pl.BlockSpec((1, tm, LANES), lambda bi, i, de: (bi, i, 0)),
                  pl.BlockSpec((1, 1, d), lambda bi, i, de: (bi, 0, 0)),
                  pl.BlockSpec((1, d), lambda bi, i, de: (0, 0)),
                  pl.BlockSpec((1, d), lambda bi, i, de: (0, 0))],
        out_specs=pl.BlockSpec((1, tm, d), lambda bi, i, de: (bi, i, 0)),
        scratch_shapes=[pltpu.VMEM((2, 2, tm, d), F32), pltpu.SemaphoreType.DMA((2,))],
    )
    return pl.pallas_call(
        functools.partial(_final_kernel, tm=tm, per_seq=per_seq),
        grid_spec=grid_spec,
        out_shape=jax.ShapeDtypeStruct((b, t, d), F32),
        compiler_params=_cparams("arbitrary", "arbitrary"),
        name="combine_ln2",
    )(dest, y_sorted, h, route, g2, ln2_g, ln2_b)


def _pad_rows(a, rows):
    return jnp.pad(a, ((0, rows - a.shape[0]), (0, 0)))


def _mixers(x, ada, past_k, past_v, s0, shift0, wts, sb_tiles, cnt0):
    b, t, d = x.shape
    sh1, sc1, g1, sh2, sc2, g2 = [a[:, None, :] for a in jnp.split(ada, 6, axis=-1)]
    q = _modmm(x, sc1, sh1, wts["w_q"], 1024, name="proj_q")
    k = _modmm(x, sc1, sh1, wts["w_k"], 1024, name="proj_k")
    v = _modmm(x, sc1, sh1, wts["w_v"], 1024, name="proj_v")
    m = _modmm(x, sc1, sh1, wts["w_m"], RW_PAD // 3, name="proj_rwkv")
    sg = _modmm(x, sc1, sh1, wts["w_g"], 1024, out_dtype=BF16, sigmoid_out=True, name="proj_gates")

    tq, tk, nl = sb_tiles
    past = past_k.shape[1]
    s_pad = -(-(past + t) // tk) * tk
    tail = jnp.zeros((b, s_pad - past - t, WIDTH), F32)
    with_history = lambda old, new: jnp.concatenate(
        [p for p in (old.reshape(b, past, WIDTH), new, tail) if p.shape[1] > 0], axis=1)
    k_all = with_history(past_k, k)
    v_all = with_history(past_v, v)
    o_a = _sb_attention(q, k_all, v_all, past, tq, tk, nl)

    shift0_p = jnp.pad(shift0, ((0, 0), (0, 0), (0, RW_PAD - RW_COLS)))
    prep = _rw_prep(m, shift0_p, wts["rw"])
    nhp = WIDTH // LANES
    s0_pairs = s0.reshape(b, nhp, 2, HEAD_DIM, HEAD_DIM)
    s0_bd = jnp.zeros((b, nhp, LANES, LANES), F32)
    s0_bd = s0_bd.at[:, :, :HEAD_DIM, :HEAD_DIM].set(s0_pairs[:, :, 0]).at[:, :, HEAD_DIM:, HEAD_DIM:].set(s0_pairs[:, :, 1])
    o_b, s_bd = _rw_scan(prep, s0_bd, wts["rw"]["lnx_w"], wts["rw"]["lnx_b"])
    s_new = jnp.stack([s_bd[:, :, :HEAD_DIM, :HEAD_DIM], s_bd[:, :, HEAD_DIM:, HEAD_DIM:]], axis=2)
    s_new = s_new.reshape(b, 2 * nhp, HEAD_DIM, HEAD_DIM)
    shift_new = m[:, t - 1:t, :RW_COLS]

    h, u2, route, cnt = _merge(o_a, o_b, sg, x, g1, sc2, sh2, wts, cnt0)
    return h, u2, route, cnt, g2, k, v, s_new, shift_new


def kernel(x_prompt, x_sample, cache_sb_k, cache_sb_v, state_rwkv, state_rwkv_shift, c_prompt, c_sample, w_ada, b_ada, w_in, tokshift_mu, rw_w0, rw_w2, rw_a0, rw_a2, rw_g2, rw_k_k, rw_k_a, rw_r_k, rw_lnx_w, rw_lnx_b, w_branch_a, w_branch_b, w_out, ln1_g, ln1_b, w_router_group, b_router_group, w_router_expert, b_router_expert, w_exp_gate, w_exp_up, w_exp_down, ln2_g, ln2_b):
    depth = w_ada.shape[0]
    assert depth == 1
    bp, tp, d = x_prompt.shape
    bs, ts, _ = x_sample.shape
    l = 0

    w_in_l = w_in[l]
    w_rw = w_in_l[:, 3 * WIDTH:3 * WIDTH + RW_COLS]
    row_vec = lambda a: a.reshape(1, -1)
    pad_cols = lambda a, n: jnp.pad(a, ((0, 0), (0, n - a.shape[1])))
    n_lg = RW_PAD - 3 * WIDTH - LANES
    wts = {
        "w_q": w_in_l[:, 0:WIDTH].astype(BF16),
        "w_k": w_in_l[:, WIDTH:2 * WIDTH].astype(BF16),
        "w_v": w_in_l[:, 2 * WIDTH:3 * WIDTH].astype(BF16),
        "w_m": pad_cols(w_rw, RW_PAD).astype(BF16),
        "w_g": w_in_l[:, 3 * WIDTH + RW_COLS:].astype(BF16),
        "p_a": w_branch_a[l].astype(BF16),
        "p_b": w_branch_b[l].astype(BF16),
        "w_out": w_out[l].astype(BF16),
        "ln1_g": row_vec(ln1_g[l]),
        "ln1_b": row_vec(ln1_b[l]),
        "w_route": pad_cols(jnp.concatenate([w_router_group[l], w_router_expert[l]], axis=1), LANES),
        "b_route": pad_cols(row_vec(jnp.concatenate([b_router_group[l], b_router_expert[l]])), LANES),
        "rw": {
            "mu": pad_cols(row_vec(tokshift_mu[l]), RW_PAD),
            "w0": row_vec(rw_w0[l]),
            "w2": _pad_rows(rw_w2[l], LANES),
            "a0": row_vec(rw_a0[l]),
            "a2": jnp.concatenate([jnp.zeros((LORA_W, WIDTH), F32), rw_a2[l]], axis=0),
            "g2": _pad_rows(rw_g2[l], n_lg),
            "k_k": row_vec(rw_k_k[l]),
            "k_a": row_vec(rw_k_a[l]),
            "r_k": row_vec(rw_r_k[l]),
            "lnx_w": row_vec(rw_lnx_w[l]),
            "lnx_b": row_vec(rw_lnx_b[l]),
        },
    }

    rows = -(-(bp + bs) // 8) * 8
    c_all = _pad_rows(jnp.concatenate([c_prompt, c_sample], axis=0), rows)
    ada = _ada(c_all, w_ada[l], b_ada[l])

    empty = jnp.zeros((bp, 0, WIDTH), F32)
    zero_state = jnp.zeros((bp, WIDTH // HEAD_DIM, HEAD_DIM, HEAD_DIM), F32)
    zero_shift = jnp.zeros((bp, 1, RW_COLS), F32)
    hp, u2p, route_p, cnt_p, g2p, kp, vp, sp_, shp = _mixers(
        x_prompt, ada[:bp], empty, empty, zero_state, zero_shift, wts, (min(256, tp), 256, 4),
        jnp.zeros((1, LANES), F32))
    hs, u2s, route_s, cnt_all, g2s, ks, vs, ss_, shs = _mixers(
        x_sample, ada[bp:bp + bs], cache_sb_k[l], cache_sb_v[l], state_rwkv[l], state_rwkv_shift[l], wts,
        (min(256, ts), 256, 4), cnt_p)

    n_p, n_s = bp * tp, bs * ts
    n_tok = n_p + n_s
    tb = 512
    route_all = jnp.concatenate([route_p.reshape(n_p, LANES), route_s.reshape(n_s, LANES)], axis=0)
    flat_e = route_all[:, 0:2].astype(I32).reshape(n_tok * 2)
    rank = route_all[:, 4:6].astype(I32).reshape(n_tok * 2)
    counts = cnt_all[0, N_GROUPS:N_GROUPS + N_EXPERTS].astype(I32)
    padded = (counts + tb - 1) // tb * tb
    pend = jnp.cumsum(padded)
    pstart = pend - padded
    dest = (pstart[flat_e] + rank).astype(I32)
    n_blocks = (2 * n_tok + N_EXPERTS * (tb - 1)) // tb + 1
    slot_token = jnp.zeros((n_blocks * tb,), I32).at[dest].set(
        jnp.arange(2 * n_tok, dtype=I32) // 2, unique_indices=True, mode="promise_in_bounds")
    block_start = jnp.arange(n_blocks, dtype=I32) * tb
    block_expert = jnp.minimum(jnp.searchsorted(pend, block_start, side="right"), N_EXPERTS - 1).astype(I32)
    n_used = (pend[-1] // tb).astype(I32).reshape(1)

    u2_all = jnp.concatenate([u2p.reshape(n_p, d), u2s.reshape(n_s, d)], axis=0)
    y_sorted = _moe(block_expert, n_used, slot_token, u2_all, w_exp_gate[l].astype(BF16),
                    w_exp_up[l].astype(BF16), w_exp_down[l].astype(BF16), tb)

    ln2g, ln2b = row_vec(ln2_g[l]), row_vec(ln2_b[l])
    y_prompt = _final(dest[:2 * n_p], y_sorted, hp, route_p, g2p, ln2g, ln2b)
    y_sample = _final(dest[2 * n_p:], y_sorted, hs, route_s, g2s, ln2g, ln2b)

    heads = lambda a: a.reshape(1, a.shape[0], a.shape[1], WIDTH // HEAD_DIM, HEAD_DIM)
    return (y_prompt, y_sample, heads(kp), heads(vp), sp_[None], shp[None],
            heads(ks), heads(vs), ss_[None], shs[None])
```

```python
import functools

import jax
import jax.numpy as jnp
import numpy as np
from jax import lax
from jax.experimental import pallas as pl
from jax.experimental.pallas import tpu as pltpu

F32 = jnp.float32
BF16 = jnp.bfloat16
I32 = jnp.int32

D_MODEL = 2048
HEAD_DIM = 64
WIDTH = 1024
LORA_W = 64
LORA_A = 64
LORA_G = 160
RW_COLS = 3 * WIDTH + LORA_W + LORA_A + LORA_G
RW_PAD = 3456
N_GROUPS = 4
PER_GROUP = 8
N_EXPERTS = 32
D_EXPERT = 1024
LN_EPS = 1e-5
GN_EPS = 64e-5
ALPHA = 2.0 ** 0.25
LOG2E = 1.4426950408889634
F32_UNDERFLOW_LOG2 = -150.0
LANES = 128
CHUNK = 64
VMEM_LIMIT = 56 * 1024 * 1024

NN = (((1,), (0,)), ((), ()))
NT = (((1,), (1,)), ((), ()))
TN = (((0,), (0,)), ((), ()))


def _cparams(*sem):
    return pltpu.CompilerParams(dimension_semantics=sem, vmem_limit_bytes=VMEM_LIMIT)


def _mm(a, b, dims=NN):
    return lax.dot_general(a, b, dims, preferred_element_type=F32)


def _split(x):
    hi = x.astype(BF16)
    lo = (x - hi.astype(F32)).astype(BF16)
    return hi, lo


def _split3(x):
    hi = x.astype(BF16)
    r1 = x - hi.astype(F32)
    mid = r1.astype(BF16)
    lo = (r1 - mid.astype(F32)).astype(BF16)
    return hi, mid, lo


def _dot3(a, b, dims=NN):
    ah, al = _split(a)
    bh, bl = _split(b)
    return _mm(ah, bh, dims) + (_mm(al, bh, dims) + _mm(ah, bl, dims))


def _dot1(a, b, dims=NN):
    return _mm(a.astype(BF16), b.astype(BF16), dims)


def _dot_exact_rhs(a, b_bf16, terms=2):
    parts = _split(a) if terms == 2 else _split3(a)
    out = _mm(parts[0], b_bf16)
    for p in parts[1:]:
        out = out + _mm(p, b_bf16)
    return out


def _softplus(z):
    return jnp.maximum(z, 0.0) + jnp.log(1.0 + jnp.exp(-jnp.abs(z)))


def _sigmoid(z):
    return 1.0 / (1.0 + jnp.exp(-z))


def _ada_kernel(c_ref, w_ref, b_ref, o_ref):
    c = c_ref[...]
    o_ref[...] = _dot3(c * _sigmoid(c), w_ref[...]) + b_ref[...]


def _ada(c_all, w_ada, b_ada):
    rows = c_all.shape[0]
    n = w_ada.shape[1]
    tn = 1024
    return pl.pallas_call(
        _ada_kernel,
        grid=(n // tn,),
        in_specs=[pl.BlockSpec((rows, D_MODEL), lambda j: (0, 0)),
                  pl.BlockSpec((D_MODEL, tn), lambda j: (0, j)),
                  pl.BlockSpec((1, tn), lambda j: (0, j))],
        out_specs=pl.BlockSpec((rows, tn), lambda j: (0, j)),
        out_shape=jax.ShapeDtypeStruct((rows, n), F32),
        compiler_params=_cparams("arbitrary"),
        name="ada",
    )(c_all, w_ada, b_ada.reshape(1, n))


def _modmm_kernel(x_ref, sc_ref, sh_ref, w_ref, o_ref, u_ref, *, sigmoid_out):
    @pl.when(pl.program_id(2) == 0)
    def _():
        u_ref[...] = (x_ref[0] * (1.0 + sc_ref[0]) + sh_ref[0]).astype(BF16)

    y = _mm(u_ref[...], w_ref[...])
    if sigmoid_out:
        y = _sigmoid(y)
    o_ref[0] = y.astype(o_ref.dtype)


def _modmm(x, sc, sh, w_bf16, tn, out_dtype=F32, sigmoid_out=False, name="proj"):
    b, t, d = x.shape
    n = w_bf16.shape[1]
    tm = min(1024, t)
    return pl.pallas_call(
        functools.partial(_modmm_kernel, sigmoid_out=sigmoid_out),
        grid=(b, t // tm, n // tn),
        in_specs=[pl.BlockSpec((1, tm, d), lambda bi, i, j: (bi, i, 0)),
                  pl.BlockSpec((1, 1, d), lambda bi, i, j: (bi, 0, 0)),
                  pl.BlockSpec((1, 1, d), lambda bi, i, j: (bi, 0, 0)),
                  pl.BlockSpec((d, tn), lambda bi, i, j: (0, j))],
        out_specs=pl.BlockSpec((1, tm, tn), lambda bi, i, j: (bi, i, j)),
        out_shape=jax.ShapeDtypeStruct((b, t, n), out_dtype),
        scratch_shapes=[pltpu.VMEM((tm, d), BF16)],
        compiler_params=_cparams("arbitrary", "arbitrary", "arbitrary"),
        name=name,
    )(x, sc, sh, w_bf16)


def _sb_kernel(q_ref, k_hbm, v_hbm, o_ref, kbuf, vbuf, sem, done_ref, *scratch, tq, tk, past, nl):
    accs, cars = scratch[:2 * nl], scratch[2 * nl:]
    bi, g, qi = pl.program_id(0), pl.program_id(1), pl.program_id(2)
    width = nl * LANES
    top = (past + (qi + 1) * tq - 2) // tk

    def kv_copies(kb, slot):
        src = (bi, pl.ds(kb * tk, tk), pl.ds(g * width, width))
        return (pltpu.make_async_copy(k_hbm.at[src], kbuf.at[slot], sem.at[0, slot]),
                pltpu.make_async_copy(v_hbm.at[src], vbuf.at[slot], sem.at[1, slot]))

    for cp in kv_copies(top, 0):
        cp.start()
    for ref in scratch:
        ref[...] = jnp.zeros_like(ref)
    done_ref[0] = 0

    later = (lax.broadcasted_iota(I32, (tk, tk), 0) > lax.broadcasted_iota(I32, (tk, tk), 1)).astype(BF16)
    lane = lax.broadcasted_iota(I32, (tq, LANES), 1)

    def sweep(kb, slot, masked):
        if masked:
            pos = past + qi * tq + lax.broadcasted_iota(I32, (tq, tk), 0)
            kpos = kb * tk + lax.broadcasted_iota(I32, (tq, tk), 1)
            mask = kpos < pos
        for l in range(nl):
            cols = slice(l * LANES, (l + 1) * LANES)
            q = q_ref[0, :, cols] * (HEAD_DIM ** -0.5 * LOG2E)
            kbf = kbuf[slot, :, cols].astype(BF16)
            vbf = vbuf[slot, :, cols].astype(BF16)
            for hh in range(2):
                head = (lane < HEAD_DIM) if hh == 0 else (lane >= HEAD_DIM)
                z = _mm(jnp.where(head, q, 0.0).astype(BF16), kbf, NT)
                neg_abs = lax.bitcast_convert_type(lax.bitcast_convert_type(z, jnp.uint32) | jnp.uint32(0x80000000), F32)
                soft = jnp.log2(1.0 + jnp.exp2(neg_abs))
                log_beta = jnp.minimum(z, 0.0) - soft
                log_keep = log_beta - z
                if masked:
                    log_keep = jnp.where(mask, log_keep, 0.0)
                car = cars[2 * l + hh]
                after = _mm(log_keep.astype(BF16), later) + car[...]
                w = jnp.exp2(log_beta + after)
                if masked:
                    w = jnp.where(mask, w, 0.0)
                accs[2 * l + hh][...] += _mm(w.astype(BF16), vbf)
                car[...] += jnp.sum(log_keep, axis=1, keepdims=True)
        worst = cars[0][...]
        for car in cars[1:]:
            worst = jnp.maximum(worst, car[...])
        done_ref[0] = (jnp.max(worst) < F32_UNDERFLOW_LOG2).astype(I32)

    def body(kb):
        slot = (top - kb) % 2
        for cp in kv_copies(kb, slot):
            cp.wait()

        @pl.when(kb > 0)
        def _():
            for cp in kv_copies(kb - 1, 1 - slot):
                cp.start()

        diag = (kb + 1) * tk - 1 >= past + qi * tq

        @pl.when(diag)
        def _():
            sweep(kb, slot, True)

        @pl.when(jnp.logical_not(diag))
        def _():
            sweep(kb, slot, False)

        return kb - 1

    kb_end = lax.while_loop(lambda kb: (kb >= 0) & (done_ref[0] == 0), body, top)

    @pl.when(kb_end >= 0)
    def _():
        for cp in kv_copies(kb_end, (top - kb_end) % 2):
            cp.wait()

    for l in range(nl):
        o_ref[0, :, l * LANES:(l + 1) * LANES] = jnp.where(lane < HEAD_DIM, accs[2 * l][...], accs[2 * l + 1][...])


def _sb_attention(q, k_all, v_all, past, tq, tk, nl):
    b, t, _ = q.shape
    width = nl * LANES
    q_spec = pl.BlockSpec((1, tq, width), lambda bi, g, qi: (bi, qi, g))
    return pl.pallas_call(
        functools.partial(_sb_kernel, tq=tq, tk=tk, past=past, nl=nl),
        grid=(b, WIDTH // width, t // tq),
        in_specs=[q_spec, pl.BlockSpec(memory_space=pl.ANY), pl.BlockSpec(memory_space=pl.ANY)],
        out_specs=q_spec,
        out_shape=jax.ShapeDtypeStruct((b, t, WIDTH), F32),
        scratch_shapes=[pltpu.VMEM((2, tk, width), F32), pltpu.VMEM((2, tk, width), F32),
                        pltpu.SemaphoreType.DMA((2, 2)), pltpu.SMEM((1,), I32)]
        + [pltpu.VMEM((tq, LANES), F32)] * (2 * nl) + [pltpu.VMEM((tq, 1), F32)] * (2 * nl),
        compiler_params=_cparams("arbitrary", "arbitrary", "arbitrary"),
        name="sb_attention",
    )(q, k_all, v_all)


def _head_sum(x, same_head):
    return jnp.concatenate(
        [_dot1(x[:, c * LANES:(c + 1) * LANES], same_head) for c in range(WIDTH // LANES)], axis=1)


def _rw_prep_kernel(m_ref, sh0_ref, mu_ref, w0_ref, w2_ref, a0_ref, a2_ref, g2_ref, kk_ref, ka_ref, rk_ref,
                    wt_ref, ut_ref, rt_ref, mrb_ref, yv_ref, bt_ref, kt_ref, v_ref, bv_ref, g_ref, pend_ref, prev_ref,
                    *, tr):
    @pl.when(pl.program_id(1) == 0)
    def _():
        prev_ref[...] = sh0_ref[0]

    m = m_ref[0]
    row = lax.broadcasted_iota(I32, m.shape, 0)
    m_prev = jnp.where(row == 0, prev_ref[...], pltpu.roll(m, 1, 0))
    prev_ref[...] = m[tr - 1:tr, :]
    ms = m + (m_prev - m) * mu_ref[...]
    r = ms[:, 0:WIDTH]
    k = ms[:, WIDTH:2 * WIDTH]
    v = ms[:, 2 * WIDTH:3 * WIDTH]
    lora_wa = ms[:, 3 * WIDTH:3 * WIDTH + LANES]
    lora_g = ms[:, 3 * WIDTH + LANES:RW_PAD]

    w_log = -_softplus(-(w0_ref[...] + _dot1(jnp.tanh(lora_wa), w2_ref[...]))) - 0.5
    log_decay = -jnp.exp(w_log)
    a = _sigmoid(a0_ref[...] + _dot1(lora_wa, a2_ref[...]))
    g = _dot1(_sigmoid(lora_g), g2_ref[...])

    same_head = (lax.broadcasted_iota(I32, (LANES, LANES), 0) // HEAD_DIM ==
                 lax.broadcasted_iota(I32, (LANES, LANES), 1) // HEAD_DIM).astype(BF16)
    kk = k * kk_ref[...]
    kk = kk / jnp.maximum(jnp.sqrt(_head_sum(kk * kk, same_head)), 1e-12)
    kh = k * (1.0 + (a - 1.0) * ka_ref[...])
    bonus = _head_sum(r * kh * rk_ref[...], same_head)

    c = CHUNK
    ti = lax.broadcasted_iota(I32, (tr, tr), 0)
    si = lax.broadcasted_iota(I32, (tr, tr), 1)
    tri = ((si <= ti) & (si // c == ti // c)).astype(BF16)
    parts = _split3(log_decay)
    cl = _mm(tri, parts[0]) + (_mm(tri, parts[1]) + _mm(tri, parts[2]))
    ec = jnp.exp(cl)
    inv = jnp.exp(-cl)
    rt = r * ec
    kt = kh * inv
    at = -kk * jnp.exp(cl - log_decay)
    bt = kk * a * inv
    rt_ref[0] = rt.astype(BF16)
    kt_ref[0] = kt.astype(BF16)
    bt_ref[0] = bt.astype(BF16)
    v_ref[0] = v.astype(BF16)
    bv_ref[0] = bonus * v
    g_ref[0] = g
    for ch in range(tr // c):
        pend_ref[0, ch] = ec[(ch + 1) * c - 1:(ch + 1) * c, :]

    lane = lax.broadcasted_iota(I32, (c, LANES), 1)
    first = lane < HEAD_DIM
    t_row = lax.broadcasted_iota(I32, (c, LANES), 0)
    s_col = lane % HEAD_DIM
    strict = s_col < t_row
    incl = s_col <= t_row
    eye2 = (s_col == t_row).astype(F32)

    def stack2(x):
        return jnp.concatenate([jnp.where(first, x, 0.0), jnp.where(first, 0.0, x)], axis=0)

    tiles = [(slice(ch * c, (ch + 1) * c), slice(hp * LANES, (hp + 1) * LANES))
             for ch in range(tr // c) for hp in range(WIDTH // LANES)]
    rows2 = lambda top, bottom: jnp.concatenate([top, bottom], axis=0)
    a_t = [at[tl] for tl in tiles]
    l_ab, g_v = [], []
    for i, tl in enumerate(tiles):
        b_s, k_s, v_s = stack2(bt[tl]), stack2(kt[tl]), stack2(v[tl])
        ar = rows2(a_t[i], rt[tl])
        on_b = _dot1(ar, b_s, NT)
        on_k = _dot1(ar, k_s, NT)
        l_ab.append(jnp.where(strict, on_b[:c], 0.0))
        mrb_ref[(0,) + tl] = jnp.where(incl, on_b[c:], 0.0).astype(BF16)
        on_v = _dot1(rows2(jnp.where(strict, on_k[:c], 0.0), jnp.where(incl, on_k[c:], 0.0)), v_s)
        g_v.append(on_v[:c])
        yv_ref[(0,) + tl] = on_v[c:]
    tinv = [eye2 + l for l in l_ab]
    pw = [_dot1(x, stack2(x)) for x in l_ab]
    for _ in range(4):
        both = [_dot1(rows2(p_, t_), stack2(p_)) for p_, t_ in zip(pw, tinv)]
        tinv = [t_ + r_[c:] for t_, r_ in zip(tinv, both)]
        pw = [r_[:c] for r_ in both]
    tinv = [t_ + _dot1(t_, stack2(p_)) for t_, p_ in zip(tinv, pw)]
    for i, tl in enumerate(tiles):
        wu = _dot1(tinv[i], jnp.concatenate([stack2(a_t[i]), stack2(g_v[i])], axis=1))
        wt_ref[(0,) + tl] = wu[:, :LANES].astype(BF16)
        ut_ref[(0,) + tl] = wu[:, LANES:]


def _rw_prep(m, shift0, rwp):
    b, t, _ = m.shape
    tr = min(2 * CHUNK, t)
    row_spec = pl.BlockSpec((1, tr, WIDTH), lambda bi, i: (bi, i, 0))
    vec = lambda n: pl.BlockSpec((1, n), lambda bi, i: (0, 0))
    mat = lambda r: pl.BlockSpec((r, WIDTH), lambda bi, i: (0, 0))
    out = jax.ShapeDtypeStruct((b, t, WIDTH), F32)
    out_mxu = jax.ShapeDtypeStruct((b, t, WIDTH), BF16)
    return pl.pallas_call(
        functools.partial(_rw_prep_kernel, tr=tr),
        grid=(b, t // tr),
        in_specs=[pl.BlockSpec((1, tr, RW_PAD), lambda bi, i: (bi, i, 0)),
                  pl.BlockSpec((1, 1, RW_PAD), lambda bi, i: (bi, 0, 0)),
                  vec(RW_PAD), vec(WIDTH), mat(LANES), vec(WIDTH), mat(LANES), mat(RW_PAD - 3 * WIDTH - LANES),
                  vec(WIDTH), vec(WIDTH), vec(WIDTH)],
        out_specs=[row_spec] * 10 + [pl.BlockSpec((1, tr // CHUNK, 1, WIDTH), lambda bi, i: (bi, i, 0, 0))],
        out_shape=[out_mxu, out, out_mxu, out_mxu, out, out_mxu, out_mxu, out_mxu, out, out,
                   jax.ShapeDtypeStruct((b, t // CHUNK, 1, WIDTH), F32)],
        scratch_shapes=[pltpu.VMEM((1, RW_PAD), F32)],
        compiler_params=_cparams("arbitrary", "arbitrary"),
        name="rwkv_prep",
    )(m, shift0, rwp["mu"], rwp["w0"], rwp["w2"], rwp["a0"], rwp["a2"], rwp["g2"], rwp["k_k"], rwp["k_a"],
      rwp["r_k"])


def _rw_scan_kernel(wt_ref, ut_ref, rt_ref, mrb_ref, yv_ref, bt_ref, kt_ref, v_ref, bv_ref, g_ref, pend_ref,
                    s0_ref, lnw_ref, lnb_ref, o_ref, sout_ref, *s_refs, n_chunks, npair):
    ci = pl.program_id(2)

    @pl.when(ci == 0)
    def _():
        for pi in range(npair):
            s_refs[pi][...] = s0_ref[0, pi]

    c = CHUNK
    first = lax.broadcasted_iota(I32, (c, LANES), 1) < HEAD_DIM
    vi = lax.broadcasted_iota(I32, (LANES, LANES), 0)
    kj = lax.broadcasted_iota(I32, (LANES, LANES), 1)
    same_head_b = (vi // HEAD_DIM) == (kj // HEAD_DIM)
    same_head = same_head_b.astype(BF16)

    def stack2(x):
        return jnp.concatenate([jnp.where(first, x, 0.0), jnp.where(first, 0.0, x)], axis=0)

    def chunk(j, carry):
        rows = pl.ds(pl.multiple_of(j * c, c), c)
        p_all = pend_ref[0, j]
        pairs = range(npair)
        col = lambda pi: slice(pi * LANES, (pi + 1) * LANES)
        s = [s_refs[pi][...] for pi in pairs]
        u = [_dot1(wt_ref[0, rows, col(pi)], s[pi], NT) + ut_ref[0, rows, col(pi)] for pi in pairs]
        for pi in pairs:
            ds = _dot1(jnp.concatenate([u[pi].astype(BF16), v_ref[0, rows, col(pi)]], axis=0),
                       jnp.concatenate([bt_ref[0, rows, col(pi)], kt_ref[0, rows, col(pi)]], axis=0), TN)
            s_refs[pi][...] = (s[pi] + jnp.where(same_head_b, ds, 0.0)) * p_all[:, col(pi)]
        y = [_dot1(rt_ref[0, rows, col(pi)], s[pi], NT) + (_dot1(mrb_ref[0, rows, col(pi)], stack2(u[pi]))
                                                           + yv_ref[0, rows, col(pi)]) for pi in pairs]
        for pi in pairs:
            mu = _dot1(y[pi], same_head) * (1.0 / HEAD_DIM)
            dlt = y[pi] - mu
            var = _dot1(dlt * dlt, same_head) * (1.0 / HEAD_DIM)
            yn = dlt * lax.rsqrt(var + GN_EPS) * lnw_ref[:, col(pi)] + lnb_ref[:, col(pi)]
            o_ref[0, rows, col(pi)] = (yn + bv_ref[0, rows, col(pi)]) * g_ref[0, rows, col(pi)]
        return carry

    lax.fori_loop(0, n_chunks, chunk, 0)

    @pl.when(ci == pl.num_programs(2) - 1)
    def _():
        for pi in range(npair):
            sout_ref[0, pi] = s_refs[pi][...]


def _rw_scan(prep, s0_bd, lnw, lnb, npair=8):
    b, t, _ = prep[0].shape
    tc = min(512, t)
    n_chunks = tc // CHUNK
    ng = WIDTH // (npair * LANES)
    row_spec = pl.BlockSpec((1, tc, npair * LANES), lambda bi, h, i: (bi, i, h))
    pend_spec = pl.BlockSpec((1, n_chunks, 1, npair * LANES), lambda bi, h, i: (bi, i, 0, h))
    st_spec = pl.BlockSpec((1, npair, LANES, LANES), lambda bi, h, i: (bi, h, 0, 0))
    vec_spec = pl.BlockSpec((1, npair * LANES), lambda bi, h, i: (0, h))
    return pl.pallas_call(
        functools.partial(_rw_scan_kernel, n_chunks=n_chunks, npair=npair),
        grid=(b, ng, t // tc),
        in_specs=[row_spec] * 10 + [pend_spec, st_spec, vec_spec, vec_spec],
        out_specs=[row_spec, st_spec],
        out_shape=[jax.ShapeDtypeStruct((b, t, WIDTH), F32),
                   jax.ShapeDtypeStruct((b, WIDTH // LANES, LANES, LANES), F32)],
        scratch_shapes=[pltpu.VMEM((LANES, LANES), F32)] * npair,
        compiler_params=_cparams("arbitrary", "arbitrary", "arbitrary"),
        name="rwkv_scan",
    )(*prep, s0_bd, lnw, lnb)


def _merge_kernel(oa_ref, ob_ref, sg_ref, x_ref, g1_ref, sc2_ref, sh2_ref, pa_ref, pb_ref, wo_ref, lng_ref, lnb_ref,
                  wr_ref, br_ref, cnt0_ref, h_ref, u2_ref, route_ref, cnt_out_ref, cnt_ref):
    sg = sg_ref[0]
    ya = _mm(oa_ref[0].astype(BF16), pa_ref[...])
    yb = _mm(ob_ref[0].astype(BF16), pb_ref[...])
    merged = sg[:, :D_MODEL].astype(F32) * ya + sg[:, D_MODEL:].astype(F32) * yb
    mix = _mm(merged.astype(BF16), wo_ref[...])
    hin = ALPHA * x_ref[0] + g1_ref[0] * mix
    mu = jnp.mean(hin, axis=-1, keepdims=True)
    dlt = hin - mu
    var = jnp.mean(dlt * dlt, axis=-1, keepdims=True)
    h = dlt * lax.rsqrt(var + LN_EPS) * lng_ref[...] + lnb_ref[...]
    h_ref[0] = h
    u2 = h * (1.0 + sc2_ref[0]) + sh2_ref[0]
    u2_ref[0] = u2

    logits = _dot3(u2, wr_ref[...]) + br_ref[...]
    lane_i = lax.broadcasted_iota(I32, logits.shape, 1)
    lane = lane_i.astype(F32)
    neg = -jnp.inf
    first_at = lambda vals, top: jnp.min(jnp.where(vals == top, lane, float(LANES)), axis=1, keepdims=True)
    lg = jnp.where(lane_i < N_GROUPS, logits, neg)
    mg = jnp.max(lg, axis=1, keepdims=True)
    g_idx = first_at(lg, mg)
    p_g = 1.0 / jnp.sum(jnp.exp(lg - mg), axis=1, keepdims=True)
    lo_lane = N_GROUPS + PER_GROUP * g_idx
    le = jnp.where((lane >= lo_lane) & (lane < lo_lane + PER_GROUP), logits, neg)
    m1 = jnp.max(le, axis=1, keepdims=True)
    i1 = first_at(le, m1)
    le2 = jnp.where(lane == i1, neg, le)
    m2 = jnp.max(le2, axis=1, keepdims=True)
    i2 = first_at(le2, m2)
    ratio = jnp.exp(m2 - m1)
    w1 = p_g / (1.0 + ratio)
    w2 = p_g * ratio / (1.0 + ratio)
    @pl.when((pl.program_id(0) == 0) & (pl.program_id(1) == 0))
    def _():
        cnt_ref[...] = cnt0_ref[...]

    tm = logits.shape[0]
    chosen = ((lane == i1) | (lane == i2)).astype(BF16)
    earlier = (lax.broadcasted_iota(I32, (tm, tm), 1) < lax.broadcasted_iota(I32, (tm, tm), 0)).astype(BF16)
    before = _mm(earlier, chosen) + cnt_ref[...]
    rank1 = jnp.sum(jnp.where(lane == i1, before, 0.0), axis=1, keepdims=True)
    rank2 = jnp.sum(jnp.where(lane == i2, before, 0.0), axis=1, keepdims=True)
    cnt_ref[...] = before[tm - 1:tm, :] + chosen[tm - 1:tm, :].astype(F32)
    cnt_out_ref[...] = cnt_ref[...]

    route = jnp.where(lane_i == 0, i1 - N_GROUPS, 0.0)
    for k, val in enumerate((i2 - N_GROUPS, w1, w2, rank1, rank2), start=1):
        route = jnp.where(lane_i == k, val, route)
    route_ref[0] = route


def _merge(o_a, o_b, sg, x, g1, sc2, sh2, wts, cnt0):
    b, t, d = x.shape
    tm = min(256, t)
    const = lambda shape: pl.BlockSpec(shape, lambda bi, i: (0,) * len(shape), pipeline_mode=pl.Buffered(1))
    seq = pl.BlockSpec((1, 1, d), lambda bi, i: (bi, 0, 0))
    row = lambda n: pl.BlockSpec((1, tm, n), lambda bi, i: (bi, i, 0))
    return pl.pallas_call(
        _merge_kernel,
        grid=(b, t // tm),
        in_specs=[row(WIDTH), row(WIDTH), row(2 * d), row(d), seq, seq, seq,
                  const((WIDTH, d)), const((WIDTH, d)), const((d, d)), const((1, d)), const((1, d)),
                  const((d, LANES)), const((1, LANES)), const((1, LANES))],
        out_specs=[row(d), row(d), row(LANES), pl.BlockSpec((1, LANES), lambda bi, i: (0, 0))],
        out_shape=[jax.ShapeDtypeStruct((b, t, d), F32), jax.ShapeDtypeStruct((b, t, d), F32),
                   jax.ShapeDtypeStruct((b, t, LANES), F32), jax.ShapeDtypeStruct((1, LANES), F32)],
        scratch_shapes=[pltpu.VMEM((1, LANES), F32)],
        compiler_params=_cparams("arbitrary", "arbitrary"),
        name="merge_route",
    )(o_a, o_b, sg, x, g1, sc2, sh2, wts["p_a"], wts["p_b"], wts["w_out"], wts["ln1_g"], wts["ln1_b"],
      wts["w_route"], wts["b_route"], cnt0)


def _moe_kernel(be_ref, nused_ref, tok_ref, x_hbm, wg_ref, wu_ref, wd_ref, y_ref, xbuf, sem, *, tb):
    i = pl.program_id(0)
    n_used = nused_ref[0]

    def start_gather(block, slot):
        def start(r, carry):
            pltpu.make_async_copy(x_hbm.at[pl.ds(tok_ref[block * tb + r], 1)], xbuf.at[slot, pl.ds(r, 1)],
                                  sem.at[slot]).start()
            return carry

        lax.fori_loop(0, tb, start, 0, unroll=8)

    @pl.when((i == 0) & (n_used > 0))
    def _():
        start_gather(0, 0)

    @pl.when(i + 1 < n_used)
    def _():
        start_gather(i + 1, (i + 1) % 2)

    @pl.when(i < n_used)
    def _():
        slot = i % 2
        pltpu.make_async_copy(x_hbm.at[pl.ds(0, tb)], xbuf.at[slot], sem.at[slot]).wait()
        x = xbuf[slot].astype(BF16)
        hg = _mm(x, wg_ref[0])
        hu = _mm(x, wu_ref[0])
        hid = (hg * _sigmoid(hg) * hu).astype(BF16)
        y_ref[...] = _mm(hid, wd_ref[0])

    @pl.when(i >= n_used)
    def _():
        y_ref[...] = jnp.zeros_like(y_ref)


def _moe(block_expert, n_used, slot_token, x_all, wg, wu, wd, tb):
    n_blocks = block_expert.shape[0]
    d = x_all.shape[1]
    grid_spec = pltpu.PrefetchScalarGridSpec(
        num_scalar_prefetch=3,
        grid=(n_blocks,),
        in_specs=[pl.BlockSpec(memory_space=pl.ANY),
                  pl.BlockSpec((1, d, D_EXPERT), lambda i, be, nu, tok: (be[i], 0, 0), pipeline_mode=pl.Buffered(1)),
                  pl.BlockSpec((1, d, D_EXPERT), lambda i, be, nu, tok: (be[i], 0, 0), pipeline_mode=pl.Buffered(1)),
                  pl.BlockSpec((1, D_EXPERT, d), lambda i, be, nu, tok: (be[i], 0, 0), pipeline_mode=pl.Buffered(1))],
        out_specs=pl.BlockSpec((tb, d), lambda i, be, nu, tok: (i, 0)),
        scratch_shapes=[pltpu.VMEM((2, tb, d), F32), pltpu.SemaphoreType.DMA((2,))],
    )
    return pl.pallas_call(
        functools.partial(_moe_kernel, tb=tb),
        grid_spec=grid_spec,
        out_shape=jax.ShapeDtypeStruct((n_blocks * tb, d), F32),
        compiler_params=_cparams("arbitrary"),
        name="moe_experts",
    )(block_expert, n_used, slot_token, x_all, wg, wu, wd)


def _final_kernel(dest_ref, y_hbm, h_ref, rw_ref, g2_ref, lng_ref, lnb_ref, o_ref, ybuf, sem, *, tm, per_seq):
    step = pl.program_id(0) * per_seq + pl.program_id(1)
    n_steps = pl.num_programs(0) * per_seq

    def start_gather(blk, slot):
        def start(r, carry):
            for k in range(2):
                pltpu.make_async_copy(y_hbm.at[pl.ds(dest_ref[2 * (blk * tm + r) + k], 1)],
                                      ybuf.at[slot, k, pl.ds(r, 1)], sem.at[slot]).start()
            return carry

        lax.fori_loop(0, tm, start, 0, unroll=4)

    @pl.when(step == 0)
    def _():
        start_gather(0, 0)

    @pl.when(step + 1 < n_steps)
    def _():
        start_gather(step + 1, (step + 1) % 2)

    slot = step % 2
    for k in range(2):
        pltpu.make_async_copy(y_hbm.at[pl.ds(0, tm)], ybuf.at[slot, k], sem.at[slot]).wait()
    rw = rw_ref[0]
    moe = rw[:, 2:3] * ybuf[slot, 0] + rw[:, 3:4] * ybuf[slot, 1]
    xin = ALPHA * h_ref[0] + g2_ref[0] * moe
    mu = jnp.mean(xin, axis=-1, keepdims=True)
    dlt = xin - mu
    var = jnp.mean(dlt * dlt, axis=-1, keepdims=True)
    o_ref[0] = dlt * lax.rsqrt(var + LN_EPS) * lng_ref[...] + lnb_ref[...]


def _final(dest, y_sorted, h, route, g2, ln2_g, ln2_b):
    b, t, d = h.shape
    tm = min(256, t)
    per_seq = t // tm
    grid_spec = pltpu.PrefetchScalarGridSpec(
        num_scalar_prefetch=1,
        grid=(b, per_seq),
        in_specs=[pl.BlockSpec(memory_space=pl.ANY),
                  pl.BlockSpec((1, tm, d), lambda bi, i, de: (bi, i, 0)),
                  pl.BlockSpec((1, tm, LANES), lambda bi, i, de: (bi, i, 0)),
                  pl.BlockSpec((1, 1, d), lambda bi, i, de: (bi, 0, 0)),
                  pl.BlockSpec((1, d), lambda bi, i, de: (0, 0)),
                  pl.BlockSpec((1, d), lambda bi, i, de: (0, 0))],
        out_specs=pl.BlockSpec((1, tm, d), lambda bi, i, de: (bi, i, 0)),
        scratch_shapes=[pltpu.VMEM((2, 2, tm, d), F32), pltpu.SemaphoreType.DMA((2,))],
    )
    return pl.pallas_call(
        functools.partial(_final_kernel, tm=tm, per_seq=per_seq),
        grid_spec=grid_spec,
        out_shape=jax.ShapeDtypeStruct((b, t, d), F32),
        compiler_params=_cparams("arbitrary", "arbitrary"),
        name="combine_ln2",
    )(dest, y_sorted, h, route, g2, ln2_g, ln2_b)


def _pad_rows(a, rows):
    return jnp.pad(a, ((0, rows - a.shape[0]), (0, 0)))


def _mixers(x, ada, past_k, past_v, s0, shift0, wts, sb_tiles, cnt0):
    b, t, d = x.shape
    sh1, sc1, g1, sh2, sc2, g2 = [a[:, None, :] for a in jnp.split(ada, 6, axis=-1)]
    q = _modmm(x, sc1, sh1, wts["w_q"], 1024, name="proj_q")
    k = _modmm(x, sc1, sh1, wts["w_k"], 1024, name="proj_k")
    v = _modmm(x, sc1, sh1, wts["w_v"], 1024, name="proj_v")
    m = _modmm(x, sc1, sh1, wts["w_m"], RW_PAD // 3, name="proj_rwkv")
    sg = _modmm(x, sc1, sh1, wts["w_g"], 1024, out_dtype=BF16, sigmoid_out=True, name="proj_gates")

    tq, tk, nl = sb_tiles
    past = past_k.shape[1]
    s_pad = -(-(past + t) // tk) * tk
    tail = jnp.zeros((b, s_pad - past - t, WIDTH), F32)
    with_history = lambda old, new: jnp.concatenate(
        [p for p in (old.reshape(b, past, WIDTH), new, tail) if p.shape[1] > 0], axis=1)
    k_all = with_history(past_k, k)
    v_all = with_history(past_v, v)
    o_a = _sb_attention(q, k_all, v_all, past, tq, tk, nl)

    shift0_p = jnp.pad(shift0, ((0, 0), (0, 0), (0, RW_PAD - RW_COLS)))
    prep = _rw_prep(m, shift0_p, wts["rw"])
    nhp = WIDTH // LANES
    s0_pairs = s0.reshape(b, nhp, 2, HEAD_DIM, HEAD_DIM)
    s0_bd = jnp.zeros((b, nhp, LANES, LANES), F32)
    s0_bd = s0_bd.at[:, :, :HEAD_DIM, :HEAD_DIM].set(s0_pairs[:, :, 0]).at[:, :, HEAD_DIM:, HEAD_DIM:].set(s0_pairs[:, :, 1])
    o_b, s_bd = _rw_scan(prep, s0_bd, wts["rw"]["lnx_w"], wts["rw"]["lnx_b"])
    s_new = jnp.stack([s_bd[:, :, :HEAD_DIM, :HEAD_DIM], s_bd[:, :, HEAD_DIM:, HEAD_DIM:]], axis=2)
    s_new = s_new.reshape(b, 2 * nhp, HEAD_DIM, HEAD_DIM)
    shift_new = m[:, t - 1:t, :RW_COLS]

    h, u2, route, cnt = _merge(o_a, o_b, sg, x, g1, sc2, sh2, wts, cnt0)
    return h, u2, route, cnt, g2, k, v, s_new, shift_new


def kernel(x_prompt, x_sample, cache_sb_k, cache_sb_v, state_rwkv, state_rwkv_shift, c_prompt, c_sample, w_ada, b_ada, w_in, tokshift_mu, rw_w0, rw_w2, rw_a0, rw_a2, rw_g2, rw_k_k, rw_k_a, rw_r_k, rw_lnx_w, rw_lnx_b, w_branch_a, w_branch_b, w_out, ln1_g, ln1_b, w_router_group, b_router_group, w_router_expert, b_router_expert, w_exp_gate, w_exp_up, w_exp_down, ln2_g, ln2_b):
    depth = w_ada.shape[0]
    assert depth == 1
    bp, tp, d = x_prompt.shape
    bs, ts, _ = x_sample.shape
    l = 0

    w_in_l = w_in[l]
    w_rw = w_in_l[:, 3 * WIDTH:3 * WIDTH + RW_COLS]
    row_vec = lambda a: a.reshape(1, -1)
    pad_cols = lambda a, n: jnp.pad(a, ((0, 0), (0, n - a.shape[1])))
    n_lg = RW_PAD - 3 * WIDTH - LANES
    wts = {
        "w_q": w_in_l[:, 0:WIDTH].astype(BF16),
        "w_k": w_in_l[:, WIDTH:2 * WIDTH].astype(BF16),
        "w_v": w_in_l[:, 2 * WIDTH:3 * WIDTH].astype(BF16),
        "w_m": pad_cols(w_rw, RW_PAD).astype(BF16),
        "w_g": w_in_l[:, 3 * WIDTH + RW_COLS:].astype(BF16),
        "p_a": w_branch_a[l].astype(BF16),
        "p_b": w_branch_b[l].astype(BF16),
        "w_out": w_out[l].astype(BF16),
        "ln1_g": row_vec(ln1_g[l]),
        "ln1_b": row_vec(ln1_b[l]),
        "w_route": pad_cols(jnp.concatenate([w_router_group[l], w_router_expert[l]], axis=1), LANES),
        "b_route": pad_cols(row_vec(jnp.concatenate([b_router_group[l], b_router_expert[l]])), LANES),
        "rw": {
            "mu": pad_cols(row_vec(tokshift_mu[l]), RW_PAD),
            "w0": row_vec(rw_w0[l]),
            "w2": _pad_rows(rw_w2[l], LANES),
            "a0": row_vec(rw_a0[l]),
            "a2": jnp.concatenate([jnp.zeros((LORA_W, WIDTH), F32), rw_a2[l]], axis=0),
            "g2": _pad_rows(rw_g2[l], n_lg),
            "k_k": row_vec(rw_k_k[l]),
            "k_a": row_vec(rw_k_a[l]),
            "r_k": row_vec(rw_r_k[l]),
            "lnx_w": row_vec(rw_lnx_w[l]),
            "lnx_b": row_vec(rw_lnx_b[l]),
        },
    }

    rows = -(-(bp + bs) // 8) * 8
    c_all = _pad_rows(jnp.concatenate([c_prompt, c_sample], axis=0), rows)
    ada = _ada(c_all, w_ada[l], b_ada[l])

    empty = jnp.zeros((bp, 0, WIDTH), F32)
    zero_state = jnp.zeros((bp, WIDTH // HEAD_DIM, HEAD_DIM, HEAD_DIM), F32)
    zero_shift = jnp.zeros((bp, 1, RW_COLS), F32)
    hp, u2p, route_p, cnt_p, g2p, kp, vp, sp_, shp = _mixers(
        x_prompt, ada[:bp], empty, empty, zero_state, zero_shift, wts, (min(256, tp), 256, 4),
        jnp.zeros((1, LANES), F32))
    hs, u2s, route_s, cnt_all, g2s, ks, vs, ss_, shs = _mixers(
        x_sample, ada[bp:bp + bs], cache_sb_k[l], cache_sb_v[l], state_rwkv[l], state_rwkv_shift[l], wts,
        (min(256, ts), 256, 4), cnt_p)

    n_p, n_s = bp * tp, bs * ts
    n_tok = n_p + n_s
    tb = 512
    route_all = jnp.concatenate([route_p.reshape(n_p, LANES), route_s.reshape(n_s, LANES)], axis=0)
    flat_e = route_all[:, 0:2].astype(I32).reshape(n_tok * 2)
    rank = route_all[:, 4:6].astype(I32).reshape(n_tok * 2)
    counts = cnt_all[0, N_GROUPS:N_GROUPS + N_EXPERTS].astype(I32)
    padded = (counts + tb - 1) // tb * tb
    pend = jnp.cumsum(padded)
    pstart = pend - padded
    dest = (pstart[flat_e] + rank).astype(I32)
    n_blocks = (2 * n_tok + N_EXPERTS * (tb - 1)) // tb + 1
    slot_token = jnp.zeros((n_blocks * tb,), I32).at[dest].set(
        jnp.arange(2 * n_tok, dtype=I32) // 2, unique_indices=True, mode="promise_in_bounds")
    block_start = jnp.arange(n_blocks, dtype=I32) * tb
    block_expert = jnp.minimum(jnp.sum((pend[None, :] <= block_start[:, None]).astype(I32), axis=1), N_EXPERTS - 1)
    n_used = (pend[-1] // tb).astype(I32).reshape(1)

    u2_all = jnp.concatenate([u2p.reshape(n_p, d), u2s.reshape(n_s, d)], axis=0)
    y_sorted = _moe(block_expert, n_used, slot_token, u2_all, w_exp_gate[l].astype(BF16),
                    w_exp_up[l].astype(BF16), w_exp_down[l].astype(BF16), tb)

    ln2g, ln2b = row_vec(ln2_g[l]), row_vec(ln2_b[l])
    y_prompt = _final(dest[:2 * n_p], y_sorted, hp, route_p, g2p, ln2g, ln2b)
    y_sample = _final(dest[2 * n_p:], y_sorted, hs, route_s, g2s, ln2g, ln2b)

    heads = lambda a: a.reshape(1, a.shape[0], a.shape[1], WIDTH // HEAD_DIM, HEAD_DIM)
    return (y_prompt, y_sample, heads(kp), heads(vp), sp_[None], shp[None],
            heads(ks), heads(vs), ss_[None], shs[None])
```

```python
import functools

import jax
import jax.numpy as jnp
import numpy as np
from jax import lax
from jax.experimental import pallas as pl
from jax.experimental.pallas import tpu as pltpu

F32 = jnp.float32
BF16 = jnp.bfloat16
I32 = jnp.int32

D_MODEL = 2048
HEAD_DIM = 64
WIDTH = 1024
LORA_W = 64
LORA_A = 64
LORA_G = 160
RW_COLS = 3 * WIDTH + LORA_W + LORA_A + LORA_G
RW_PAD = 3456
N_GROUPS = 4
PER_GROUP = 8
N_EXPERTS = 32
D_EXPERT = 1024
LN_EPS = 1e-5
GN_EPS = 64e-5
ALPHA = 2.0 ** 0.25
LOG2E = 1.4426950408889634
F32_UNDERFLOW_LOG2 = -150.0
LANES = 128
CHUNK = 64
VMEM_LIMIT = 56 * 1024 * 1024

NN = (((1,), (0,)), ((), ()))
NT = (((1,), (1,)), ((), ()))
TN = (((0,), (0,)), ((), ()))


def _cparams(*sem):
    return pltpu.CompilerParams(dimension_semantics=sem, vmem_limit_bytes=VMEM_LIMIT)


def _mm(a, b, dims=NN):
    return lax.dot_general(a, b, dims, preferred_element_type=F32)


def _split(x):
    hi = x.astype(BF16)
    lo = (x - hi.astype(F32)).astype(BF16)
    return hi, lo


def _split3(x):
    hi = x.astype(BF16)
    r1 = x - hi.astype(F32)
    mid = r1.astype(BF16)
    lo = (r1 - mid.astype(F32)).astype(BF16)
    return hi, mid, lo


def _dot3(a, b, dims=NN):
    ah, al = _split(a)
    bh, bl = _split(b)
    return _mm(ah, bh, dims) + (_mm(al, bh, dims) + _mm(ah, bl, dims))


def _dot1(a, b, dims=NN):
    return _mm(a.astype(BF16), b.astype(BF16), dims)


def _dot_exact_rhs(a, b_bf16, terms=2):
    parts = _split(a) if terms == 2 else _split3(a)
    out = _mm(parts[0], b_bf16)
    for p in parts[1:]:
        out = out + _mm(p, b_bf16)
    return out


def _softplus(z):
    return jnp.maximum(z, 0.0) + jnp.log(1.0 + jnp.exp(-jnp.abs(z)))


def _sigmoid(z):
    return 1.0 / (1.0 + jnp.exp(-z))


def _ada_kernel(c_ref, w_ref, b_ref, o_ref):
    c = c_ref[...]
    o_ref[...] = _dot3(c * _sigmoid(c), w_ref[...]) + b_ref[...]


def _ada(c_all, w_ada, b_ada):
    rows = c_all.shape[0]
    n = w_ada.shape[1]
    tn = 1024
    return pl.pallas_call(
        _ada_kernel,
        grid=(n // tn,),
        in_specs=[pl.BlockSpec((rows, D_MODEL), lambda j: (0, 0)),
                  pl.BlockSpec((D_MODEL, tn), lambda j: (0, j)),
                  pl.BlockSpec((1, tn), lambda j: (0, j))],
        out_specs=pl.BlockSpec((rows, tn), lambda j: (0, j)),
        out_shape=jax.ShapeDtypeStruct((rows, n), F32),
        compiler_params=_cparams("arbitrary"),
        name="ada",
    )(c_all, w_ada, b_ada.reshape(1, n))


def _modmm_kernel(x_ref, sc_ref, sh_ref, w_ref, o_ref, u_ref, *, sigmoid_out):
    @pl.when(pl.program_id(2) == 0)
    def _():
        u_ref[...] = (x_ref[0] * (1.0 + sc_ref[0]) + sh_ref[0]).astype(BF16)

    y = _mm(u_ref[...], w_ref[...])
    if sigmoid_out:
        y = _sigmoid(y)
    o_ref[0] = y.astype(o_ref.dtype)


def _modmm(x, sc, sh, w_bf16, tn, out_dtype=F32, sigmoid_out=False, name="proj"):
    b, t, d = x.shape
    n = w_bf16.shape[1]
    tm = min(1024, t)
    return pl.pallas_call(
        functools.partial(_modmm_kernel, sigmoid_out=sigmoid_out),
        grid=(b, t // tm, n // tn),
        in_specs=[pl.BlockSpec((1, tm, d), lambda bi, i, j: (bi, i, 0)),
                  pl.BlockSpec((1, 1, d), lambda bi, i, j: (bi, 0, 0)),
                  pl.BlockSpec((1, 1, d), lambda bi, i, j: (bi, 0, 0)),
                  pl.BlockSpec((d, tn), lambda bi, i, j: (0, j))],
        out_specs=pl.BlockSpec((1, tm, tn), lambda bi, i, j: (bi, i, j)),
        out_shape=jax.ShapeDtypeStruct((b, t, n), out_dtype),
        scratch_shapes=[pltpu.VMEM((tm, d), BF16)],
        compiler_params=_cparams("arbitrary", "arbitrary", "arbitrary"),
        name=name,
    )(x, sc, sh, w_bf16)


def _sb_kernel(q_ref, k_hbm, v_hbm, o_ref, kbuf, vbuf, sem, done_ref, *scratch, tq, tk, past, nl):
    accs, cars = scratch[:2 * nl], scratch[2 * nl:]
    bi, g, qi = pl.program_id(0), pl.program_id(1), pl.program_id(2)
    width = nl * LANES
    n_g, n_q = pl.num_programs(1), pl.num_programs(2)
    step = (bi * n_g + g) * n_q + qi
    n_steps = pl.num_programs(0) * n_g * n_q
    top_of = lambda q_idx: (past + (q_idx + 1) * tq - 2) // tk
    top = top_of(qi)

    def kv_copies(kb, slot, b_idx=bi, g_idx=g):
        src = (b_idx, pl.ds(kb * tk, tk), pl.ds(g_idx * width, width))
        return (pltpu.make_async_copy(k_hbm.at[src], kbuf.at[slot], sem.at[0, slot]),
                pltpu.make_async_copy(v_hbm.at[src], vbuf.at[slot], sem.at[1, slot]))

    @pl.when(step == 0)
    def _():
        for cp in kv_copies(top, 0):
            cp.start()

    for ref in scratch:
        ref[...] = jnp.zeros_like(ref)
    done_ref[0] = 0

    later = (lax.broadcasted_iota(I32, (tk, tk), 0) > lax.broadcasted_iota(I32, (tk, tk), 1)).astype(BF16)
    lane = lax.broadcasted_iota(I32, (tq, LANES), 1)

    def sweep(kb, slot, masked):
        if masked:
            pos = past + qi * tq + lax.broadcasted_iota(I32, (tq, tk), 0)
            kpos = kb * tk + lax.broadcasted_iota(I32, (tq, tk), 1)
            mask = kpos < pos
        for l in range(nl):
            cols = slice(l * LANES, (l + 1) * LANES)
            q = q_ref[0, :, cols] * (HEAD_DIM ** -0.5 * LOG2E)
            kbf = kbuf[slot, :, cols].astype(BF16)
            vbf = vbuf[slot, :, cols].astype(BF16)
            for hh in range(2):
                head = (lane < HEAD_DIM) if hh == 0 else (lane >= HEAD_DIM)
                z = _mm(jnp.where(head, q, 0.0).astype(BF16), kbf, NT)
                neg_abs = lax.bitcast_convert_type(lax.bitcast_convert_type(z, jnp.uint32) | jnp.uint32(0x80000000), F32)
                soft = jnp.log2(1.0 + jnp.exp2(neg_abs))
                log_beta = jnp.minimum(z, 0.0) - soft
                log_keep = log_beta - z
                if masked:
                    log_keep = jnp.where(mask, log_keep, 0.0)
                car = cars[2 * l + hh]
                after = _mm(log_keep.astype(BF16), later) + car[...]
                w = jnp.exp2(log_beta + after)
                if masked:
                    w = jnp.where(mask, w, 0.0)
                accs[2 * l + hh][...] += _mm(w.astype(BF16), vbf)
                car[...] += jnp.sum(log_keep, axis=1, keepdims=True)
        worst = cars[0][...]
        for car in cars[1:]:
            worst = jnp.maximum(worst, car[...])
        done_ref[0] = (jnp.max(worst) < F32_UNDERFLOW_LOG2).astype(I32)

    def body(kb):
        slot = (top - kb) % 2
        for cp in kv_copies(kb, slot):
            cp.wait()

        @pl.when(kb > 0)
        def _():
            for cp in kv_copies(kb - 1, 1 - slot):
                cp.start()

        diag = (kb + 1) * tk - 1 >= past + qi * tq

        @pl.when(diag)
        def _():
            sweep(kb, slot, True)

        @pl.when(jnp.logical_not(diag))
        def _():
            sweep(kb, slot, False)

        return kb - 1

    kb_end = lax.while_loop(lambda kb: (kb >= 0) & (done_ref[0] == 0), body, top)

    @pl.when(kb_end >= 0)
    def _():
        for cp in kv_copies(kb_end, (top - kb_end) % 2):
            cp.wait()

    @pl.when(step + 1 < n_steps)
    def _():
        wrap_q = qi + 1 == n_q
        wrap_g = wrap_q & (g + 1 == n_g)
        q_next = jnp.where(wrap_q, 0, qi + 1)
        g_next = jnp.where(wrap_g, 0, jnp.where(wrap_q, g + 1, g))
        b_next = jnp.where(wrap_g, bi + 1, bi)
        for cp in kv_copies(top_of(q_next), 0, b_next, g_next):
            cp.start()

    for l in range(nl):
        o_ref[0, :, l * LANES:(l + 1) * LANES] = jnp.where(lane < HEAD_DIM, accs[2 * l][...], accs[2 * l + 1][...])


def _sb_attention(q, k_all, v_all, past, tq, tk, nl):
    b, t, _ = q.shape
    width = nl * LANES
    q_spec = pl.BlockSpec((1, tq, width), lambda bi, g, qi: (bi, qi, g))
    return pl.pallas_call(
        functools.partial(_sb_kernel, tq=tq, tk=tk, past=past, nl=nl),
        grid=(b, WIDTH // width, t // tq),
        in_specs=[q_spec, pl.BlockSpec(memory_space=pl.ANY), pl.BlockSpec(memory_space=pl.ANY)],
        out_specs=q_spec,
        out_shape=jax.ShapeDtypeStruct((b, t, WIDTH), F32),
        scratch_shapes=[pltpu.VMEM((2, tk, width), F32), pltpu.VMEM((2, tk, width), F32),
                        pltpu.SemaphoreType.DMA((2, 2)), pltpu.SMEM((1,), I32)]
        + [pltpu.VMEM((tq, LANES), F32)] * (2 * nl) + [pltpu.VMEM((tq, 1), F32)] * (2 * nl),
        compiler_params=_cparams("arbitrary", "arbitrary", "arbitrary"),
        name="sb_attention",
    )(q, k_all, v_all)


def _head_sum(x, same_head):
    return jnp.concatenate(
        [_dot1(x[:, c * LANES:(c + 1) * LANES], same_head) for c in range(WIDTH // LANES)], axis=1)


def _rw_prep_kernel(m_ref, sh0_ref, mu_ref, w0_ref, w2_ref, a0_ref, a2_ref, g2_ref, kk_ref, ka_ref, rk_ref,
                    wt_ref, ut_ref, rt_ref, mrb_ref, yv_ref, bt_ref, kt_ref, v_ref, bv_ref, g_ref, pend_ref, prev_ref,
                    *, tr):
    @pl.when(pl.program_id(1) == 0)
    def _():
        prev_ref[...] = sh0_ref[0]

    m = m_ref[0]
    row = lax.broadcasted_iota(I32, m.shape, 0)
    m_prev = jnp.where(row == 0, prev_ref[...], pltpu.roll(m, 1, 0))
    prev_ref[...] = m[tr - 1:tr, :]
    ms = m + (m_prev - m) * mu_ref[...]
    r = ms[:, 0:WIDTH]
    k = ms[:, WIDTH:2 * WIDTH]
    v = ms[:, 2 * WIDTH:3 * WIDTH]
    lora_wa = ms[:, 3 * WIDTH:3 * WIDTH + LANES]
    lora_g = ms[:, 3 * WIDTH + LANES:RW_PAD]

    w_log = -_softplus(-(w0_ref[...] + _dot1(jnp.tanh(lora_wa), w2_ref[...]))) - 0.5
    log_decay = -jnp.exp(w_log)
    a = _sigmoid(a0_ref[...] + _dot1(lora_wa, a2_ref[...]))
    g = _dot1(_sigmoid(lora_g), g2_ref[...])

    same_head = (lax.broadcasted_iota(I32, (LANES, LANES), 0) // HEAD_DIM ==
                 lax.broadcasted_iota(I32, (LANES, LANES), 1) // HEAD_DIM).astype(BF16)
    kk = k * kk_ref[...]
    kk = kk / jnp.maximum(jnp.sqrt(_head_sum(kk * kk, same_head)), 1e-12)
    kh = k * (1.0 + (a - 1.0) * ka_ref[...])
    bonus = _head_sum(r * kh * rk_ref[...], same_head)

    c = CHUNK
    ti = lax.broadcasted_iota(I32, (tr, tr), 0)
    si = lax.broadcasted_iota(I32, (tr, tr), 1)
    tri = ((si <= ti) & (si // c == ti // c)).astype(BF16)
    parts = _split3(log_decay)
    cl = _mm(tri, parts[0]) + (_mm(tri, parts[1]) + _mm(tri, parts[2]))
    ec = jnp.exp(cl)
    inv = jnp.exp(-cl)
    rt = r * ec
    kt = kh * inv
    at = -kk * jnp.exp(cl - log_decay)
    bt = kk * a * inv
    rt_ref[0] = rt.astype(BF16)
    kt_ref[0] = kt.astype(BF16)
    bt_ref[0] = bt.astype(BF16)
    v_ref[0] = v.astype(BF16)
    bv_ref[0] = bonus * v
    g_ref[0] = g
    for ch in range(tr // c):
        pend_ref[0, ch] = ec[(ch + 1) * c - 1:(ch + 1) * c, :]

    lane = lax.broadcasted_iota(I32, (c, LANES), 1)
    first = lane < HEAD_DIM
    t_row = lax.broadcasted_iota(I32, (c, LANES), 0)
    s_col = lane % HEAD_DIM
    strict = s_col < t_row
    incl = s_col <= t_row
    eye2 = (s_col == t_row).astype(F32)

    def stack2(x):
        return jnp.concatenate([jnp.where(first, x, 0.0), jnp.where(first, 0.0, x)], axis=0)

    tiles = [(slice(ch * c, (ch + 1) * c), slice(hp * LANES, (hp + 1) * LANES))
             for ch in range(tr // c) for hp in range(WIDTH // LANES)]
    rows2 = lambda top, bottom: jnp.concatenate([top, bottom], axis=0)
    a_t = [at[tl] for tl in tiles]
    l_ab, g_v = [], []
    for i, tl in enumerate(tiles):
        b_s, k_s, v_s = stack2(bt[tl]), stack2(kt[tl]), stack2(v[tl])
        ar = rows2(a_t[i], rt[tl])
        on_b = _dot1(ar, b_s, NT)
        on_k = _dot1(ar, k_s, NT)
        l_ab.append(jnp.where(strict, on_b[:c], 0.0))
        mrb_ref[(0,) + tl] = jnp.where(incl, on_b[c:], 0.0).astype(BF16)
        on_v = _dot1(rows2(jnp.where(strict, on_k[:c], 0.0), jnp.where(incl, on_k[c:], 0.0)), v_s)
        g_v.append(on_v[:c])
        yv_ref[(0,) + tl] = on_v[c:]
    tinv = [eye2 + l for l in l_ab]
    pw = [_dot1(x, stack2(x)) for x in l_ab]
    for _ in range(4):
        both = [_dot1(rows2(p_, t_), stack2(p_)) for p_, t_ in zip(pw, tinv)]
        tinv = [t_ + r_[c:] for t_, r_ in zip(tinv, both)]
        pw = [r_[:c] for r_ in both]
    tinv = [t_ + _dot1(t_, stack2(p_)) for t_, p_ in zip(tinv, pw)]
    for i, tl in enumerate(tiles):
        wu = _dot1(tinv[i], jnp.concatenate([stack2(a_t[i]), stack2(g_v[i])], axis=1))
        wt_ref[(0,) + tl] = wu[:, :LANES].astype(BF16)
        ut_ref[(0,) + tl] = wu[:, LANES:]


def _rw_prep(m, shift0, rwp):
    b, t, _ = m.shape
    tr = min(2 * CHUNK, t)
    row_spec = pl.BlockSpec((1, tr, WIDTH), lambda bi, i: (bi, i, 0))
    vec = lambda n: pl.BlockSpec((1, n), lambda bi, i: (0, 0))
    mat = lambda r: pl.BlockSpec((r, WIDTH), lambda bi, i: (0, 0))
    out = jax.ShapeDtypeStruct((b, t, WIDTH), F32)
    out_mxu = jax.ShapeDtypeStruct((b, t, WIDTH), BF16)
    return pl.pallas_call(
        functools.partial(_rw_prep_kernel, tr=tr),
        grid=(b, t // tr),
        in_specs=[pl.BlockSpec((1, tr, RW_PAD), lambda bi, i: (bi, i, 0)),
                  pl.BlockSpec((1, 1, RW_PAD), lambda bi, i: (bi, 0, 0)),
                  vec(RW_PAD), vec(WIDTH), mat(LANES), vec(WIDTH), mat(LANES), mat(RW_PAD - 3 * WIDTH - LANES),
                  vec(WIDTH), vec(WIDTH), vec(WIDTH)],
        out_specs=[row_spec] * 10 + [pl.BlockSpec((1, tr // CHUNK, 1, WIDTH), lambda bi, i: (bi, i, 0, 0))],
        out_shape=[out_mxu, out, out_mxu, out_mxu, out, out_mxu, out_mxu, out_mxu, out, out,
                   jax.ShapeDtypeStruct((b, t // CHUNK, 1, WIDTH), F32)],
        scratch_shapes=[pltpu.VMEM((1, RW_PAD), F32)],
        compiler_params=_cparams("arbitrary", "arbitrary"),
        name="rwkv_prep",
    )(m, shift0, rwp["mu"], rwp["w0"], rwp["w2"], rwp["a0"], rwp["a2"], rwp["g2"], rwp["k_k"], rwp["k_a"],
      rwp["r_k"])


def _rw_scan_kernel(wt_ref, ut_ref, rt_ref, mrb_ref, yv_ref, bt_ref, kt_ref, v_ref, bv_ref, g_ref, pend_ref,
                    s0_ref, lnw_ref, lnb_ref, o_ref, sout_ref, *s_refs, n_chunks, npair):
    ci = pl.program_id(2)

    @pl.when(ci == 0)
    def _():
        for pi in range(npair):
            s_refs[pi][...] = s0_ref[0, pi]

    c = CHUNK
    first = lax.broadcasted_iota(I32, (c, LANES), 1) < HEAD_DIM
    vi = lax.broadcasted_iota(I32, (LANES, LANES), 0)
    kj = lax.broadcasted_iota(I32, (LANES, LANES), 1)
    same_head_b = (vi // HEAD_DIM) == (kj // HEAD_DIM)
    same_head = same_head_b.astype(BF16)

    def stack2(x):
        return jnp.concatenate([jnp.where(first, x, 0.0), jnp.where(first, 0.0, x)], axis=0)

    def chunk(j, carry):
        rows = pl.ds(pl.multiple_of(j * c, c), c)
        p_all = pend_ref[0, j]
        pairs = range(npair)
        col = lambda pi: slice(pi * LANES, (pi + 1) * LANES)
        s = [s_refs[pi][...] for pi in pairs]
        u = [_dot1(wt_ref[0, rows, col(pi)], s[pi], NT) + ut_ref[0, rows, col(pi)] for pi in pairs]
        for pi in pairs:
            ds = _dot1(jnp.concatenate([u[pi].astype(BF16), v_ref[0, rows, col(pi)]], axis=0),
                       jnp.concatenate([bt_ref[0, rows, col(pi)], kt_ref[0, rows, col(pi)]], axis=0), TN)
            s_refs[pi][...] = (s[pi] + jnp.where(same_head_b, ds, 0.0)) * p_all[:, col(pi)]
        y = [_dot1(rt_ref[0, rows, col(pi)], s[pi], NT) + (_dot1(mrb_ref[0, rows, col(pi)], stack2(u[pi]))
                                                           + yv_ref[0, rows, col(pi)]) for pi in pairs]
        for pi in pairs:
            mu = _dot1(y[pi], same_head) * (1.0 / HEAD_DIM)
            dlt = y[pi] - mu
            var = _dot1(dlt * dlt, same_head) * (1.0 / HEAD_DIM)
            yn = dlt * lax.rsqrt(var + GN_EPS) * lnw_ref[:, col(pi)] + lnb_ref[:, col(pi)]
            o_ref[0, rows, col(pi)] = (yn + bv_ref[0, rows, col(pi)]) * g_ref[0, rows, col(pi)]
        return carry

    lax.fori_loop(0, n_chunks, chunk, 0)

    @pl.when(ci == pl.num_programs(2) - 1)
    def _():
        for pi in range(npair):
            sout_ref[0, pi] = s_refs[pi][...]


def _rw_scan(prep, s0_bd, lnw, lnb, npair=8):
    b, t, _ = prep[0].shape
    tc = min(512, t)
    n_chunks = tc // CHUNK
    ng = WIDTH // (npair * LANES)
    row_spec = pl.BlockSpec((1, tc, npair * LANES), lambda bi, h, i: (bi, i, h))
    pend_spec = pl.BlockSpec((1, n_chunks, 1, npair * LANES), lambda bi, h, i: (bi, i, 0, h))
    st_spec = pl.BlockSpec((1, npair, LANES, LANES), lambda bi, h, i: (bi, h, 0, 0))
    vec_spec = pl.BlockSpec((1, npair * LANES), lambda bi, h, i: (0, h))
    return pl.pallas_call(
        functools.partial(_rw_scan_kernel, n_chunks=n_chunks, npair=npair),
        grid=(b, ng, t // tc),
        in_specs=[row_spec] * 10 + [pend_spec, st_spec, vec_spec, vec_spec],
        out_specs=[row_spec, st_spec],
        out_shape=[jax.ShapeDtypeStruct((b, t, WIDTH), F32),
                   jax.ShapeDtypeStruct((b, WIDTH // LANES, LANES, LANES), F32)],
        scratch_shapes=[pltpu.VMEM((LANES, LANES), F32)] * npair,
        compiler_params=_cparams("arbitrary", "arbitrary", "arbitrary"),
        name="rwkv_scan",
    )(*prep, s0_bd, lnw, lnb)


def _merge_kernel(oa_ref, ob_ref, sg_ref, x_ref, g1_ref, sc2_ref, sh2_ref, pa_ref, pb_ref, wo_ref, lng_ref, lnb_ref,
                  wr_ref, br_ref, cnt0_ref, h_ref, u2_ref, route_ref, cnt_out_ref, cnt_ref):
    sg = sg_ref[0]
    ya = _mm(oa_ref[0].astype(BF16), pa_ref[...])
    yb = _mm(ob_ref[0].astype(BF16), pb_ref[...])
    merged = sg[:, :D_MODEL].astype(F32) * ya + sg[:, D_MODEL:].astype(F32) * yb
    mix = _mm(merged.astype(BF16), wo_ref[...])
    hin = ALPHA * x_ref[0] + g1_ref[0] * mix
    mu = jnp.mean(hin, axis=-1, keepdims=True)
    dlt = hin - mu
    var = jnp.mean(dlt * dlt, axis=-1, keepdims=True)
    h = dlt * lax.rsqrt(var + LN_EPS) * lng_ref[...] + lnb_ref[...]
    h_ref[0] = h
    u2 = h * (1.0 + sc2_ref[0]) + sh2_ref[0]
    u2_ref[0] = u2

    logits = _dot3(u2, wr_ref[...]) + br_ref[...]
    lane_i = lax.broadcasted_iota(I32, logits.shape, 1)
    lane = lane_i.astype(F32)
    neg = -jnp.inf
    first_at = lambda vals, top: jnp.min(jnp.where(vals == top, lane, float(LANES)), axis=1, keepdims=True)
    lg = jnp.where(lane_i < N_GROUPS, logits, neg)
    mg = jnp.max(lg, axis=1, keepdims=True)
    g_idx = first_at(lg, mg)
    p_g = 1.0 / jnp.sum(jnp.exp(lg - mg), axis=1, keepdims=True)
    lo_lane = N_GROUPS + PER_GROUP * g_idx
    le = jnp.where((lane >= lo_lane) & (lane < lo_lane + PER_GROUP), logits, neg)
    m1 = jnp.max(le, axis=1, keepdims=True)
    i1 = first_at(le, m1)
    le2 = jnp.where(lane == i1, neg, le)
    m2 = jnp.max(le2, axis=1, keepdims=True)
    i2 = first_at(le2, m2)
    ratio = jnp.exp(m2 - m1)
    w1 = p_g / (1.0 + ratio)
    w2 = p_g * ratio / (1.0 + ratio)
    @pl.when((pl.program_id(0) == 0) & (pl.program_id(1) == 0))
    def _():
        cnt_ref[...] = cnt0_ref[...]

    tm = logits.shape[0]
    chosen = ((lane == i1) | (lane == i2)).astype(BF16)
    earlier = (lax.broadcasted_iota(I32, (tm, tm), 1) < lax.broadcasted_iota(I32, (tm, tm), 0)).astype(BF16)
    before = _mm(earlier, chosen) + cnt_ref[...]
    rank1 = jnp.sum(jnp.where(lane == i1, before, 0.0), axis=1, keepdims=True)
    rank2 = jnp.sum(jnp.where(lane == i2, before, 0.0), axis=1, keepdims=True)
    cnt_ref[...] = before[tm - 1:tm, :] + chosen[tm - 1:tm, :].astype(F32)
    cnt_out_ref[...] = cnt_ref[...]

    route = jnp.where(lane_i == 0, i1 - N_GROUPS, 0.0)
    for k, val in enumerate((i2 - N_GROUPS, w1, w2, rank1, rank2), start=1):
        route = jnp.where(lane_i == k, val, route)
    route_ref[0] = route


def _merge(o_a, o_b, sg, x, g1, sc2, sh2, wts, cnt0):
    b, t, d = x.shape
    tm = min(256, t)
    const = lambda shape: pl.BlockSpec(shape, lambda bi, i: (0,) * len(shape), pipeline_mode=pl.Buffered(1))
    seq = pl.BlockSpec((1, 1, d), lambda bi, i: (bi, 0, 0))
    row = lambda n: pl.BlockSpec((1, tm, n), lambda bi, i: (bi, i, 0))
    return pl.pallas_call(
        _merge_kernel,
        grid=(b, t // tm),
        in_specs=[row(WIDTH), row(WIDTH), row(2 * d), row(d), seq, seq, seq,
                  const((WIDTH, d)), const((WIDTH, d)), const((d, d)), const((1, d)), const((1, d)),
                  const((d, LANES)), const((1, LANES)), const((1, LANES))],
        out_specs=[row(d), row(d), row(LANES), pl.BlockSpec((1, LANES), lambda bi, i: (0, 0))],
        out_shape=[jax.ShapeDtypeStruct((b, t, d), F32), jax.ShapeDtypeStruct((b, t, d), F32),
                   jax.ShapeDtypeStruct((b, t, LANES), F32), jax.ShapeDtypeStruct((1, LANES), F32)],
        scratch_shapes=[pltpu.VMEM((1, LANES), F32)],
        compiler_params=_cparams("arbitrary", "arbitrary"),
        name="merge_route",
    )(o_a, o_b, sg, x, g1, sc2, sh2, wts["p_a"], wts["p_b"], wts["w_out"], wts["ln1_g"], wts["ln1_b"],
      wts["w_route"], wts["b_route"], cnt0)


def _moe_kernel(be_ref, nused_ref, tok_ref, x_hbm, wg_ref, wu_ref, wd_ref, y_ref, xbuf, sem, *, tb):
    i = pl.program_id(0)
    n_used = nused_ref[0]
    last = n_used - 1

    def row_copy(block, r, slot):
        return pltpu.make_async_copy(x_hbm.at[pl.ds(tok_ref[block * tb + r], 1)], xbuf.at[slot, pl.ds(r, 1)],
                                     sem.at[slot])

    def wait_slot(slot):
        pltpu.make_async_copy(x_hbm.at[pl.ds(0, tb)], xbuf.at[slot], sem.at[slot]).wait()

    @pl.when((i == 0) & (n_used > 0))
    def _():
        def start(r, carry):
            row_copy(0, r, 0).start()
            return carry

        lax.fori_loop(0, tb, start, 0, unroll=8)

    @pl.when(i < n_used)
    def _():
        slot = i % 2
        wait_slot(slot)
        x = xbuf[slot].astype(BF16)
        nxt = jnp.minimum(i + 1, last)
        for r in range(tb):
            row_copy(nxt, r, 1 - slot).start()
        hg = _mm(x, wg_ref[0])
        hu = _mm(x, wu_ref[0])
        hid = (hg * _sigmoid(hg) * hu).astype(BF16)
        y_ref[...] = _mm(hid, wd_ref[0])

    @pl.when(i == last)
    def _():
        wait_slot(1 - i % 2)

    @pl.when(i >= n_used)
    def _():
        y_ref[...] = jnp.zeros_like(y_ref)


def _moe(block_expert, n_used, slot_token, x_all, wg, wu, wd, tb):
    n_blocks = block_expert.shape[0]
    d = x_all.shape[1]
    grid_spec = pltpu.PrefetchScalarGridSpec(
        num_scalar_prefetch=3,
        grid=(n_blocks,),
        in_specs=[pl.BlockSpec(memory_space=pl.ANY),
                  pl.BlockSpec((1, d, D_EXPERT), lambda i, be, nu, tok: (be[i], 0, 0), pipeline_mode=pl.Buffered(1)),
                  pl.BlockSpec((1, d, D_EXPERT), lambda i, be, nu, tok: (be[i], 0, 0), pipeline_mode=pl.Buffered(1)),
                  pl.BlockSpec((1, D_EXPERT, d), lambda i, be, nu, tok: (be[i], 0, 0), pipeline_mode=pl.Buffered(1))],
        out_specs=pl.BlockSpec((tb, d), lambda i, be, nu, tok: (i, 0)),
        scratch_shapes=[pltpu.VMEM((2, tb, d), F32), pltpu.SemaphoreType.DMA((2,))],
    )
    return pl.pallas_call(
        functools.partial(_moe_kernel, tb=tb),
        grid_spec=grid_spec,
        out_shape=jax.ShapeDtypeStruct((n_blocks * tb, d), F32),
        compiler_params=_cparams("arbitrary"),
        name="moe_experts",
    )(block_expert, n_used, slot_token, x_all, wg, wu, wd)


def _final_kernel(dest_ref, y_hbm, h_ref, rw_ref, g2_ref, lng_ref, lnb_ref, o_ref, ybuf, sem, *, tm, per_seq):
    step = pl.program_id(0) * per_seq + pl.program_id(1)
    n_steps = pl.num_programs(0) * per_seq

    def row_copy(blk, r, k, slot):
        return pltpu.make_async_copy(y_hbm.at[pl.ds(dest_ref[2 * (blk * tm + r) + k], 1)],
                                     ybuf.at[slot, k, pl.ds(r, 1)], sem.at[slot])

    def wait_slot(slot):
        for k in range(2):
            pltpu.make_async_copy(y_hbm.at[pl.ds(0, tm)], ybuf.at[slot, k], sem.at[slot]).wait()

    @pl.when(step == 0)
    def _():
        def start(r, carry):
            for k in range(2):
                row_copy(0, r, k, 0).start()
            return carry

        lax.fori_loop(0, tm, start, 0, unroll=4)

    slot = step % 2
    wait_slot(slot)
    nxt = jnp.minimum(step + 1, n_steps - 1)
    for r in range(tm):
        for k in range(2):
            row_copy(nxt, r, k, 1 - slot).start()
    rw = rw_ref[0]
    moe = rw[:, 2:3] * ybuf[slot, 0] + rw[:, 3:4] * ybuf[slot, 1]
    xin = ALPHA * h_ref[0] + g2_ref[0] * moe
    mu = jnp.mean(xin, axis=-1, keepdims=True)
    dlt = xin - mu
    var = jnp.mean(dlt * dlt, axis=-1, keepdims=True)
    o_ref[0] = dlt * lax.rsqrt(var + LN_EPS) * lng_ref[...] + lnb_ref[...]

    @pl.when(step == n_steps - 1)
    def _():
        wait_slot(1 - slot)


def _final(dest, y_sorted, h, route, g2, ln2_g, ln2_b):
    b, t, d = h.shape
    tm = min(256, t)
    per_seq = t // tm
    grid_spec = pltpu.PrefetchScalarGridSpec(
        num_scalar_prefetch=1,
        grid=(b, per_seq),
        in_specs=[pl.BlockSpec(memory_space=pl.ANY),
                  pl.BlockSpec((1, tm, d), lambda bi, i, de: (bi, i, 0)),
                  pl.BlockSpec((1, tm, LANES), lambda bi, i, de: (bi, i, 0)),
                  pl.BlockSpec((1, 1, d), lambda bi, i, de: (bi, 0, 0)),
                  pl.BlockSpec((1, d), lambda bi, i, de: (0, 0)),
                  pl.BlockSpec((1, d), lambda bi, i, de: (0, 0))],
        out_specs=pl.BlockSpec((1, tm, d), lambda bi, i, de: (bi, i, 0)),
        scratch_shapes=[pltpu.VMEM((2, 2, tm, d), F32), pltpu.SemaphoreType.DMA((2,))],
    )
    return pl.pallas_call(
        functools.partial(_final_kernel, tm=tm, per_seq=per_seq),
        grid_spec=grid_spec,
        out_shape=jax.ShapeDtypeStruct((b, t, d), F32),
        compiler_params=_cparams("arbitrary", "arbitrary"),
        name="combine_ln2",
    )(dest, y_sorted, h, route, g2, ln2_g, ln2_b)


def _pad_rows(a, rows):
    return jnp.pad(a, ((0, rows - a.shape[0]), (0, 0)))


def _mixers(x, ada, past_k, past_v, s0, shift0, wts, sb_tiles, cnt0):
    b, t, d = x.shape
    sh1, sc1, g1, sh2, sc2, g2 = [a[:, None, :] for a in jnp.split(ada, 6, axis=-1)]
    q = _modmm(x, sc1, sh1, wts["w_q"], 1024, name="proj_q")
    k = _modmm(x, sc1, sh1, wts["w_k"], 1024, name="proj_k")
    v = _modmm(x, sc1, sh1, wts["w_v"], 1024, name="proj_v")
    m = _modmm(x, sc1, sh1, wts["w_m"], RW_PAD // 3, name="proj_rwkv")
    sg = _modmm(x, sc1, sh1, wts["w_g"], 1024, out_dtype=BF16, sigmoid_out=True, name="proj_gates")

    tq, tk, nl = sb_tiles
    past = past_k.shape[1]
    s_pad = -(-(past + t) // tk) * tk
    tail = jnp.zeros((b, s_pad - past - t, WIDTH), F32)
    with_history = lambda old, new: jnp.concatenate(
        [p for p in (old.reshape(b, past, WIDTH), new, tail) if p.shape[1] > 0], axis=1)
    k_all = with_history(past_k, k)
    v_all = with_history(past_v, v)
    o_a = _sb_attention(q, k_all, v_all, past, tq, tk, nl)

    shift0_p = jnp.pad(shift0, ((0, 0), (0, 0), (0, RW_PAD - RW_COLS)))
    prep = _rw_prep(m, shift0_p, wts["rw"])
    nhp = WIDTH // LANES
    s0_pairs = s0.reshape(b, nhp, 2, HEAD_DIM, HEAD_DIM)
    s0_bd = jnp.zeros((b, nhp, LANES, LANES), F32)
    s0_bd = s0_bd.at[:, :, :HEAD_DIM, :HEAD_DIM].set(s0_pairs[:, :, 0]).at[:, :, HEAD_DIM:, HEAD_DIM:].set(s0_pairs[:, :, 1])
    o_b, s_bd = _rw_scan(prep, s0_bd, wts["rw"]["lnx_w"], wts["rw"]["lnx_b"])
    s_new = jnp.stack([s_bd[:, :, :HEAD_DIM, :HEAD_DIM], s_bd[:, :, HEAD_DIM:, HEAD_DIM:]], axis=2)
    s_new = s_new.reshape(b, 2 * nhp, HEAD_DIM, HEAD_DIM)
    shift_new = m[:, t - 1:t, :RW_COLS]

    h, u2, route, cnt = _merge(o_a, o_b, sg, x, g1, sc2, sh2, wts, cnt0)
    return h, u2, route, cnt, g2, k, v, s_new, shift_new


def kernel(x_prompt, x_sample, cache_sb_k, cache_sb_v, state_rwkv, state_rwkv_shift, c_prompt, c_sample, w_ada, b_ada, w_in, tokshift_mu, rw_w0, rw_w2, rw_a0, rw_a2, rw_g2, rw_k_k, rw_k_a, rw_r_k, rw_lnx_w, rw_lnx_b, w_branch_a, w_branch_b, w_out, ln1_g, ln1_b, w_router_group, b_router_group, w_router_expert, b_router_expert, w_exp_gate, w_exp_up, w_exp_down, ln2_g, ln2_b):
    depth = w_ada.shape[0]
    assert depth == 1
    bp, tp, d = x_prompt.shape
    bs, ts, _ = x_sample.shape
    l = 0

    w_in_l = w_in[l]
    w_rw = w_in_l[:, 3 * WIDTH:3 * WIDTH + RW_COLS]
    row_vec = lambda a: a.reshape(1, -1)
    pad_cols = lambda a, n: jnp.pad(a, ((0, 0), (0, n - a.shape[1])))
    n_lg = RW_PAD - 3 * WIDTH - LANES
    wts = {
        "w_q": w_in_l[:, 0:WIDTH].astype(BF16),
        "w_k": w_in_l[:, WIDTH:2 * WIDTH].astype(BF16),
        "w_v": w_in_l[:, 2 * WIDTH:3 * WIDTH].astype(BF16),
        "w_m": pad_cols(w_rw, RW_PAD).astype(BF16),
        "w_g": w_in_l[:, 3 * WIDTH + RW_COLS:].astype(BF16),
        "p_a": w_branch_a[l].astype(BF16),
        "p_b": w_branch_b[l].astype(BF16),
        "w_out": w_out[l].astype(BF16),
        "ln1_g": row_vec(ln1_g[l]),
        "ln1_b": row_vec(ln1_b[l]),
        "w_route": pad_cols(jnp.concatenate([w_router_group[l], w_router_expert[l]], axis=1), LANES),
        "b_route": pad_cols(row_vec(jnp.concatenate([b_router_group[l], b_router_expert[l]])), LANES),
        "rw": {
            "mu": pad_cols(row_vec(tokshift_mu[l]), RW_PAD),
            "w0": row_vec(rw_w0[l]),
            "w2": _pad_rows(rw_w2[l], LANES),
            "a0": row_vec(rw_a0[l]),
            "a2": jnp.concatenate([jnp.zeros((LORA_W, WIDTH), F32), rw_a2[l]], axis=0),
            "g2": _pad_rows(rw_g2[l], n_lg),
            "k_k": row_vec(rw_k_k[l]),
            "k_a": row_vec(rw_k_a[l]),
            "r_k": row_vec(rw_r_k[l]),
            "lnx_w": row_vec(rw_lnx_w[l]),
            "lnx_b": row_vec(rw_lnx_b[l]),
        },
    }

    rows = -(-(bp + bs) // 8) * 8
    c_all = _pad_rows(jnp.concatenate([c_prompt, c_sample], axis=0), rows)
    ada = _ada(c_all, w_ada[l], b_ada[l])

    empty = jnp.zeros((bp, 0, WIDTH), F32)
    zero_state = jnp.zeros((bp, WIDTH // HEAD_DIM, HEAD_DIM, HEAD_DIM), F32)
    zero_shift = jnp.zeros((bp, 1, RW_COLS), F32)
    hp, u2p, route_p, cnt_p, g2p, kp, vp, sp_, shp = _mixers(
        x_prompt, ada[:bp], empty, empty, zero_state, zero_shift, wts, (min(256, tp), 256, 4),
        jnp.zeros((1, LANES), F32))
    hs, u2s, route_s, cnt_all, g2s, ks, vs, ss_, shs = _mixers(
        x_sample, ada[bp:bp + bs], cache_sb_k[l], cache_sb_v[l], state_rwkv[l], state_rwkv_shift[l], wts,
        (min(256, ts), 256, 4), cnt_p)

    n_p, n_s = bp * tp, bs * ts
    n_tok = n_p + n_s
    tb = 512
    route_all = jnp.concatenate([route_p.reshape(n_p, LANES), route_s.reshape(n_s, LANES)], axis=0)
    flat_e = route_all[:, 0:2].astype(I32).reshape(n_tok * 2)
    rank = route_all[:, 4:6].astype(I32).reshape(n_tok * 2)
    counts = cnt_all[0, N_GROUPS:N_GROUPS + N_EXPERTS].astype(I32)
    padded = (counts + tb - 1) // tb * tb
    pend = jnp.cumsum(padded)
    pstart = pend - padded
    dest = (pstart[flat_e] + rank).astype(I32)
    n_blocks = (2 * n_tok + N_EXPERTS * (tb - 1)) // tb + 1
    slot_token = jnp.zeros((n_blocks * tb,), I32).at[dest].set(
        jnp.arange(2 * n_tok, dtype=I32) // 2, unique_indices=True, mode="promise_in_bounds")
    block_start = jnp.arange(n_blocks, dtype=I32) * tb
    block_expert = jnp.minimum(jnp.sum((pend[None, :] <= block_start[:, None]).astype(I32), axis=1), N_EXPERTS - 1)
    n_used = (pend[-1] // tb).astype(I32).reshape(1)

    u2_all = jnp.concatenate([u2p.reshape(n_p, d), u2s.reshape(n_s, d)], axis=0)
    y_sorted = _moe(block_expert, n_used, slot_token, u2_all, w_exp_gate[l].astype(BF16),
                    w_exp_up[l].astype(BF16), w_exp_down[l].astype(BF16), tb)

    ln2g, ln2b = row_vec(ln2_g[l]), row_vec(ln2_b[l])
    y_prompt = _final(dest[:2 * n_p], y_sorted, hp, route_p, g2p, ln2g, ln2b)
    y_sample = _final(dest[2 * n_p:], y_sorted, hs, route_s, g2s, ln2g, ln2b)

    heads = lambda a: a.reshape(1, a.shape[0], a.shape[1], WIDTH // HEAD_DIM, HEAD_DIM)
    return (y_prompt, y_sample, heads(kp), heads(vp), sp_[None], shp[None],
            heads(ks), heads(vs), ss_[None], shs[None])
```

```python
import functools

import jax
import jax.numpy as jnp
import numpy as np
from jax import lax
from jax.experimental import pallas as pl
from jax.experimental.pallas import tpu as pltpu

F32 = jnp.float32
BF16 = jnp.bfloat16
I32 = jnp.int32

D_MODEL = 2048
HEAD_DIM = 64
WIDTH = 1024
LORA_W = 64
LORA_A = 64
LORA_G = 160
RW_COLS = 3 * WIDTH + LORA_W + LORA_A + LORA_G
RW_PAD = 3456
N_GROUPS = 4
PER_GROUP = 8
N_EXPERTS = 32
D_EXPERT = 1024
LN_EPS = 1e-5
GN_EPS = 64e-5
ALPHA = 2.0 ** 0.25
LOG2E = 1.4426950408889634
F32_UNDERFLOW_LOG2 = -150.0
LANES = 128
CHUNK = 64
VMEM_LIMIT = 56 * 1024 * 1024

NN = (((1,), (0,)), ((), ()))
NT = (((1,), (1,)), ((), ()))
TN = (((0,), (0,)), ((), ()))


def _cparams(*sem):
    return pltpu.CompilerParams(dimension_semantics=sem, vmem_limit_bytes=VMEM_LIMIT)


def _mm(a, b, dims=NN):
    return lax.dot_general(a, b, dims, preferred_element_type=F32)


def _split(x):
    hi = x.astype(BF16)
    lo = (x - hi.astype(F32)).astype(BF16)
    return hi, lo


def _split3(x):
    hi = x.astype(BF16)
    r1 = x - hi.astype(F32)
    mid = r1.astype(BF16)
    lo = (r1 - mid.astype(F32)).astype(BF16)
    return hi, mid, lo


def _dot3(a, b, dims=NN):
    ah, al = _split(a)
    bh, bl = _split(b)
    return _mm(ah, bh, dims) + (_mm(al, bh, dims) + _mm(ah, bl, dims))


def _dot1(a, b, dims=NN):
    return _mm(a.astype(BF16), b.astype(BF16), dims)


def _dot_exact_rhs(a, b_bf16, terms=2):
    parts = _split(a) if terms == 2 else _split3(a)
    out = _mm(parts[0], b_bf16)
    for p in parts[1:]:
        out = out + _mm(p, b_bf16)
    return out


def _softplus(z):
    return jnp.maximum(z, 0.0) + jnp.log(1.0 + jnp.exp(-jnp.abs(z)))


def _sigmoid(z):
    return 1.0 / (1.0 + jnp.exp(-z))


def _ada_kernel(c_ref, w_ref, b_ref, o_ref):
    c = c_ref[...]
    o_ref[...] = _dot3(c * _sigmoid(c), w_ref[...]) + b_ref[...]


def _ada(c_all, w_ada, b_ada):
    rows = c_all.shape[0]
    n = w_ada.shape[1]
    tn = 1024
    return pl.pallas_call(
        _ada_kernel,
        grid=(n // tn,),
        in_specs=[pl.BlockSpec((rows, D_MODEL), lambda j: (0, 0)),
                  pl.BlockSpec((D_MODEL, tn), lambda j: (0, j)),
                  pl.BlockSpec((1, tn), lambda j: (0, j))],
        out_specs=pl.BlockSpec((rows, tn), lambda j: (0, j)),
        out_shape=jax.ShapeDtypeStruct((rows, n), F32),
        compiler_params=_cparams("arbitrary"),
        name="ada",
    )(c_all, w_ada, b_ada.reshape(1, n))


def _modmm_kernel(x_ref, sc_ref, sh_ref, w_ref, o_ref, u_ref, *, sigmoid_out):
    @pl.when(pl.program_id(2) == 0)
    def _():
        u_ref[...] = (x_ref[0] * (1.0 + sc_ref[0]) + sh_ref[0]).astype(BF16)

    y = _mm(u_ref[...], w_ref[...])
    if sigmoid_out:
        y = _sigmoid(y)
    o_ref[0] = y.astype(o_ref.dtype)


def _modmm(x, sc, sh, w_bf16, tn, out_dtype=F32, sigmoid_out=False, name="proj"):
    b, t, d = x.shape
    n = w_bf16.shape[1]
    tm = min(1024, t)
    return pl.pallas_call(
        functools.partial(_modmm_kernel, sigmoid_out=sigmoid_out),
        grid=(b, t // tm, n // tn),
        in_specs=[pl.BlockSpec((1, tm, d), lambda bi, i, j: (bi, i, 0)),
                  pl.BlockSpec((1, 1, d), lambda bi, i, j: (bi, 0, 0)),
                  pl.BlockSpec((1, 1, d), lambda bi, i, j: (bi, 0, 0)),
                  pl.BlockSpec((d, tn), lambda bi, i, j: (0, j))],
        out_specs=pl.BlockSpec((1, tm, tn), lambda bi, i, j: (bi, i, j)),
        out_shape=jax.ShapeDtypeStruct((b, t, n), out_dtype),
        scratch_shapes=[pltpu.VMEM((tm, d), BF16)],
        compiler_params=_cparams("arbitrary", "arbitrary", "arbitrary"),
        name=name,
    )(x, sc, sh, w_bf16)


def _sb_kernel(q_ref, k_hbm, v_hbm, kold_hbm, vold_hbm, o_ref, kbuf, vbuf, sem, done_ref, *scratch,
               tq, tk, past, nl, split):
    accs, cars = scratch[:2 * nl], scratch[2 * nl:]
    bi, g, qi = pl.program_id(0), pl.program_id(1), pl.program_id(2)
    width = nl * LANES
    n_g, n_q = pl.num_programs(1), pl.num_programs(2)
    step = (bi * n_g + g) * n_q + qi
    n_steps = pl.num_programs(0) * n_g * n_q
    top_of = lambda q_idx: (past + (q_idx + 1) * tq - 2) // tk
    top = top_of(qi)

    def kv_copies(kb, slot, b_idx=bi, g_idx=g, newest=False):
        lanes = pl.ds(g_idx * width, width)
        if split and newest:
            src, rows = (b_idx, pl.ds(0, tq), lanes), pl.ds(0, tq)
            return (pltpu.make_async_copy(k_hbm.at[src], kbuf.at[slot, rows], sem.at[0, slot]),
                    pltpu.make_async_copy(v_hbm.at[src], vbuf.at[slot, rows], sem.at[1, slot]))
        k_src, v_src = (kold_hbm, vold_hbm) if split else (k_hbm, v_hbm)
        src = (b_idx, pl.ds(kb * tk, tk), lanes)
        return (pltpu.make_async_copy(k_src.at[src], kbuf.at[slot], sem.at[0, slot]),
                pltpu.make_async_copy(v_src.at[src], vbuf.at[slot], sem.at[1, slot]))

    @pl.when(step == 0)
    def _():
        if split and tq < tk:
            for buf in (kbuf, vbuf):
                buf[:, tq:, :] = jnp.zeros((2, tk - tq, width), F32)
        for cp in kv_copies(top, 0, newest=True):
            cp.start()

    for ref in scratch:
        ref[...] = jnp.zeros_like(ref)
    done_ref[0] = 0

    later = (lax.broadcasted_iota(I32, (tk, tk), 0) > lax.broadcasted_iota(I32, (tk, tk), 1)).astype(BF16)
    lane = lax.broadcasted_iota(I32, (tq, LANES), 1)

    def sweep(kb, slot, masked):
        if masked:
            pos = past + qi * tq + lax.broadcasted_iota(I32, (tq, tk), 0)
            kpos = kb * tk + lax.broadcasted_iota(I32, (tq, tk), 1)
            mask = kpos < pos
        for l in range(nl):
            cols = slice(l * LANES, (l + 1) * LANES)
            q = q_ref[0, :, cols] * (HEAD_DIM ** -0.5 * LOG2E)
            kbf = kbuf[slot, :, cols].astype(BF16)
            vbf = vbuf[slot, :, cols].astype(BF16)
            for hh in range(2):
                head = (lane < HEAD_DIM) if hh == 0 else (lane >= HEAD_DIM)
                z = _mm(jnp.where(head, q, 0.0).astype(BF16), kbf, NT)
                neg_abs = lax.bitcast_convert_type(lax.bitcast_convert_type(z, jnp.uint32) | jnp.uint32(0x80000000), F32)
                soft = jnp.log2(1.0 + jnp.exp2(neg_abs))
                log_beta = jnp.minimum(z, 0.0) - soft
                log_keep = log_beta - z
                if masked:
                    log_keep = jnp.where(mask, log_keep, 0.0)
                car = cars[2 * l + hh]
                after = _mm(log_keep.astype(BF16), later) + car[...]
                w = jnp.exp2(log_beta + after)
                if masked:
                    w = jnp.where(mask, w, 0.0)
                accs[2 * l + hh][...] += _mm(w.astype(BF16), vbf)
                car[...] += jnp.sum(log_keep, axis=1, keepdims=True)
        worst = cars[0][...]
        for car in cars[1:]:
            worst = jnp.maximum(worst, car[...])
        done_ref[0] = (jnp.max(worst) < F32_UNDERFLOW_LOG2).astype(I32)

    def body(kb):
        slot = (top - kb) % 2
        if split:
            @pl.when(kb == top)
            def _():
                for cp in kv_copies(kb, slot, newest=True):
                    cp.wait()

            @pl.when(kb != top)
            def _():
                for cp in kv_copies(kb, slot):
                    cp.wait()
        else:
            for cp in kv_copies(kb, slot):
                cp.wait()

        @pl.when(kb > 0)
        def _():
            for cp in kv_copies(kb - 1, 1 - slot):
                cp.start()

        diag = (kb + 1) * tk - 1 >= past + qi * tq

        @pl.when(diag)
        def _():
            sweep(kb, slot, True)

        @pl.when(jnp.logical_not(diag))
        def _():
            sweep(kb, slot, False)

        return kb - 1

    kb_end = lax.while_loop(lambda kb: (kb >= 0) & (done_ref[0] == 0), body, top)

    @pl.when(kb_end >= 0)
    def _():
        for cp in kv_copies(kb_end, (top - kb_end) % 2):
            cp.wait()

    @pl.when(step + 1 < n_steps)
    def _():
        wrap_q = qi + 1 == n_q
        wrap_g = wrap_q & (g + 1 == n_g)
        q_next = jnp.where(wrap_q, 0, qi + 1)
        g_next = jnp.where(wrap_g, 0, jnp.where(wrap_q, g + 1, g))
        b_next = jnp.where(wrap_g, bi + 1, bi)
        for cp in kv_copies(top_of(q_next), 0, b_next, g_next, newest=True):
            cp.start()

    for l in range(nl):
        o_ref[0, :, l * LANES:(l + 1) * LANES] = jnp.where(lane < HEAD_DIM, accs[2 * l][...], accs[2 * l + 1][...])


def _sb_attention(q, k_new, v_new, past_k, past_v, tq, tk, nl):
    b, t, _ = q.shape
    past = past_k.shape[1]
    width = nl * LANES
    split = past > 0 and past % tk == 0 and t == tq and tq <= tk
    if split:
        operands = (k_new, v_new, past_k, past_v)
    else:
        tail = jnp.zeros((b, -(-(past + t) // tk) * tk - past - t, WIDTH), F32)
        joined = [jnp.concatenate([p for p in (old, new, tail) if p.shape[1] > 0], axis=1)
                  for old, new in ((past_k, k_new), (past_v, v_new))]
        operands = (joined[0], joined[1], joined[0], joined[1])
    q_spec = pl.BlockSpec((1, tq, width), lambda bi, g, qi: (bi, qi, g))
    return pl.pallas_call(
        functools.partial(_sb_kernel, tq=tq, tk=tk, past=past, nl=nl, split=split),
        grid=(b, WIDTH // width, t // tq),
        in_specs=[q_spec] + [pl.BlockSpec(memory_space=pl.ANY)] * 4,
        out_specs=q_spec,
        out_shape=jax.ShapeDtypeStruct((b, t, WIDTH), F32),
        scratch_shapes=[pltpu.VMEM((2, tk, width), F32), pltpu.VMEM((2, tk, width), F32),
                        pltpu.SemaphoreType.DMA((2, 2)), pltpu.SMEM((1,), I32)]
        + [pltpu.VMEM((tq, LANES), F32)] * (2 * nl) + [pltpu.VMEM((tq, 1), F32)] * (2 * nl),
        compiler_params=_cparams("arbitrary", "arbitrary", "arbitrary"),
        name="sb_attention",
    )(q, *operands)


def _head_sum(x, same_head):
    return jnp.concatenate(
        [_dot1(x[:, c * LANES:(c + 1) * LANES], same_head) for c in range(WIDTH // LANES)], axis=1)


def _rw_prep_kernel(m_ref, sh0_ref, mu_ref, w0_ref, w2_ref, a0_ref, a2_ref, g2_ref, kk_ref, ka_ref, rk_ref,
                    wt_ref, ut_ref, rt_ref, mrb_ref, yv_ref, bt_ref, kt_ref, v_ref, bv_ref, g_ref, pend_ref, prev_ref,
                    *, tr):
    @pl.when(pl.program_id(1) == 0)
    def _():
        prev_ref[...] = sh0_ref[0]

    m = m_ref[0]
    row = lax.broadcasted_iota(I32, m.shape, 0)
    m_prev = jnp.where(row == 0, prev_ref[...], pltpu.roll(m, 1, 0))
    prev_ref[...] = m[tr - 1:tr, :]
    ms = m + (m_prev - m) * mu_ref[...]
    r = ms[:, 0:WIDTH]
    k = ms[:, WIDTH:2 * WIDTH]
    v = ms[:, 2 * WIDTH:3 * WIDTH]
    lora_wa = ms[:, 3 * WIDTH:3 * WIDTH + LANES]
    lora_g = ms[:, 3 * WIDTH + LANES:RW_PAD]

    w_log = -_softplus(-(w0_ref[...] + _dot1(jnp.tanh(lora_wa), w2_ref[...]))) - 0.5
    log_decay = -jnp.exp(w_log)
    a = _sigmoid(a0_ref[...] + _dot1(lora_wa, a2_ref[...]))
    g = _dot1(_sigmoid(lora_g), g2_ref[...])

    same_head = (lax.broadcasted_iota(I32, (LANES, LANES), 0) // HEAD_DIM ==
                 lax.broadcasted_iota(I32, (LANES, LANES), 1) // HEAD_DIM).astype(BF16)
    kk = k * kk_ref[...]
    kk = kk / jnp.maximum(jnp.sqrt(_head_sum(kk * kk, same_head)), 1e-12)
    kh = k * (1.0 + (a - 1.0) * ka_ref[...])
    bonus = _head_sum(r * kh * rk_ref[...], same_head)

    c = CHUNK
    ti = lax.broadcasted_iota(I32, (tr, tr), 0)
    si = lax.broadcasted_iota(I32, (tr, tr), 1)
    tri = ((si <= ti) & (si // c == ti // c)).astype(BF16)
    parts = _split3(log_decay)
    cl = _mm(tri, parts[0]) + (_mm(tri, parts[1]) + _mm(tri, parts[2]))
    ec = jnp.exp(cl)
    inv = jnp.exp(-cl)
    rt = r * ec
    kt = kh * inv
    at = -kk * jnp.exp(cl - log_decay)
    bt = kk * a * inv
    rt_ref[0] = rt.astype(BF16)
    kt_ref[0] = kt.astype(BF16)
    bt_ref[0] = bt.astype(BF16)
    v_ref[0] = v.astype(BF16)
    bv_ref[0] = bonus * v
    g_ref[0] = g
    for ch in range(tr // c):
        pend_ref[0, ch] = ec[(ch + 1) * c - 1:(ch + 1) * c, :]

    lane = lax.broadcasted_iota(I32, (c, LANES), 1)
    first = lane < HEAD_DIM
    t_row = lax.broadcasted_iota(I32, (c, LANES), 0)
    s_col = lane % HEAD_DIM
    strict = s_col < t_row
    incl = s_col <= t_row
    eye2 = (s_col == t_row).astype(F32)

    def stack2(x):
        return jnp.concatenate([jnp.where(first, x, 0.0), jnp.where(first, 0.0, x)], axis=0)

    tiles = [(slice(ch * c, (ch + 1) * c), slice(hp * LANES, (hp + 1) * LANES))
             for ch in range(tr // c) for hp in range(WIDTH // LANES)]
    rows2 = lambda top, bottom: jnp.concatenate([top, bottom], axis=0)
    a_t = [at[tl] for tl in tiles]
    l_ab, g_v = [], []
    for i, tl in enumerate(tiles):
        b_s, k_s, v_s = stack2(bt[tl]), stack2(kt[tl]), stack2(v[tl])
        ar = rows2(a_t[i], rt[tl])
        on_b = _dot1(ar, b_s, NT)
        on_k = _dot1(ar, k_s, NT)
        l_ab.append(jnp.where(strict, on_b[:c], 0.0))
        mrb_ref[(0,) + tl] = jnp.where(incl, on_b[c:], 0.0).astype(BF16)
        on_v = _dot1(rows2(jnp.where(strict, on_k[:c], 0.0), jnp.where(incl, on_k[c:], 0.0)), v_s)
        g_v.append(on_v[:c])
        yv_ref[(0,) + tl] = on_v[c:]
    tinv = [eye2 + l for l in l_ab]
    pw = [_dot1(x, stack2(x)) for x in l_ab]
    for _ in range(4):
        both = [_dot1(rows2(p_, t_), stack2(p_)) for p_, t_ in zip(pw, tinv)]
        tinv = [t_ + r_[c:] for t_, r_ in zip(tinv, both)]
        pw = [r_[:c] for r_ in both]
    tinv = [t_ + _dot1(t_, stack2(p_)) for t_, p_ in zip(tinv, pw)]
    for i, tl in enumerate(tiles):
        wu = _dot1(tinv[i], jnp.concatenate([stack2(a_t[i]), stack2(g_v[i])], axis=1))
        wt_ref[(0,) + tl] = wu[:, :LANES].astype(BF16)
        ut_ref[(0,) + tl] = wu[:, LANES:]


def _rw_prep(m, shift0, rwp):
    b, t, _ = m.shape
    tr = min(2 * CHUNK, t)
    row_spec = pl.BlockSpec((1, tr, WIDTH), lambda bi, i: (bi, i, 0))
    vec = lambda n: pl.BlockSpec((1, n), lambda bi, i: (0, 0))
    mat = lambda r: pl.BlockSpec((r, WIDTH), lambda bi, i: (0, 0))
    out = jax.ShapeDtypeStruct((b, t, WIDTH), F32)
    out_mxu = jax.ShapeDtypeStruct((b, t, WIDTH), BF16)
    return pl.pallas_call(
        functools.partial(_rw_prep_kernel, tr=tr),
        grid=(b, t // tr),
        in_specs=[pl.BlockSpec((1, tr, RW_PAD), lambda bi, i: (bi, i, 0)),
                  pl.BlockSpec((1, 1, RW_PAD), lambda bi, i: (bi, 0, 0)),
                  vec(RW_PAD), vec(WIDTH), mat(LANES), vec(WIDTH), mat(LANES), mat(RW_PAD - 3 * WIDTH - LANES),
                  vec(WIDTH), vec(WIDTH), vec(WIDTH)],
        out_specs=[row_spec] * 10 + [pl.BlockSpec((1, tr // CHUNK, 1, WIDTH), lambda bi, i: (bi, i, 0, 0))],
        out_shape=[out_mxu, out, out_mxu, out_mxu, out, out_mxu, out_mxu, out_mxu, out, out,
                   jax.ShapeDtypeStruct((b, t // CHUNK, 1, WIDTH), F32)],
        scratch_shapes=[pltpu.VMEM((1, RW_PAD), F32)],
        compiler_params=_cparams("arbitrary", "arbitrary"),
        name="rwkv_prep",
    )(m, shift0, rwp["mu"], rwp["w0"], rwp["w2"], rwp["a0"], rwp["a2"], rwp["g2"], rwp["k_k"], rwp["k_a"],
      rwp["r_k"])


def _rw_scan_kernel(wt_ref, ut_ref, rt_ref, mrb_ref, yv_ref, bt_ref, kt_ref, v_ref, bv_ref, g_ref, pend_ref,
                    s0_ref, lnw_ref, lnb_ref, o_ref, sout_ref, *s_refs, n_chunks, npair):
    ci = pl.program_id(2)

    @pl.when(ci == 0)
    def _():
        for pi in range(npair):
            s_refs[pi][...] = s0_ref[0, pi]

    c = CHUNK
    first = lax.broadcasted_iota(I32, (c, LANES), 1) < HEAD_DIM
    vi = lax.broadcasted_iota(I32, (LANES, LANES), 0)
    kj = lax.broadcasted_iota(I32, (LANES, LANES), 1)
    same_head_b = (vi // HEAD_DIM) == (kj // HEAD_DIM)
    same_head = same_head_b.astype(BF16)

    def stack2(x):
        return jnp.concatenate([jnp.where(first, x, 0.0), jnp.where(first, 0.0, x)], axis=0)

    def chunk(j, carry):
        rows = pl.ds(pl.multiple_of(j * c, c), c)
        p_all = pend_ref[0, j]
        pairs = range(npair)
        col = lambda pi: slice(pi * LANES, (pi + 1) * LANES)
        s = [s_refs[pi][...] for pi in pairs]
        u = [_dot1(wt_ref[0, rows, col(pi)], s[pi], NT) + ut_ref[0, rows, col(pi)] for pi in pairs]
        for pi in pairs:
            ds = _dot1(jnp.concatenate([u[pi].astype(BF16), v_ref[0, rows, col(pi)]], axis=0),
                       jnp.concatenate([bt_ref[0, rows, col(pi)], kt_ref[0, rows, col(pi)]], axis=0), TN)
            s_refs[pi][...] = (s[pi] + jnp.where(same_head_b, ds, 0.0)) * p_all[:, col(pi)]
        y = [_dot1(rt_ref[0, rows, col(pi)], s[pi], NT) + (_dot1(mrb_ref[0, rows, col(pi)], stack2(u[pi]))
                                                           + yv_ref[0, rows, col(pi)]) for pi in pairs]
        for pi in pairs:
            mu = _dot1(y[pi], same_head) * (1.0 / HEAD_DIM)
            dlt = y[pi] - mu
            var = _dot1(dlt * dlt, same_head) * (1.0 / HEAD_DIM)
            yn = dlt * lax.rsqrt(var + GN_EPS) * lnw_ref[:, col(pi)] + lnb_ref[:, col(pi)]
            o_ref[0, rows, col(pi)] = (yn + bv_ref[0, rows, col(pi)]) * g_ref[0, rows, col(pi)]
        return carry

    lax.fori_loop(0, n_chunks, chunk, 0)

    @pl.when(ci == pl.num_programs(2) - 1)
    def _():
        for pi in range(npair):
            sout_ref[0, pi] = s_refs[pi][...]


def _rw_scan(prep, s0_bd, lnw, lnb, npair=8):
    b, t, _ = prep[0].shape
    tc = min(512, t)
    n_chunks = tc // CHUNK
    ng = WIDTH // (npair * LANES)
    row_spec = pl.BlockSpec((1, tc, npair * LANES), lambda bi, h, i: (bi, i, h))
    pend_spec = pl.BlockSpec((1, n_chunks, 1, npair * LANES), lambda bi, h, i: (bi, i, 0, h))
    st_spec = pl.BlockSpec((1, npair, LANES, LANES), lambda bi, h, i: (bi, h, 0, 0))
    vec_spec = pl.BlockSpec((1, npair * LANES), lambda bi, h, i: (0, h))
    return pl.pallas_call(
        functools.partial(_rw_scan_kernel, n_chunks=n_chunks, npair=npair),
        grid=(b, ng, t // tc),
        in_specs=[row_spec] * 10 + [pend_spec, st_spec, vec_spec, vec_spec],
        out_specs=[row_spec, st_spec],
        out_shape=[jax.ShapeDtypeStruct((b, t, WIDTH), F32),
                   jax.ShapeDtypeStruct((b, WIDTH // LANES, LANES, LANES), F32)],
        scratch_shapes=[pltpu.VMEM((LANES, LANES), F32)] * npair,
        compiler_params=_cparams("arbitrary", "arbitrary", "arbitrary"),
        name="rwkv_scan",
    )(*prep, s0_bd, lnw, lnb)


def _merge_kernel(oa_ref, ob_ref, sg_ref, x_ref, g1_ref, sc2_ref, sh2_ref, pa_ref, pb_ref, wo_ref, lng_ref, lnb_ref,
                  wr_ref, br_ref, cnt0_ref, h_ref, u2_ref, route_ref, cnt_out_ref, cnt_ref):
    sg = sg_ref[0]
    ya = _mm(oa_ref[0].astype(BF16), pa_ref[...])
    yb = _mm(ob_ref[0].astype(BF16), pb_ref[...])
    merged = sg[:, :D_MODEL].astype(F32) * ya + sg[:, D_MODEL:].astype(F32) * yb
    mix = _mm(merged.astype(BF16), wo_ref[...])
    hin = ALPHA * x_ref[0] + g1_ref[0] * mix
    mu = jnp.mean(hin, axis=-1, keepdims=True)
    dlt = hin - mu
    var = jnp.mean(dlt * dlt, axis=-1, keepdims=True)
    h = dlt * lax.rsqrt(var + LN_EPS) * lng_ref[...] + lnb_ref[...]
    h_ref[0] = h
    u2 = h * (1.0 + sc2_ref[0]) + sh2_ref[0]
    u2_ref[0] = u2

    logits = _dot3(u2, wr_ref[...]) + br_ref[...]
    lane_i = lax.broadcasted_iota(I32, logits.shape, 1)
    lane = lane_i.astype(F32)
    neg = -jnp.inf
    first_at = lambda vals, top: jnp.min(jnp.where(vals == top, lane, float(LANES)), axis=1, keepdims=True)
    lg = jnp.where(lane_i < N_GROUPS, logits, neg)
    mg = jnp.max(lg, axis=1, keepdims=True)
    g_idx = first_at(lg, mg)
    p_g = 1.0 / jnp.sum(jnp.exp(lg - mg), axis=1, keepdims=True)
    lo_lane = N_GROUPS + PER_GROUP * g_idx
    le = jnp.where((lane >= lo_lane) & (lane < lo_lane + PER_GROUP), logits, neg)
    m1 = jnp.max(le, axis=1, keepdims=True)
    i1 = first_at(le, m1)
    le2 = jnp.where(lane == i1, neg, le)
    m2 = jnp.max(le2, axis=1, keepdims=True)
    i2 = first_at(le2, m2)
    ratio = jnp.exp(m2 - m1)
    w1 = p_g / (1.0 + ratio)
    w2 = p_g * ratio / (1.0 + ratio)
    @pl.when((pl.program_id(0) == 0) & (pl.program_id(1) == 0))
    def _():
        cnt_ref[...] = cnt0_ref[...]

    tm = logits.shape[0]
    chosen = ((lane == i1) | (lane == i2)).astype(BF16)
    earlier = (lax.broadcasted_iota(I32, (tm, tm), 1) < lax.broadcasted_iota(I32, (tm, tm), 0)).astype(BF16)
    before = _mm(earlier, chosen) + cnt_ref[...]
    rank1 = jnp.sum(jnp.where(lane == i1, before, 0.0), axis=1, keepdims=True)
    rank2 = jnp.sum(jnp.where(lane == i2, before, 0.0), axis=1, keepdims=True)
    cnt_ref[...] = before[tm - 1:tm, :] + chosen[tm - 1:tm, :].astype(F32)
    cnt_out_ref[...] = cnt_ref[...]

    route = jnp.where(lane_i == 0, i1 - N_GROUPS, 0.0)
    for k, val in enumerate((i2 - N_GROUPS, w1, w2, rank1, rank2), start=1):
        route = jnp.where(lane_i == k, val, route)
    route_ref[0] = route


def _merge(o_a, o_b, sg, x, g1, sc2, sh2, wts, cnt0):
    b, t, d = x.shape
    tm = min(256, t)
    const = lambda shape: pl.BlockSpec(shape, lambda bi, i: (0,) * len(shape), pipeline_mode=pl.Buffered(1))
    seq = pl.BlockSpec((1, 1, d), lambda bi, i: (bi, 0, 0))
    row = lambda n: pl.BlockSpec((1, tm, n), lambda bi, i: (bi, i, 0))
    return pl.pallas_call(
        _merge_kernel,
        grid=(b, t // tm),
        in_specs=[row(WIDTH), row(WIDTH), row(2 * d), row(d), seq, seq, seq,
                  const((WIDTH, d)), const((WIDTH, d)), const((d, d)), const((1, d)), const((1, d)),
                  const((d, LANES)), const((1, LANES)), const((1, LANES))],
        out_specs=[row(d), row(d), row(LANES), pl.BlockSpec((1, LANES), lambda bi, i: (0, 0))],
        out_shape=[jax.ShapeDtypeStruct((b, t, d), F32), jax.ShapeDtypeStruct((b, t, d), F32),
                   jax.ShapeDtypeStruct((b, t, LANES), F32), jax.ShapeDtypeStruct((1, LANES), F32)],
        scratch_shapes=[pltpu.VMEM((1, LANES), F32)],
        compiler_params=_cparams("arbitrary", "arbitrary"),
        name="merge_route",
    )(o_a, o_b, sg, x, g1, sc2, sh2, wts["p_a"], wts["p_b"], wts["w_out"], wts["ln1_g"], wts["ln1_b"],
      wts["w_route"], wts["b_route"], cnt0)


def _moe_kernel(be_ref, nused_ref, tok_ref, x_hbm, wg_ref, wu_ref, wd_ref, y_ref, xbuf, sem, *, tb):
    i = pl.program_id(0)
    n_used = nused_ref[0]
    last = n_used - 1

    def row_copy(block, r, slot):
        return pltpu.make_async_copy(x_hbm.at[pl.ds(tok_ref[block * tb + r], 1)], xbuf.at[slot, pl.ds(r, 1)],
                                     sem.at[slot])

    def wait_slot(slot):
        pltpu.make_async_copy(x_hbm.at[pl.ds(0, tb)], xbuf.at[slot], sem.at[slot]).wait()

    @pl.when((i == 0) & (n_used > 0))
    def _():
        def start(r, carry):
            row_copy(0, r, 0).start()
            return carry

        lax.fori_loop(0, tb, start, 0, unroll=8)

    @pl.when(i < n_used)
    def _():
        slot = i % 2
        wait_slot(slot)
        x = xbuf[slot].astype(BF16)
        nxt = jnp.minimum(i + 1, last)
        for r in range(tb):
            row_copy(nxt, r, 1 - slot).start()
        hg = _mm(x, wg_ref[0])
        hu = _mm(x, wu_ref[0])
        hid = (hg * _sigmoid(hg) * hu).astype(BF16)
        y_ref[...] = _mm(hid, wd_ref[0])

    @pl.when(i == last)
    def _():
        wait_slot(1 - i % 2)

    @pl.when(i >= n_used)
    def _():
        y_ref[...] = jnp.zeros_like(y_ref)


def _moe(block_expert, n_used, slot_token, x_all, wg, wu, wd, tb):
    n_blocks = block_expert.shape[0]
    d = x_all.shape[1]
    grid_spec = pltpu.PrefetchScalarGridSpec(
        num_scalar_prefetch=3,
        grid=(n_blocks,),
        in_specs=[pl.BlockSpec(memory_space=pl.ANY),
                  pl.BlockSpec((1, d, D_EXPERT), lambda i, be, nu, tok: (be[i], 0, 0), pipeline_mode=pl.Buffered(1)),
                  pl.BlockSpec((1, d, D_EXPERT), lambda i, be, nu, tok: (be[i], 0, 0), pipeline_mode=pl.Buffered(1)),
                  pl.BlockSpec((1, D_EXPERT, d), lambda i, be, nu, tok: (be[i], 0, 0), pipeline_mode=pl.Buffered(1))],
        out_specs=pl.BlockSpec((tb, d), lambda i, be, nu, tok: (i, 0)),
        scratch_shapes=[pltpu.VMEM((2, tb, d), F32), pltpu.SemaphoreType.DMA((2,))],
    )
    return pl.pallas_call(
        functools.partial(_moe_kernel, tb=tb),
        grid_spec=grid_spec,
        out_shape=jax.ShapeDtypeStruct((n_blocks * tb, d), F32),
        compiler_params=_cparams("arbitrary"),
        name="moe_experts",
    )(block_expert, n_used, slot_token, x_all, wg, wu, wd)


def _final_kernel(dest_ref, y_hbm, h_ref, rw_ref, g2_ref, lng_ref, lnb_ref, o_ref, ybuf, sem, *, tm, per_seq):
    step = pl.program_id(0) * per_seq + pl.program_id(1)
    n_steps = pl.num_programs(0) * per_seq

    def row_copy(blk, r, k, slot):
        return pltpu.make_async_copy(y_hbm.at[pl.ds(dest_ref[2 * (blk * tm + r) + k], 1)],
                                     ybuf.at[slot, k, pl.ds(r, 1)], sem.at[slot])

    def wait_slot(slot):
        for k in range(2):
            pltpu.make_async_copy(y_hbm.at[pl.ds(0, tm)], ybuf.at[slot, k], sem.at[slot]).wait()

    @pl.when(step == 0)
    def _():
        def start(r, carry):
            for k in range(2):
                row_copy(0, r, k, 0).start()
            return carry

        lax.fori_loop(0, tm, start, 0, unroll=4)

    slot = step % 2
    wait_slot(slot)
    nxt = jnp.minimum(step + 1, n_steps - 1)
    for r in range(tm):
        for k in range(2):
            row_copy(nxt, r, k, 1 - slot).start()
    rw = rw_ref[0]
    moe = rw[:, 2:3] * ybuf[slot, 0] + rw[:, 3:4] * ybuf[slot, 1]
    xin = ALPHA * h_ref[0] + g2_ref[0] * moe
    mu = jnp.mean(xin, axis=-1, keepdims=True)
    dlt = xin - mu
    var = jnp.mean(dlt * dlt, axis=-1, keepdims=True)
    o_ref[0] = dlt * lax.rsqrt(var + LN_EPS) * lng_ref[...] + lnb_ref[...]

    @pl.when(step == n_steps - 1)
    def _():
        wait_slot(1 - slot)


def _final(dest, y_sorted, h, route, g2, ln2_g, ln2_b):
    b, t, d = h.shape
    tm = min(256, t)
    per_seq = t // tm
    grid_spec = pltpu.PrefetchScalarGridSpec(
        num_scalar_prefetch=1,
        grid=(b, per_seq),
        in_specs=[pl.BlockSpec(memory_space=pl.ANY),
                  pl.BlockSpec((1, tm, d), lambda bi, i, de: (bi, i, 0)),
                  pl.BlockSpec((1, tm, LANES), lambda bi, i, de: (bi, i, 0)),
                  pl.BlockSpec((1, 1, d), lambda bi, i, de: (bi, 0, 0)),
                  pl.BlockSpec((1, d), lambda bi, i, de: (0, 0)),
                  pl.BlockSpec((1, d), lambda bi, i, de: (0, 0))],
        out_specs=pl.BlockSpec((1, tm, d), lambda bi, i, de: (bi, i, 0)),
        scratch_shapes=[pltpu.VMEM((2, 2, tm, d), F32), pltpu.SemaphoreType.DMA((2,))],
    )
    return pl.pallas_call(
        functools.partial(_final_kernel, tm=tm, per_seq=per_seq),
        grid_spec=grid_spec,
        out_shape=jax.ShapeDtypeStruct((b, t, d), F32),
        compiler_params=_cparams("arbitrary", "arbitrary"),
        name="combine_ln2",
    )(dest, y_sorted, h, route, g2, ln2_g, ln2_b)


def _pad_rows(a, rows):
    return jnp.pad(a, ((0, rows - a.shape[0]), (0, 0)))


def _mixers(x, ada, past_k, past_v, s0, shift0, wts, sb_tiles, cnt0):
    b, t, d = x.shape
    sh1, sc1, g1, sh2, sc2, g2 = [a[:, None, :] for a in jnp.split(ada, 6, axis=-1)]
    q = _modmm(x, sc1, sh1, wts["w_q"], 1024, name="proj_q")
    k = _modmm(x, sc1, sh1, wts["w_k"], 1024, name="proj_k")
    v = _modmm(x, sc1, sh1, wts["w_v"], 1024, name="proj_v")
    m = _modmm(x, sc1, sh1, wts["w_m"], RW_PAD // 3, name="proj_rwkv")
    sg = _modmm(x, sc1, sh1, wts["w_g"], 1024, out_dtype=BF16, sigmoid_out=True, name="proj_gates")

    tq, tk, nl = sb_tiles
    past = past_k.shape[1]
    o_a = _sb_attention(q, k, v, past_k.reshape(b, past, WIDTH), past_v.reshape(b, past, WIDTH), tq, tk, nl)

    shift0_p = jnp.pad(shift0, ((0, 0), (0, 0), (0, RW_PAD - RW_COLS)))
    prep = _rw_prep(m, shift0_p, wts["rw"])
    nhp = WIDTH // LANES
    s0_pairs = s0.reshape(b, nhp, 2, HEAD_DIM, HEAD_DIM)
    zero4 = ((0, 0), (0, 0))
    s0_bd = (jnp.pad(s0_pairs[:, :, 0], zero4 + ((0, HEAD_DIM), (0, HEAD_DIM)))
             + jnp.pad(s0_pairs[:, :, 1], zero4 + ((HEAD_DIM, 0), (HEAD_DIM, 0))))
    o_b, s_bd = _rw_scan(prep, s0_bd, wts["rw"]["lnx_w"], wts["rw"]["lnx_b"])
    s_new = jnp.stack([s_bd[:, :, :HEAD_DIM, :HEAD_DIM], s_bd[:, :, HEAD_DIM:, HEAD_DIM:]], axis=2)
    s_new = s_new.reshape(b, 2 * nhp, HEAD_DIM, HEAD_DIM)
    shift_new = m[:, t - 1:t, :RW_COLS]

    h, u2, route, cnt = _merge(o_a, o_b, sg, x, g1, sc2, sh2, wts, cnt0)
    return h, u2, route, cnt, g2, k, v, s_new, shift_new


def kernel(x_prompt, x_sample, cache_sb_k, cache_sb_v, state_rwkv, state_rwkv_shift, c_prompt, c_sample, w_ada, b_ada, w_in, tokshift_mu, rw_w0, rw_w2, rw_a0, rw_a2, rw_g2, rw_k_k, rw_k_a, rw_r_k, rw_lnx_w, rw_lnx_b, w_branch_a, w_branch_b, w_out, ln1_g, ln1_b, w_router_group, b_router_group, w_router_expert, b_router_expert, w_exp_gate, w_exp_up, w_exp_down, ln2_g, ln2_b):
    depth = w_ada.shape[0]
    assert depth == 1
    bp, tp, d = x_prompt.shape
    bs, ts, _ = x_sample.shape
    l = 0

    w_in_l = w_in[l]
    w_rw = w_in_l[:, 3 * WIDTH:3 * WIDTH + RW_COLS]
    row_vec = lambda a: a.reshape(1, -1)
    pad_cols = lambda a, n: jnp.pad(a, ((0, 0), (0, n - a.shape[1])))
    n_lg = RW_PAD - 3 * WIDTH - LANES
    wts = {
        "w_q": w_in_l[:, 0:WIDTH].astype(BF16),
        "w_k": w_in_l[:, WIDTH:2 * WIDTH].astype(BF16),
        "w_v": w_in_l[:, 2 * WIDTH:3 * WIDTH].astype(BF16),
        "w_m": pad_cols(w_rw, RW_PAD).astype(BF16),
        "w_g": w_in_l[:, 3 * WIDTH + RW_COLS:].astype(BF16),
        "p_a": w_branch_a[l].astype(BF16),
        "p_b": w_branch_b[l].astype(BF16),
        "w_out": w_out[l].astype(BF16),
        "ln1_g": row_vec(ln1_g[l]),
        "ln1_b": row_vec(ln1_b[l]),
        "w_route": pad_cols(jnp.concatenate([w_router_group[l], w_router_expert[l]], axis=1), LANES),
        "b_route": pad_cols(row_vec(jnp.concatenate([b_router_group[l], b_router_expert[l]])), LANES),
        "rw": {
            "mu": pad_cols(row_vec(tokshift_mu[l]), RW_PAD),
            "w0": row_vec(rw_w0[l]),
            "w2": _pad_rows(rw_w2[l], LANES),
            "a0": row_vec(rw_a0[l]),
            "a2": jnp.concatenate([jnp.zeros((LORA_W, WIDTH), F32), rw_a2[l]], axis=0),
            "g2": _pad_rows(rw_g2[l], n_lg),
            "k_k": row_vec(rw_k_k[l]),
            "k_a": row_vec(rw_k_a[l]),
            "r_k": row_vec(rw_r_k[l]),
            "lnx_w": row_vec(rw_lnx_w[l]),
            "lnx_b": row_vec(rw_lnx_b[l]),
        },
    }

    rows = -(-(bp + bs) // 8) * 8
    c_all = _pad_rows(jnp.concatenate([c_prompt, c_sample], axis=0), rows)
    ada = _ada(c_all, w_ada[l], b_ada[l])

    empty = jnp.zeros((bp, 0, WIDTH), F32)
    zero_state = jnp.zeros((bp, WIDTH // HEAD_DIM, HEAD_DIM, HEAD_DIM), F32)
    zero_shift = jnp.zeros((bp, 1, RW_COLS), F32)
    hp, u2p, route_p, cnt_p, g2p, kp, vp, sp_, shp = _mixers(
        x_prompt, ada[:bp], empty, empty, zero_state, zero_shift, wts, (min(256, tp), 256, 4),
        jnp.zeros((1, LANES), F32))
    hs, u2s, route_s, cnt_all, g2s, ks, vs, ss_, shs = _mixers(
        x_sample, ada[bp:bp + bs], cache_sb_k[l], cache_sb_v[l], state_rwkv[l], state_rwkv_shift[l], wts,
        (min(256, ts), 256, 4), cnt_p)

    n_p, n_s = bp * tp, bs * ts
    n_tok = n_p + n_s
    tb = 512
    route_all = jnp.concatenate([route_p.reshape(n_p, LANES), route_s.reshape(n_s, LANES)], axis=0)
    flat_e = route_all[:, 0:2].astype(I32).reshape(n_tok * 2)
    rank = route_all[:, 4:6].astype(I32).reshape(n_tok * 2)
    counts = cnt_all[0, N_GROUPS:N_GROUPS + N_EXPERTS].astype(I32)
    padded = (counts + tb - 1) // tb * tb
    pend = jnp.cumsum(padded)
    pstart = pend - padded
    dest = (pstart[flat_e] + rank).astype(I32)
    n_blocks = (2 * n_tok + N_EXPERTS * (tb - 1)) // tb + 1
    slot_token = jnp.zeros((n_blocks * tb,), I32).at[dest].set(
        jnp.arange(2 * n_tok, dtype=I32) // 2, unique_indices=True, mode="promise_in_bounds")
    block_start = jnp.arange(n_blocks, dtype=I32) * tb
    block_expert = jnp.minimum(jnp.sum((pend[None, :] <= block_start[:, None]).astype(I32), axis=1), N_EXPERTS - 1)
    n_used = (pend[-1] // tb).astype(I32).reshape(1)

    u2_all = jnp.concatenate([u2p.reshape(n_p, d), u2s.reshape(n_s, d)], axis=0)
    y_sorted = _moe(block_expert, n_used, slot_token, u2_all, w_exp_gate[l].astype(BF16),
                    w_exp_up[l].astype(BF16), w_exp_down[l].astype(BF16), tb)

    ln2g, ln2b = row_vec(ln2_g[l]), row_vec(ln2_b[l])
    y_prompt = _final(dest[:2 * n_p], y_sorted, hp, route_p, g2p, ln2g, ln2b)
    y_sample = _final(dest[2 * n_p:], y_sorted, hs, route_s, g2s, ln2g, ln2b)

    heads = lambda a: a.reshape(1, a.shape[0], a.shape[1], WIDTH // HEAD_DIM, HEAD_DIM)
    return (y_prompt, y_sample, heads(kp), heads(vp), sp_[None], shp[None],
            heads(ks), heads(vs), ss_[None], shs[None])
```

```python
import functools

import jax
import jax.numpy as jnp
import numpy as np
from jax import lax
from jax.experimental import pallas as pl
from jax.experimental.pallas import tpu as pltpu

F32 = jnp.float32
BF16 = jnp.bfloat16
I32 = jnp.int32

D_MODEL = 2048
HEAD_DIM = 64
WIDTH = 1024
LORA_W = 64
LORA_A = 64
LORA_G = 160
RW_COLS = 3 * WIDTH + LORA_W + LORA_A + LORA_G
RW_PAD = 3456
N_GROUPS = 4
PER_GROUP = 8
N_EXPERTS = 32
D_EXPERT = 1024
LN_EPS = 1e-5
GN_EPS = 64e-5
ALPHA = 2.0 ** 0.25
LOG2E = 1.4426950408889634
F32_UNDERFLOW_LOG2 = -150.0
LANES = 128
CHUNK = 64
VMEM_LIMIT = 56 * 1024 * 1024

NN = (((1,), (0,)), ((), ()))
NT = (((1,), (1,)), ((), ()))
TN = (((0,), (0,)), ((), ()))


def _cparams(*sem):
    return pltpu.CompilerParams(dimension_semantics=sem, vmem_limit_bytes=VMEM_LIMIT)


def _mm(a, b, dims=NN):
    return lax.dot_general(a, b, dims, preferred_element_type=F32)


def _split(x):
    hi = x.astype(BF16)
    lo = (x - hi.astype(F32)).astype(BF16)
    return hi, lo


def _split3(x):
    hi = x.astype(BF16)
    r1 = x - hi.astype(F32)
    mid = r1.astype(BF16)
    lo = (r1 - mid.astype(F32)).astype(BF16)
    return hi, mid, lo


def _dot3(a, b, dims=NN):
    ah, al = _split(a)
    bh, bl = _split(b)
    return _mm(ah, bh, dims) + (_mm(al, bh, dims) + _mm(ah, bl, dims))


def _dot1(a, b, dims=NN):
    return _mm(a.astype(BF16), b.astype(BF16), dims)


def _dot_exact_rhs(a, b_bf16, terms=2):
    parts = _split(a) if terms == 2 else _split3(a)
    out = _mm(parts[0], b_bf16)
    for p in parts[1:]:
        out = out + _mm(p, b_bf16)
    return out


def _softplus(z):
    return jnp.maximum(z, 0.0) + jnp.log(1.0 + jnp.exp(-jnp.abs(z)))


def _sigmoid(z):
    return 1.0 / (1.0 + jnp.exp(-z))


def _ada_kernel(c_ref, w_ref, b_ref, o_ref):
    c = c_ref[...]
    o_ref[...] = _dot3(c * _sigmoid(c), w_ref[...]) + b_ref[...]


def _ada(c_all, w_ada, b_ada):
    rows = c_all.shape[0]
    n = w_ada.shape[1]
    tn = 1024
    return pl.pallas_call(
        _ada_kernel,
        grid=(n // tn,),
        in_specs=[pl.BlockSpec((rows, D_MODEL), lambda j: (0, 0)),
                  pl.BlockSpec((D_MODEL, tn), lambda j: (0, j)),
                  pl.BlockSpec((1, tn), lambda j: (0, j))],
        out_specs=pl.BlockSpec((rows, tn), lambda j: (0, j)),
        out_shape=jax.ShapeDtypeStruct((rows, n), F32),
        compiler_params=_cparams("arbitrary"),
        name="ada",
    )(c_all, w_ada, b_ada.reshape(1, n))


def _modmm_kernel(x_ref, sc_ref, sh_ref, w_ref, o_ref, u_ref, *, sigmoid_out):
    @pl.when(pl.program_id(2) == 0)
    def _():
        u_ref[...] = (x_ref[0] * (1.0 + sc_ref[0]) + sh_ref[0]).astype(BF16)

    y = _mm(u_ref[...], w_ref[...])
    if sigmoid_out:
        y = _sigmoid(y)
    o_ref[0] = y.astype(o_ref.dtype)


def _modmm(x, sc, sh, w_bf16, tn, out_dtype=F32, sigmoid_out=False, name="proj"):
    b, t, d = x.shape
    n = w_bf16.shape[1]
    tm = min(1024, t)
    return pl.pallas_call(
        functools.partial(_modmm_kernel, sigmoid_out=sigmoid_out),
        grid=(b, t // tm, n // tn),
        in_specs=[pl.BlockSpec((1, tm, d), lambda bi, i, j: (bi, i, 0)),
                  pl.BlockSpec((1, 1, d), lambda bi, i, j: (bi, 0, 0)),
                  pl.BlockSpec((1, 1, d), lambda bi, i, j: (bi, 0, 0)),
                  pl.BlockSpec((d, tn), lambda bi, i, j: (0, j))],
        out_specs=pl.BlockSpec((1, tm, tn), lambda bi, i, j: (bi, i, j)),
        out_shape=jax.ShapeDtypeStruct((b, t, n), out_dtype),
        scratch_shapes=[pltpu.VMEM((tm, d), BF16)],
        compiler_params=_cparams("arbitrary", "arbitrary", "arbitrary"),
        name=name,
    )(x, sc, sh, w_bf16)


def _sb_kernel(q_ref, k_hbm, v_hbm, kold_hbm, vold_hbm, o_ref, kbuf, vbuf, sem, done_ref, *scratch,
               tq, tk, past, nl, split):
    accs, cars = scratch[:2 * nl], scratch[2 * nl:]
    bi, g, qi = pl.program_id(0), pl.program_id(1), pl.program_id(2)
    width = nl * LANES
    n_g, n_q = pl.num_programs(1), pl.num_programs(2)
    step = (bi * n_g + g) * n_q + qi
    n_steps = pl.num_programs(0) * n_g * n_q
    top_of = lambda q_idx: (past + (q_idx + 1) * tq - 2) // tk
    top = top_of(qi)

    def kv_copies(kb, slot, b_idx=bi, g_idx=g, newest=False):
        lanes = pl.ds(g_idx * width, width)
        if split and newest:
            src, rows = (b_idx, pl.ds(0, tq), lanes), pl.ds(0, tq)
            return (pltpu.make_async_copy(k_hbm.at[src], kbuf.at[slot, rows], sem.at[0, slot]),
                    pltpu.make_async_copy(v_hbm.at[src], vbuf.at[slot, rows], sem.at[1, slot]))
        k_src, v_src = (kold_hbm, vold_hbm) if split else (k_hbm, v_hbm)
        src = (b_idx, pl.ds(kb * tk, tk), lanes)
        return (pltpu.make_async_copy(k_src.at[src], kbuf.at[slot], sem.at[0, slot]),
                pltpu.make_async_copy(v_src.at[src], vbuf.at[slot], sem.at[1, slot]))

    @pl.when(step == 0)
    def _():
        if split and tq < tk:
            for buf in (kbuf, vbuf):
                buf[:, tq:, :] = jnp.zeros((2, tk - tq, width), F32)
        for cp in kv_copies(top, 0, newest=True):
            cp.start()

    for ref in scratch:
        ref[...] = jnp.zeros_like(ref)
    done_ref[0] = 0

    later = (lax.broadcasted_iota(I32, (tk, tk), 0) > lax.broadcasted_iota(I32, (tk, tk), 1)).astype(BF16)
    lane = lax.broadcasted_iota(I32, (tq, LANES), 1)

    def sweep(kb, slot, masked):
        if masked:
            pos = past + qi * tq + lax.broadcasted_iota(I32, (tq, tk), 0)
            kpos = kb * tk + lax.broadcasted_iota(I32, (tq, tk), 1)
            mask = kpos < pos
        for l in range(nl):
            cols = slice(l * LANES, (l + 1) * LANES)
            q = q_ref[0, :, cols] * (HEAD_DIM ** -0.5 * LOG2E)
            kbf = kbuf[slot, :, cols].astype(BF16)
            vbf = vbuf[slot, :, cols].astype(BF16)
            for hh in range(2):
                head = (lane < HEAD_DIM) if hh == 0 else (lane >= HEAD_DIM)
                z = _mm(jnp.where(head, q, 0.0).astype(BF16), kbf, NT)
                neg_abs = lax.bitcast_convert_type(lax.bitcast_convert_type(z, jnp.uint32) | jnp.uint32(0x80000000), F32)
                soft = jnp.log2(1.0 + jnp.exp2(neg_abs))
                log_beta = jnp.minimum(z, 0.0) - soft
                log_keep = log_beta - z
                if masked:
                    log_keep = jnp.where(mask, log_keep, 0.0)
                car = cars[2 * l + hh]
                after = _mm(log_keep.astype(BF16), later) + car[...]
                w = jnp.exp2(log_beta + after)
                if masked:
                    w = jnp.where(mask, w, 0.0)
                accs[2 * l + hh][...] += _mm(w.astype(BF16), vbf)
                car[...] += jnp.sum(log_keep, axis=1, keepdims=True)
        worst = cars[0][...]
        for car in cars[1:]:
            worst = jnp.maximum(worst, car[...])
        done_ref[0] = (jnp.max(worst) < F32_UNDERFLOW_LOG2).astype(I32)

    def body(kb):
        slot = (top - kb) % 2
        if split:
            @pl.when(kb == top)
            def _():
                for cp in kv_copies(kb, slot, newest=True):
                    cp.wait()

            @pl.when(kb != top)
            def _():
                for cp in kv_copies(kb, slot):
                    cp.wait()
        else:
            for cp in kv_copies(kb, slot):
                cp.wait()

        @pl.when(kb > 0)
        def _():
            for cp in kv_copies(kb - 1, 1 - slot):
                cp.start()

        diag = (kb + 1) * tk - 1 >= past + qi * tq

        @pl.when(diag)
        def _():
            sweep(kb, slot, True)

        @pl.when(jnp.logical_not(diag))
        def _():
            sweep(kb, slot, False)

        return kb - 1

    kb_end = lax.while_loop(lambda kb: (kb >= 0) & (done_ref[0] == 0), body, top)

    @pl.when(kb_end >= 0)
    def _():
        for cp in kv_copies(kb_end, (top - kb_end) % 2):
            cp.wait()

    @pl.when(step + 1 < n_steps)
    def _():
        wrap_q = qi + 1 == n_q
        wrap_g = wrap_q & (g + 1 == n_g)
        q_next = jnp.where(wrap_q, 0, qi + 1)
        g_next = jnp.where(wrap_g, 0, jnp.where(wrap_q, g + 1, g))
        b_next = jnp.where(wrap_g, bi + 1, bi)
        for cp in kv_copies(top_of(q_next), 0, b_next, g_next, newest=True):
            cp.start()

    for l in range(nl):
        o_ref[0, :, l * LANES:(l + 1) * LANES] = jnp.where(lane < HEAD_DIM, accs[2 * l][...], accs[2 * l + 1][...])


def _sb_attention(q, k_new, v_new, past_k, past_v, tq, tk, nl):
    b, t, _ = q.shape
    past = past_k.shape[1]
    width = nl * LANES
    split = past > 0 and past % tk == 0 and t == tq and tq <= tk
    if split:
        operands = (k_new, v_new, past_k, past_v)
    else:
        tail = jnp.zeros((b, -(-(past + t) // tk) * tk - past - t, WIDTH), F32)
        joined = [jnp.concatenate([p for p in (old, new, tail) if p.shape[1] > 0], axis=1)
                  for old, new in ((past_k, k_new), (past_v, v_new))]
        operands = (joined[0], joined[1], joined[0], joined[1])
    q_spec = pl.BlockSpec((1, tq, width), lambda bi, g, qi: (bi, qi, g))
    return pl.pallas_call(
        functools.partial(_sb_kernel, tq=tq, tk=tk, past=past, nl=nl, split=split),
        grid=(b, WIDTH // width, t // tq),
        in_specs=[q_spec] + [pl.BlockSpec(memory_space=pl.ANY)] * 4,
        out_specs=q_spec,
        out_shape=jax.ShapeDtypeStruct((b, t, WIDTH), F32),
        scratch_shapes=[pltpu.VMEM((2, tk, width), F32), pltpu.VMEM((2, tk, width), F32),
                        pltpu.SemaphoreType.DMA((2, 2)), pltpu.SMEM((1,), I32)]
        + [pltpu.VMEM((tq, LANES), F32)] * (2 * nl) + [pltpu.VMEM((tq, 1), F32)] * (2 * nl),
        compiler_params=_cparams("arbitrary", "arbitrary", "arbitrary"),
        name="sb_attention",
    )(q, *operands)


def _head_sum(x, same_head):
    return jnp.concatenate(
        [_dot1(x[:, c * LANES:(c + 1) * LANES], same_head) for c in range(WIDTH // LANES)], axis=1)


def _rw_prep_kernel(m_ref, sh0_ref, mu_ref, w0_ref, w2_ref, a0_ref, a2_ref, g2_ref, kk_ref, ka_ref, rk_ref,
                    wt_ref, ut_ref, rt_ref, mrb_ref, yv_ref, bt_ref, kt_ref, v_ref, bv_ref, g_ref, pend_ref, prev_ref,
                    *, tr):
    @pl.when(pl.program_id(1) == 0)
    def _():
        prev_ref[...] = sh0_ref[0]

    m = m_ref[0]
    row = lax.broadcasted_iota(I32, m.shape, 0)
    m_prev = jnp.where(row == 0, prev_ref[...], pltpu.roll(m, 1, 0))
    prev_ref[...] = m[tr - 1:tr, :]
    ms = m + (m_prev - m) * mu_ref[...]
    r = ms[:, 0:WIDTH]
    k = ms[:, WIDTH:2 * WIDTH]
    v = ms[:, 2 * WIDTH:3 * WIDTH]
    lora_wa = ms[:, 3 * WIDTH:3 * WIDTH + LANES]
    lora_g = ms[:, 3 * WIDTH + LANES:RW_PAD]

    w_log = -_softplus(-(w0_ref[...] + _dot1(jnp.tanh(lora_wa), w2_ref[...]))) - 0.5
    log_decay = -jnp.exp(w_log)
    a = _sigmoid(a0_ref[...] + _dot1(lora_wa, a2_ref[...]))
    g = _dot1(_sigmoid(lora_g), g2_ref[...])

    same_head = (lax.broadcasted_iota(I32, (LANES, LANES), 0) // HEAD_DIM ==
                 lax.broadcasted_iota(I32, (LANES, LANES), 1) // HEAD_DIM).astype(BF16)
    kk = k * kk_ref[...]
    kk = kk / jnp.maximum(jnp.sqrt(_head_sum(kk * kk, same_head)), 1e-12)
    kh = k * (1.0 + (a - 1.0) * ka_ref[...])
    bonus = _head_sum(r * kh * rk_ref[...], same_head)

    c = CHUNK
    ti = lax.broadcasted_iota(I32, (tr, tr), 0)
    si = lax.broadcasted_iota(I32, (tr, tr), 1)
    tri = ((si <= ti) & (si // c == ti // c)).astype(BF16)
    parts = _split3(log_decay)
    cl = _mm(tri, parts[0]) + (_mm(tri, parts[1]) + _mm(tri, parts[2]))
    ec = jnp.exp(cl)
    inv = jnp.exp(-cl)
    rt = r * ec
    kt = kh * inv
    at = -kk * jnp.exp(cl - log_decay)
    bt = kk * a * inv
    rt_ref[0] = rt.astype(BF16)
    kt_ref[0] = kt.astype(BF16)
    bt_ref[0] = bt.astype(BF16)
    v_ref[0] = v.astype(BF16)
    bv_ref[0] = bonus * v
    g_ref[0] = g
    for ch in range(tr // c):
        pend_ref[0, ch] = ec[(ch + 1) * c - 1:(ch + 1) * c, :]

    lane = lax.broadcasted_iota(I32, (c, LANES), 1)
    first = lane < HEAD_DIM
    t_row = lax.broadcasted_iota(I32, (c, LANES), 0)
    s_col = lane % HEAD_DIM
    strict = s_col < t_row
    incl = s_col <= t_row
    eye2 = (s_col == t_row).astype(F32)

    def stack2(x):
        return jnp.concatenate([jnp.where(first, x, 0.0), jnp.where(first, 0.0, x)], axis=0)

    tiles = [(slice(ch * c, (ch + 1) * c), slice(hp * LANES, (hp + 1) * LANES))
             for ch in range(tr // c) for hp in range(WIDTH // LANES)]
    rows2 = lambda top, bottom: jnp.concatenate([top, bottom], axis=0)
    a_t = [at[tl] for tl in tiles]
    l_ab, g_v = [], []
    for i, tl in enumerate(tiles):
        b_s, k_s, v_s = stack2(bt[tl]), stack2(kt[tl]), stack2(v[tl])
        ar = rows2(a_t[i], rt[tl])
        on_b = _dot1(ar, b_s, NT)
        on_k = _dot1(ar, k_s, NT)
        l_ab.append(jnp.where(strict, on_b[:c], 0.0))
        mrb_ref[(0,) + tl] = jnp.where(incl, on_b[c:], 0.0).astype(BF16)
        on_v = _dot1(rows2(jnp.where(strict, on_k[:c], 0.0), jnp.where(incl, on_k[c:], 0.0)), v_s)
        g_v.append(on_v[:c])
        yv_ref[(0,) + tl] = on_v[c:]
    tinv = [eye2 + l for l in l_ab]
    pw = [_dot1(x, stack2(x)) for x in l_ab]
    for _ in range(4):
        both = [_dot1(rows2(p_, t_), stack2(p_)) for p_, t_ in zip(pw, tinv)]
        tinv = [t_ + r_[c:] for t_, r_ in zip(tinv, both)]
        pw = [r_[:c] for r_ in both]
    tinv = [t_ + _dot1(t_, stack2(p_)) for t_, p_ in zip(tinv, pw)]
    for i, tl in enumerate(tiles):
        wu = _dot1(tinv[i], jnp.concatenate([stack2(a_t[i]), stack2(g_v[i])], axis=1))
        wt_ref[(0,) + tl] = wu[:, :LANES].astype(BF16)
        ut_ref[(0,) + tl] = wu[:, LANES:]


def _rw_prep(m, shift0, rwp):
    b, t, _ = m.shape
    tr = min(2 * CHUNK, t)
    row_spec = pl.BlockSpec((1, tr, WIDTH), lambda bi, i: (bi, i, 0))
    vec = lambda n: pl.BlockSpec((1, n), lambda bi, i: (0, 0))
    mat = lambda r: pl.BlockSpec((r, WIDTH), lambda bi, i: (0, 0))
    out = jax.ShapeDtypeStruct((b, t, WIDTH), F32)
    out_mxu = jax.ShapeDtypeStruct((b, t, WIDTH), BF16)
    return pl.pallas_call(
        functools.partial(_rw_prep_kernel, tr=tr),
        grid=(b, t // tr),
        in_specs=[pl.BlockSpec((1, tr, RW_PAD), lambda bi, i: (bi, i, 0)),
                  pl.BlockSpec((1, 1, RW_PAD), lambda bi, i: (bi, 0, 0)),
                  vec(RW_PAD), vec(WIDTH), mat(LANES), vec(WIDTH), mat(LANES), mat(RW_PAD - 3 * WIDTH - LANES),
                  vec(WIDTH), vec(WIDTH), vec(WIDTH)],
        out_specs=[row_spec] * 10 + [pl.BlockSpec((1, tr // CHUNK, 1, WIDTH), lambda bi, i: (bi, i, 0, 0))],
        out_shape=[out_mxu, out, out_mxu, out_mxu, out, out_mxu, out_mxu, out_mxu, out, out,
                   jax.ShapeDtypeStruct((b, t // CHUNK, 1, WIDTH), F32)],
        scratch_shapes=[pltpu.VMEM((1, RW_PAD), F32)],
        compiler_params=_cparams("arbitrary", "arbitrary"),
        name="rwkv_prep",
    )(m, shift0, rwp["mu"], rwp["w0"], rwp["w2"], rwp["a0"], rwp["a2"], rwp["g2"], rwp["k_k"], rwp["k_a"],
      rwp["r_k"])


def _rw_scan_kernel(wt_ref, ut_ref, rt_ref, mrb_ref, yv_ref, bt_ref, kt_ref, v_ref, bv_ref, g_ref, pend_ref,
                    s0_ref, lnw_ref, lnb_ref, o_ref, sout_ref, *s_refs, n_chunks, npair):
    ci = pl.program_id(2)

    @pl.when(ci == 0)
    def _():
        for pi in range(npair):
            s_refs[pi][...] = s0_ref[0, pi]

    c = CHUNK
    first = lax.broadcasted_iota(I32, (c, LANES), 1) < HEAD_DIM
    vi = lax.broadcasted_iota(I32, (LANES, LANES), 0)
    kj = lax.broadcasted_iota(I32, (LANES, LANES), 1)
    same_head_b = (vi // HEAD_DIM) == (kj // HEAD_DIM)
    same_head = same_head_b.astype(BF16)

    def stack2(x):
        return jnp.concatenate([jnp.where(first, x, 0.0), jnp.where(first, 0.0, x)], axis=0)

    def chunk(j, carry):
        rows = pl.ds(pl.multiple_of(j * c, c), c)
        p_all = pend_ref[0, j]
        pairs = range(npair)
        col = lambda pi: slice(pi * LANES, (pi + 1) * LANES)
        s = [s_refs[pi][...] for pi in pairs]
        u = [_dot1(wt_ref[0, rows, col(pi)], s[pi], NT) + ut_ref[0, rows, col(pi)] for pi in pairs]
        for pi in pairs:
            ds = _dot1(jnp.concatenate([u[pi].astype(BF16), v_ref[0, rows, col(pi)]], axis=0),
                       jnp.concatenate([bt_ref[0, rows, col(pi)], kt_ref[0, rows, col(pi)]], axis=0), TN)
            s_refs[pi][...] = (s[pi] + jnp.where(same_head_b, ds, 0.0)) * p_all[:, col(pi)]
        y = [_dot1(rt_ref[0, rows, col(pi)], s[pi], NT) + (_dot1(mrb_ref[0, rows, col(pi)], stack2(u[pi]))
                                                           + yv_ref[0, rows, col(pi)]) for pi in pairs]
        for pi in pairs:
            mu = _dot1(y[pi], same_head) * (1.0 / HEAD_DIM)
            dlt = y[pi] - mu
            var = _dot1(dlt * dlt, same_head) * (1.0 / HEAD_DIM)
            yn = dlt * lax.rsqrt(var + GN_EPS) * lnw_ref[:, col(pi)] + lnb_ref[:, col(pi)]
            o_ref[0, rows, col(pi)] = (yn + bv_ref[0, rows, col(pi)]) * g_ref[0, rows, col(pi)]
        return carry

    lax.fori_loop(0, n_chunks, chunk, 0)

    @pl.when(ci == pl.num_programs(2) - 1)
    def _():
        for pi in range(npair):
            sout_ref[0, pi] = s_refs[pi][...]


def _rw_scan(prep, s0_bd, lnw, lnb, npair=8):
    b, t, _ = prep[0].shape
    tc = min(512, t)
    n_chunks = tc // CHUNK
    ng = WIDTH // (npair * LANES)
    row_spec = pl.BlockSpec((1, tc, npair * LANES), lambda bi, h, i: (bi, i, h))
    pend_spec = pl.BlockSpec((1, n_chunks, 1, npair * LANES), lambda bi, h, i: (bi, i, 0, h))
    st_spec = pl.BlockSpec((1, npair, LANES, LANES), lambda bi, h, i: (bi, h, 0, 0))
    vec_spec = pl.BlockSpec((1, npair * LANES), lambda bi, h, i: (0, h))
    return pl.pallas_call(
        functools.partial(_rw_scan_kernel, n_chunks=n_chunks, npair=npair),
        grid=(b, ng, t // tc),
        in_specs=[row_spec] * 10 + [pend_spec, st_spec, vec_spec, vec_spec],
        out_specs=[row_spec, st_spec],
        out_shape=[jax.ShapeDtypeStruct((b, t, WIDTH), F32),
                   jax.ShapeDtypeStruct((b, WIDTH // LANES, LANES, LANES), F32)],
        scratch_shapes=[pltpu.VMEM((LANES, LANES), F32)] * npair,
        compiler_params=_cparams("arbitrary", "arbitrary", "arbitrary"),
        name="rwkv_scan",
    )(*prep, s0_bd, lnw, lnb)


def _merge_kernel(oa_ref, ob_ref, sg_ref, x_ref, g1_ref, sc2_ref, sh2_ref, pa_ref, pb_ref, wo_ref, lng_ref, lnb_ref,
                  wr_ref, br_ref, cnt0_ref, h_ref, u2_ref, route_ref, cnt_out_ref, cnt_ref):
    sg = sg_ref[0]
    ya = _mm(oa_ref[0].astype(BF16), pa_ref[...])
    yb = _mm(ob_ref[0].astype(BF16), pb_ref[...])
    merged = sg[:, :D_MODEL].astype(F32) * ya + sg[:, D_MODEL:].astype(F32) * yb
    mix = _mm(merged.astype(BF16), wo_ref[...])
    hin = ALPHA * x_ref[0] + g1_ref[0] * mix
    mu = jnp.mean(hin, axis=-1, keepdims=True)
    dlt = hin - mu
    var = jnp.mean(dlt * dlt, axis=-1, keepdims=True)
    h = dlt * lax.rsqrt(var + LN_EPS) * lng_ref[...] + lnb_ref[...]
    h_ref[0] = h
    u2 = h * (1.0 + sc2_ref[0]) + sh2_ref[0]
    u2_ref[0] = u2

    logits = _dot3(u2, wr_ref[...]) + br_ref[...]
    lane_i = lax.broadcasted_iota(I32, logits.shape, 1)
    lane = lane_i.astype(F32)
    neg = -jnp.inf
    first_at = lambda vals, top: jnp.min(jnp.where(vals == top, lane, float(LANES)), axis=1, keepdims=True)
    lg = jnp.where(lane_i < N_GROUPS, logits, neg)
    mg = jnp.max(lg, axis=1, keepdims=True)
    g_idx = first_at(lg, mg)
    p_g = 1.0 / jnp.sum(jnp.exp(lg - mg), axis=1, keepdims=True)
    lo_lane = N_GROUPS + PER_GROUP * g_idx
    le = jnp.where((lane >= lo_lane) & (lane < lo_lane + PER_GROUP), logits, neg)
    m1 = jnp.max(le, axis=1, keepdims=True)
    i1 = first_at(le, m1)
    le2 = jnp.where(lane == i1, neg, le)
    m2 = jnp.max(le2, axis=1, keepdims=True)
    i2 = first_at(le2, m2)
    ratio = jnp.exp(m2 - m1)
    w1 = p_g / (1.0 + ratio)
    w2 = p_g * ratio / (1.0 + ratio)
    @pl.when((pl.program_id(0) == 0) & (pl.program_id(1) == 0))
    def _():
        cnt_ref[...] = cnt0_ref[...]

    tm = logits.shape[0]
    chosen = ((lane == i1) | (lane == i2)).astype(BF16)
    earlier = (lax.broadcasted_iota(I32, (tm, tm), 1) < lax.broadcasted_iota(I32, (tm, tm), 0)).astype(BF16)
    before = _mm(earlier, chosen) + cnt_ref[...]
    rank1 = jnp.sum(jnp.where(lane == i1, before, 0.0), axis=1, keepdims=True)
    rank2 = jnp.sum(jnp.where(lane == i2, before, 0.0), axis=1, keepdims=True)
    cnt_ref[...] = before[tm - 1:tm, :] + chosen[tm - 1:tm, :].astype(F32)
    cnt_out_ref[...] = cnt_ref[...]

    route = jnp.where(lane_i == 0, i1 - N_GROUPS, 0.0)
    for k, val in enumerate((i2 - N_GROUPS, w1, w2, rank1, rank2), start=1):
        route = jnp.where(lane_i == k, val, route)
    route_ref[0] = route


def _merge(o_a, o_b, sg, x, g1, sc2, sh2, wts, cnt0):
    b, t, d = x.shape
    tm = min(256, t)
    const = lambda shape: pl.BlockSpec(shape, lambda bi, i: (0,) * len(shape), pipeline_mode=pl.Buffered(1))
    seq = pl.BlockSpec((1, 1, d), lambda bi, i: (bi, 0, 0))
    row = lambda n: pl.BlockSpec((1, tm, n), lambda bi, i: (bi, i, 0))
    return pl.pallas_call(
        _merge_kernel,
        grid=(b, t // tm),
        in_specs=[row(WIDTH), row(WIDTH), row(2 * d), row(d), seq, seq, seq,
                  const((WIDTH, d)), const((WIDTH, d)), const((d, d)), const((1, d)), const((1, d)),
                  const((d, LANES)), const((1, LANES)), const((1, LANES))],
        out_specs=[row(d), row(d), row(LANES), pl.BlockSpec((1, LANES), lambda bi, i: (0, 0))],
        out_shape=[jax.ShapeDtypeStruct((b, t, d), F32), jax.ShapeDtypeStruct((b, t, d), F32),
                   jax.ShapeDtypeStruct((b, t, LANES), F32), jax.ShapeDtypeStruct((1, LANES), F32)],
        scratch_shapes=[pltpu.VMEM((1, LANES), F32)],
        compiler_params=_cparams("arbitrary", "arbitrary"),
        name="merge_route",
    )(o_a, o_b, sg, x, g1, sc2, sh2, wts["p_a"], wts["p_b"], wts["w_out"], wts["ln1_g"], wts["ln1_b"],
      wts["w_route"], wts["b_route"], cnt0)


def _moe_kernel(be_ref, nused_ref, tok_ref, x_hbm, wg_ref, wu_ref, wd_ref, y_ref, xbuf, sem, *, tb):
    i = pl.program_id(0)
    n_used = nused_ref[0]
    last = n_used - 1

    def row_copy(block, r, slot):
        return pltpu.make_async_copy(x_hbm.at[pl.ds(tok_ref[block * tb + r], 1)], xbuf.at[slot, pl.ds(r, 1)],
                                     sem.at[slot])

    def wait_slot(slot):
        pltpu.make_async_copy(x_hbm.at[pl.ds(0, tb)], xbuf.at[slot], sem.at[slot]).wait()

    @pl.when((i == 0) & (n_used > 0))
    def _():
        def start(r, carry):
            row_copy(0, r, 0).start()
            return carry

        lax.fori_loop(0, tb, start, 0, unroll=8)

    @pl.when(i < n_used)
    def _():
        slot = i % 2
        wait_slot(slot)
        x = xbuf[slot].astype(BF16)
        nxt = jnp.minimum(i + 1, last)
        for r in range(tb):
            row_copy(nxt, r, 1 - slot).start()
        hg = _mm(x, wg_ref[0])
        hu = _mm(x, wu_ref[0])
        hid = (hg * _sigmoid(hg) * hu).astype(BF16)
        y_ref[...] = _mm(hid, wd_ref[0])

    @pl.when(i == last)
    def _():
        wait_slot(1 - i % 2)

    @pl.when(i >= n_used)
    def _():
        y_ref[...] = jnp.zeros_like(y_ref)


def _moe(block_expert, n_used, slot_token, x_all, wg, wu, wd, tb):
    n_blocks = block_expert.shape[0]
    d = x_all.shape[1]
    grid_spec = pltpu.PrefetchScalarGridSpec(
        num_scalar_prefetch=3,
        grid=(n_blocks,),
        in_specs=[pl.BlockSpec(memory_space=pl.ANY),
                  pl.BlockSpec((1, d, D_EXPERT), lambda i, be, nu, tok: (be[i], 0, 0), pipeline_mode=pl.Buffered(1)),
                  pl.BlockSpec((1, d, D_EXPERT), lambda i, be, nu, tok: (be[i], 0, 0), pipeline_mode=pl.Buffered(1)),
                  pl.BlockSpec((1, D_EXPERT, d), lambda i, be, nu, tok: (be[i], 0, 0), pipeline_mode=pl.Buffered(1))],
        out_specs=pl.BlockSpec((tb, d), lambda i, be, nu, tok: (i, 0)),
        scratch_shapes=[pltpu.VMEM((2, tb, d), F32), pltpu.SemaphoreType.DMA((2,))],
    )
    return pl.pallas_call(
        functools.partial(_moe_kernel, tb=tb),
        grid_spec=grid_spec,
        out_shape=jax.ShapeDtypeStruct((n_blocks * tb, d), F32),
        compiler_params=_cparams("arbitrary"),
        name="moe_experts",
    )(block_expert, n_used, slot_token, x_all, wg, wu, wd)


def _final_kernel(dest_ref, y_hbm, h_ref, rw_ref, g2_ref, lng_ref, lnb_ref, o_ref, ybuf, sem, *, tm, per_seq):
    step = pl.program_id(0) * per_seq + pl.program_id(1)
    n_steps = pl.num_programs(0) * per_seq

    def row_copy(blk, r, k, slot):
        return pltpu.make_async_copy(y_hbm.at[pl.ds(dest_ref[2 * (blk * tm + r) + k], 1)],
                                     ybuf.at[slot, k, pl.ds(r, 1)], sem.at[slot])

    def wait_slot(slot):
        for k in range(2):
            pltpu.make_async_copy(y_hbm.at[pl.ds(0, tm)], ybuf.at[slot, k], sem.at[slot]).wait()

    @pl.when(step == 0)
    def _():
        def start(r, carry):
            for k in range(2):
                row_copy(0, r, k, 0).start()
            return carry

        lax.fori_loop(0, tm, start, 0, unroll=4)

    slot = step % 2
    wait_slot(slot)
    nxt = jnp.minimum(step + 1, n_steps - 1)
    for r in range(tm):
        for k in range(2):
            row_copy(nxt, r, k, 1 - slot).start()
    rw = rw_ref[0]
    moe = rw[:, 2:3] * ybuf[slot, 0] + rw[:, 3:4] * ybuf[slot, 1]
    xin = ALPHA * h_ref[0] + g2_ref[0] * moe
    mu = jnp.mean(xin, axis=-1, keepdims=True)
    dlt = xin - mu
    var = jnp.mean(dlt * dlt, axis=-1, keepdims=True)
    o_ref[0] = dlt * lax.rsqrt(var + LN_EPS) * lng_ref[...] + lnb_ref[...]

    @pl.when(step == n_steps - 1)
    def _():
        wait_slot(1 - slot)


def _final(dest, y_sorted, h, route, g2, ln2_g, ln2_b):
    b, t, d = h.shape
    tm = min(256, t)
    per_seq = t // tm
    grid_spec = pltpu.PrefetchScalarGridSpec(
        num_scalar_prefetch=1,
        grid=(b, per_seq),
        in_specs=[pl.BlockSpec(memory_space=pl.ANY),
                  pl.BlockSpec((1, tm, d), lambda bi, i, de: (bi, i, 0)),
                  pl.BlockSpec((1, tm, LANES), lambda bi, i, de: (bi, i, 0)),
                  pl.BlockSpec((1, 1, d), lambda bi, i, de: (bi, 0, 0)),
                  pl.BlockSpec((1, d), lambda bi, i, de: (0, 0)),
                  pl.BlockSpec((1, d), lambda bi, i, de: (0, 0))],
        out_specs=pl.BlockSpec((1, tm, d), lambda bi, i, de: (bi, i, 0)),
        scratch_shapes=[pltpu.VMEM((2, 2, tm, d), F32), pltpu.SemaphoreType.DMA((2,))],
    )
    return pl.pallas_call(
        functools.partial(_final_kernel, tm=tm, per_seq=per_seq),
        grid_spec=grid_spec,
        out_shape=jax.ShapeDtypeStruct((b, t, d), F32),
        compiler_params=_cparams("arbitrary", "arbitrary"),
        name="combine_ln2",
    )(dest, y_sorted, h, route, g2, ln2_g, ln2_b)


def _pad_rows(a, rows):
    return jnp.pad(a, ((0, rows - a.shape[0]), (0, 0)))


def _mixers(x, ada, past_k, past_v, s0, shift0, wts, sb_tiles, cnt0):
    b, t, d = x.shape
    sh1, sc1, g1, sh2, sc2, g2 = [a[:, None, :] for a in jnp.split(ada, 6, axis=-1)]
    q = _modmm(x, sc1, sh1, wts["w_q"], 1024, name="proj_q")
    k = _modmm(x, sc1, sh1, wts["w_k"], 1024, name="proj_k")
    v = _modmm(x, sc1, sh1, wts["w_v"], 1024, name="proj_v")
    m = _modmm(x, sc1, sh1, wts["w_m"], RW_PAD // 3, name="proj_rwkv")
    sg = _modmm(x, sc1, sh1, wts["w_g"], 1024, out_dtype=BF16, sigmoid_out=True, name="proj_gates")

    tq, tk, nl = sb_tiles
    past = past_k.shape[1]
    o_a = _sb_attention(q, k, v, past_k.reshape(b, past, WIDTH), past_v.reshape(b, past, WIDTH), tq, tk, nl)

    shift0_p = jnp.pad(shift0, ((0, 0), (0, 0), (0, RW_PAD - RW_COLS)))
    prep = _rw_prep(m, shift0_p, wts["rw"])
    nhp = WIDTH // LANES
    s0_pairs = s0.reshape(b, nhp, 2, HEAD_DIM, HEAD_DIM)
    zero4 = ((0, 0), (0, 0))
    s0_bd = (jnp.pad(s0_pairs[:, :, 0], zero4 + ((0, HEAD_DIM), (0, HEAD_DIM)))
             + jnp.pad(s0_pairs[:, :, 1], zero4 + ((HEAD_DIM, 0), (HEAD_DIM, 0))))
    o_b, s_bd = _rw_scan(prep, s0_bd, wts["rw"]["lnx_w"], wts["rw"]["lnx_b"])
    s_new = jnp.stack([s_bd[:, :, :HEAD_DIM, :HEAD_DIM], s_bd[:, :, HEAD_DIM:, HEAD_DIM:]], axis=2)
    s_new = s_new.reshape(b, 2 * nhp, HEAD_DIM, HEAD_DIM)
    shift_new = m[:, t - 1:t, :RW_COLS]

    h, u2, route, cnt = _merge(o_a, o_b, sg, x, g1, sc2, sh2, wts, cnt0)
    return h, u2, route, cnt, g2, k, v, s_new, shift_new


def kernel(x_prompt, x_sample, cache_sb_k, cache_sb_v, state_rwkv, state_rwkv_shift, c_prompt, c_sample, w_ada, b_ada, w_in, tokshift_mu, rw_w0, rw_w2, rw_a0, rw_a2, rw_g2, rw_k_k, rw_k_a, rw_r_k, rw_lnx_w, rw_lnx_b, w_branch_a, w_branch_b, w_out, ln1_g, ln1_b, w_router_group, b_router_group, w_router_expert, b_router_expert, w_exp_gate, w_exp_up, w_exp_down, ln2_g, ln2_b):
    depth = w_ada.shape[0]
    assert depth == 1
    bp, tp, d = x_prompt.shape
    bs, ts, _ = x_sample.shape
    l = 0

    w_in_l = w_in[l]
    w_rw = w_in_l[:, 3 * WIDTH:3 * WIDTH + RW_COLS]
    row_vec = lambda a: a.reshape(1, -1)
    pad_cols = lambda a, n: jnp.pad(a, ((0, 0), (0, n - a.shape[1])))
    n_lg = RW_PAD - 3 * WIDTH - LANES
    wts = {
        "w_q": w_in_l[:, 0:WIDTH].astype(BF16),
        "w_k": w_in_l[:, WIDTH:2 * WIDTH].astype(BF16),
        "w_v": w_in_l[:, 2 * WIDTH:3 * WIDTH].astype(BF16),
        "w_m": pad_cols(w_rw, RW_PAD).astype(BF16),
        "w_g": w_in_l[:, 3 * WIDTH + RW_COLS:].astype(BF16),
        "p_a": w_branch_a[l].astype(BF16),
        "p_b": w_branch_b[l].astype(BF16),
        "w_out": w_out[l].astype(BF16),
        "ln1_g": row_vec(ln1_g[l]),
        "ln1_b": row_vec(ln1_b[l]),
        "w_route": pad_cols(jnp.concatenate([w_router_group[l], w_router_expert[l]], axis=1), LANES),
        "b_route": pad_cols(row_vec(jnp.concatenate([b_router_group[l], b_router_expert[l]])), LANES),
        "rw": {
            "mu": pad_cols(row_vec(tokshift_mu[l]), RW_PAD),
            "w0": row_vec(rw_w0[l]),
            "w2": _pad_rows(rw_w2[l], LANES),
            "a0": row_vec(rw_a0[l]),
            "a2": jnp.concatenate([jnp.zeros((LORA_W, WIDTH), F32), rw_a2[l]], axis=0),
            "g2": _pad_rows(rw_g2[l], n_lg),
            "k_k": row_vec(rw_k_k[l]),
            "k_a": row_vec(rw_k_a[l]),
            "r_k": row_vec(rw_r_k[l]),
            "lnx_w": row_vec(rw_lnx_w[l]),
            "lnx_b": row_vec(rw_lnx_b[l]),
        },
    }

    rows = -(-(bp + bs) // 8) * 8
    c_all = _pad_rows(jnp.concatenate([c_prompt, c_sample], axis=0), rows)
    ada = _ada(c_all, w_ada[l], b_ada[l])

    empty = jnp.zeros((bp, 0, WIDTH), F32)
    zero_state = jnp.zeros((bp, WIDTH // HEAD_DIM, HEAD_DIM, HEAD_DIM), F32)
    zero_shift = jnp.zeros((bp, 1, RW_COLS), F32)
    hp, u2p, route_p, cnt_p, g2p, kp, vp, sp_, shp = _mixers(
        x_prompt, ada[:bp], empty, empty, zero_state, zero_shift, wts, (min(256, tp), 256, 8),
        jnp.zeros((1, LANES), F32))
    hs, u2s, route_s, cnt_all, g2s, ks, vs, ss_, shs = _mixers(
        x_sample, ada[bp:bp + bs], cache_sb_k[l], cache_sb_v[l], state_rwkv[l], state_rwkv_shift[l], wts,
        (min(256, ts), 256, 8), cnt_p)

    n_p, n_s = bp * tp, bs * ts
    n_tok = n_p + n_s
    tb = 512
    route_all = jnp.concatenate([route_p.reshape(n_p, LANES), route_s.reshape(n_s, LANES)], axis=0)
    flat_e = route_all[:, 0:2].astype(I32).reshape(n_tok * 2)
    rank = route_all[:, 4:6].astype(I32).reshape(n_tok * 2)
    counts = cnt_all[0, N_GROUPS:N_GROUPS + N_EXPERTS].astype(I32)
    padded = (counts + tb - 1) // tb * tb
    pend = jnp.cumsum(padded)
    pstart = pend - padded
    dest = (pstart[flat_e] + rank).astype(I32)
    n_blocks = (2 * n_tok + N_EXPERTS * (tb - 1)) // tb + 1
    slot_token = jnp.zeros((n_blocks * tb,), I32).at[dest].set(
        jnp.arange(2 * n_tok, dtype=I32) // 2, unique_indices=True, mode="promise_in_bounds")
    block_start = jnp.arange(n_blocks, dtype=I32) * tb
    block_expert = jnp.minimum(jnp.sum((pend[None, :] <= block_start[:, None]).astype(I32), axis=1), N_EXPERTS - 1)
    n_used = (pend[-1] // tb).astype(I32).reshape(1)

    u2_all = jnp.concatenate([u2p.reshape(n_p, d), u2s.reshape(n_s, d)], axis=0)
    y_sorted = _moe(block_expert, n_used, slot_token, u2_all, w_exp_gate[l].astype(BF16),
                    w_exp_up[l].astype(BF16), w_exp_down[l].astype(BF16), tb)

    ln2g, ln2b = row_vec(ln2_g[l]), row_vec(ln2_b[l])
    y_prompt = _final(dest[:2 * n_p], y_sorted, hp, route_p, g2p, ln2g, ln2b)
    y_sample = _final(dest[2 * n_p:], y_sorted, hs, route_s, g2s, ln2g, ln2b)

    heads = lambda a: a.reshape(1, a.shape[0], a.shape[1], WIDTH // HEAD_DIM, HEAD_DIM)
    return (y_prompt, y_sample, heads(kp), heads(vp), sp_[None], shp[None],
            heads(ks), heads(vs), ss_[None], shs[None])
```

```python
import functools

import jax
import jax.numpy as jnp
import numpy as np
from jax import lax
from jax.experimental import pallas as pl
from jax.experimental.pallas import tpu as pltpu

F32 = jnp.float32
BF16 = jnp.bfloat16
I32 = jnp.int32

D_MODEL = 2048
HEAD_DIM = 64
WIDTH = 1024
LORA_W = 64
LORA_A = 64
LORA_G = 160
RW_COLS = 3 * WIDTH + LORA_W + LORA_A + LORA_G
RW_PAD = 3456
N_GROUPS = 4
PER_GROUP = 8
N_EXPERTS = 32
D_EXPERT = 1024
LN_EPS = 1e-5
GN_EPS = 64e-5
ALPHA = 2.0 ** 0.25
LOG2E = 1.4426950408889634
F32_UNDERFLOW_LOG2 = -150.0
LANES = 128
CHUNK = 64
VMEM_LIMIT = 56 * 1024 * 1024

NN = (((1,), (0,)), ((), ()))
NT = (((1,), (1,)), ((), ()))
TN = (((0,), (0,)), ((), ()))


def _cparams(*sem):
    return pltpu.CompilerParams(dimension_semantics=sem, vmem_limit_bytes=VMEM_LIMIT)


def _mm(a, b, dims=NN):
    return lax.dot_general(a, b, dims, preferred_element_type=F32)


def _split(x):
    hi = x.astype(BF16)
    lo = (x - hi.astype(F32)).astype(BF16)
    return hi, lo


def _split3(x):
    hi = x.astype(BF16)
    r1 = x - hi.astype(F32)
    mid = r1.astype(BF16)
    lo = (r1 - mid.astype(F32)).astype(BF16)
    return hi, mid, lo


def _dot3(a, b, dims=NN):
    ah, al = _split(a)
    bh, bl = _split(b)
    return _mm(ah, bh, dims) + (_mm(al, bh, dims) + _mm(ah, bl, dims))


def _dot1(a, b, dims=NN):
    return _mm(a.astype(BF16), b.astype(BF16), dims)


def _dot_exact_rhs(a, b_bf16, terms=2):
    parts = _split(a) if terms == 2 else _split3(a)
    out = _mm(parts[0], b_bf16)
    for p in parts[1:]:
        out = out + _mm(p, b_bf16)
    return out


def _softplus(z):
    return jnp.maximum(z, 0.0) + jnp.log(1.0 + jnp.exp(-jnp.abs(z)))


def _sigmoid(z):
    return 1.0 / (1.0 + jnp.exp(-z))


def _ada_kernel(c_ref, w_ref, b_ref, o_ref):
    c = c_ref[...]
    o_ref[...] = _dot3(c * _sigmoid(c), w_ref[...]) + b_ref[...]


def _ada(c_all, w_ada, b_ada):
    rows = c_all.shape[0]
    n = w_ada.shape[1]
    tn = 1024
    return pl.pallas_call(
        _ada_kernel,
        grid=(n // tn,),
        in_specs=[pl.BlockSpec((rows, D_MODEL), lambda j: (0, 0)),
                  pl.BlockSpec((D_MODEL, tn), lambda j: (0, j)),
                  pl.BlockSpec((1, tn), lambda j: (0, j))],
        out_specs=pl.BlockSpec((rows, tn), lambda j: (0, j)),
        out_shape=jax.ShapeDtypeStruct((rows, n), F32),
        compiler_params=_cparams("arbitrary"),
        name="ada",
    )(c_all, w_ada, b_ada.reshape(1, n))


def _modmm_kernel(x_ref, sc_ref, sh_ref, w_ref, o_ref, u_ref, *, sigmoid_out):
    @pl.when(pl.program_id(2) == 0)
    def _():
        u_ref[...] = (x_ref[0] * (1.0 + sc_ref[0]) + sh_ref[0]).astype(BF16)

    y = _mm(u_ref[...], w_ref[...])
    if sigmoid_out:
        y = _sigmoid(y)
    o_ref[0] = y.astype(o_ref.dtype)


def _modmm(x, sc, sh, w_bf16, tn, out_dtype=F32, sigmoid_out=False, name="proj"):
    b, t, d = x.shape
    n = w_bf16.shape[1]
    tm = min(1024, t)
    return pl.pallas_call(
        functools.partial(_modmm_kernel, sigmoid_out=sigmoid_out),
        grid=(b, t // tm, n // tn),
        in_specs=[pl.BlockSpec((1, tm, d), lambda bi, i, j: (bi, i, 0)),
                  pl.BlockSpec((1, 1, d), lambda bi, i, j: (bi, 0, 0)),
                  pl.BlockSpec((1, 1, d), lambda bi, i, j: (bi, 0, 0)),
                  pl.BlockSpec((d, tn), lambda bi, i, j: (0, j))],
        out_specs=pl.BlockSpec((1, tm, tn), lambda bi, i, j: (bi, i, j)),
        out_shape=jax.ShapeDtypeStruct((b, t, n), out_dtype),
        scratch_shapes=[pltpu.VMEM((tm, d), BF16)],
        compiler_params=_cparams("arbitrary", "arbitrary", "arbitrary"),
        name=name,
    )(x, sc, sh, w_bf16)


def _sb_kernel(q_ref, k_hbm, v_hbm, kold_hbm, vold_hbm, o_ref, kbuf, vbuf, sem, done_ref, *scratch,
               tq, tk, past, nl, split):
    accs, cars = scratch[:2 * nl], scratch[2 * nl:]
    bi, g, qi = pl.program_id(0), pl.program_id(1), pl.program_id(2)
    width = nl * LANES
    n_g, n_q = pl.num_programs(1), pl.num_programs(2)
    step = (bi * n_g + g) * n_q + qi
    n_steps = pl.num_programs(0) * n_g * n_q
    top_of = lambda q_idx: (past + (q_idx + 1) * tq - 2) // tk
    top = top_of(qi)

    def kv_copies(kb, slot, b_idx=bi, g_idx=g, newest=False):
        lanes = pl.ds(g_idx * width, width)
        if split and newest:
            src, rows = (b_idx, pl.ds(0, tq), lanes), pl.ds(0, tq)
            return (pltpu.make_async_copy(k_hbm.at[src], kbuf.at[slot, rows], sem.at[0, slot]),
                    pltpu.make_async_copy(v_hbm.at[src], vbuf.at[slot, rows], sem.at[1, slot]))
        k_src, v_src = (kold_hbm, vold_hbm) if split else (k_hbm, v_hbm)
        src = (b_idx, pl.ds(kb * tk, tk), lanes)
        return (pltpu.make_async_copy(k_src.at[src], kbuf.at[slot], sem.at[0, slot]),
                pltpu.make_async_copy(v_src.at[src], vbuf.at[slot], sem.at[1, slot]))

    @pl.when(step == 0)
    def _():
        if split and tq < tk:
            for buf in (kbuf, vbuf):
                buf[:, tq:, :] = jnp.zeros((2, tk - tq, width), F32)
        for cp in kv_copies(top, 0, newest=True):
            cp.start()

    for ref in scratch:
        ref[...] = jnp.zeros_like(ref)
    done_ref[0] = 0

    later = (lax.broadcasted_iota(I32, (tk, tk), 0) > lax.broadcasted_iota(I32, (tk, tk), 1)).astype(BF16)
    lane = lax.broadcasted_iota(I32, (tq, LANES), 1)

    def sweep(kb, slot, masked):
        if masked:
            pos = past + qi * tq + lax.broadcasted_iota(I32, (tq, tk), 0)
            kpos = kb * tk + lax.broadcasted_iota(I32, (tq, tk), 1)
            mask = kpos < pos
        for l in range(nl):
            cols = slice(l * LANES, (l + 1) * LANES)
            q = q_ref[0, :, cols] * (HEAD_DIM ** -0.5 * LOG2E)
            kbf = kbuf[slot, :, cols].astype(BF16)
            vbf = vbuf[slot, :, cols].astype(BF16)
            for hh in range(2):
                head = (lane < HEAD_DIM) if hh == 0 else (lane >= HEAD_DIM)
                z = _mm(jnp.where(head, q, 0.0).astype(BF16), kbf, NT)
                neg_abs = lax.bitcast_convert_type(lax.bitcast_convert_type(z, jnp.uint32) | jnp.uint32(0x80000000), F32)
                soft = jnp.log2(1.0 + jnp.exp2(neg_abs))
                log_beta = jnp.minimum(z, 0.0) - soft
                log_keep = log_beta - z
                if masked:
                    log_keep = jnp.where(mask, log_keep, 0.0)
                car = cars[2 * l + hh]
                after = _mm(log_keep.astype(BF16), later) + car[...]
                w = jnp.exp2(log_beta + after)
                if masked:
                    w = jnp.where(mask, w, 0.0)
                accs[2 * l + hh][...] += _mm(w.astype(BF16), vbf)
                car[...] += jnp.sum(log_keep, axis=1, keepdims=True)
        worst = cars[0][...]
        for car in cars[1:]:
            worst = jnp.maximum(worst, car[...])
        done_ref[0] = (jnp.max(worst) < F32_UNDERFLOW_LOG2).astype(I32)

    def body(kb):
        slot = (top - kb) % 2
        if split:
            @pl.when(kb == top)
            def _():
                for cp in kv_copies(kb, slot, newest=True):
                    cp.wait()

            @pl.when(kb != top)
            def _():
                for cp in kv_copies(kb, slot):
                    cp.wait()
        else:
            for cp in kv_copies(kb, slot):
                cp.wait()

        @pl.when(kb > 0)
        def _():
            for cp in kv_copies(kb - 1, 1 - slot):
                cp.start()

        diag = (kb + 1) * tk - 1 >= past + qi * tq

        @pl.when(diag)
        def _():
            sweep(kb, slot, True)

        @pl.when(jnp.logical_not(diag))
        def _():
            sweep(kb, slot, False)

        return kb - 1

    kb_end = lax.while_loop(lambda kb: (kb >= 0) & (done_ref[0] == 0), body, top)

    @pl.when(kb_end >= 0)
    def _():
        for cp in kv_copies(kb_end, (top - kb_end) % 2):
            cp.wait()

    @pl.when(step + 1 < n_steps)
    def _():
        wrap_q = qi + 1 == n_q
        wrap_g = wrap_q & (g + 1 == n_g)
        q_next = jnp.where(wrap_q, 0, qi + 1)
        g_next = jnp.where(wrap_g, 0, jnp.where(wrap_q, g + 1, g))
        b_next = jnp.where(wrap_g, bi + 1, bi)
        for cp in kv_copies(top_of(q_next), 0, b_next, g_next, newest=True):
            cp.start()

    for l in range(nl):
        o_ref[0, :, l * LANES:(l + 1) * LANES] = jnp.where(lane < HEAD_DIM, accs[2 * l][...], accs[2 * l + 1][...])


def _sb_attention(q, k_new, v_new, past_k, past_v, tq, tk, nl):
    b, t, _ = q.shape
    past = past_k.shape[1]
    width = nl * LANES
    split = past > 0 and past % tk == 0 and t == tq and tq == tk
    if split:
        operands = (k_new, v_new, past_k, past_v)
    else:
        tail = jnp.zeros((b, -(-(past + t) // tk) * tk - past - t, WIDTH), F32)
        joined = [jnp.concatenate([p for p in (old, new, tail) if p.shape[1] > 0], axis=1)
                  for old, new in ((past_k, k_new), (past_v, v_new))]
        operands = (joined[0], joined[1], joined[0], joined[1])
    q_spec = pl.BlockSpec((1, tq, width), lambda bi, g, qi: (bi, qi, g))
    return pl.pallas_call(
        functools.partial(_sb_kernel, tq=tq, tk=tk, past=past, nl=nl, split=split),
        grid=(b, WIDTH // width, t // tq),
        in_specs=[q_spec] + [pl.BlockSpec(memory_space=pl.ANY)] * 4,
        out_specs=q_spec,
        out_shape=jax.ShapeDtypeStruct((b, t, WIDTH), F32),
        scratch_shapes=[pltpu.VMEM((2, tk, width), F32), pltpu.VMEM((2, tk, width), F32),
                        pltpu.SemaphoreType.DMA((2, 2)), pltpu.SMEM((1,), I32)]
        + [pltpu.VMEM((tq, LANES), F32)] * (2 * nl) + [pltpu.VMEM((tq, 1), F32)] * (2 * nl),
        compiler_params=_cparams("arbitrary", "arbitrary", "arbitrary"),
        name="sb_attention",
    )(q, *operands)


def _head_sum(x, same_head):
    return jnp.concatenate(
        [_dot1(x[:, c * LANES:(c + 1) * LANES], same_head) for c in range(WIDTH // LANES)], axis=1)


def _rw_prep_kernel(m_ref, sh0_ref, mu_ref, w0_ref, w2_ref, a0_ref, a2_ref, g2_ref, kk_ref, ka_ref, rk_ref,
                    wt_ref, ut_ref, rt_ref, mrb_ref, yv_ref, bt_ref, kt_ref, v_ref, bv_ref, g_ref, pend_ref, prev_ref,
                    *, tr):
    @pl.when(pl.program_id(1) == 0)
    def _():
        prev_ref[...] = sh0_ref[0]

    m = m_ref[0]
    row = lax.broadcasted_iota(I32, m.shape, 0)
    m_prev = jnp.where(row == 0, prev_ref[...], pltpu.roll(m, 1, 0))
    prev_ref[...] = m[tr - 1:tr, :]
    ms = m + (m_prev - m) * mu_ref[...]
    r = ms[:, 0:WIDTH]
    k = ms[:, WIDTH:2 * WIDTH]
    v = ms[:, 2 * WIDTH:3 * WIDTH]
    lora_wa = ms[:, 3 * WIDTH:3 * WIDTH + LANES]
    lora_g = ms[:, 3 * WIDTH + LANES:RW_PAD]

    w_log = -_softplus(-(w0_ref[...] + _dot1(jnp.tanh(lora_wa), w2_ref[...]))) - 0.5
    log_decay = -jnp.exp(w_log)
    a = _sigmoid(a0_ref[...] + _dot1(lora_wa, a2_ref[...]))
    g = _dot1(_sigmoid(lora_g), g2_ref[...])

    same_head = (lax.broadcasted_iota(I32, (LANES, LANES), 0) // HEAD_DIM ==
                 lax.broadcasted_iota(I32, (LANES, LANES), 1) // HEAD_DIM).astype(BF16)
    kk = k * kk_ref[...]
    kk = kk / jnp.maximum(jnp.sqrt(_head_sum(kk * kk, same_head)), 1e-12)
    kh = k * (1.0 + (a - 1.0) * ka_ref[...])
    bonus = _head_sum(r * kh * rk_ref[...], same_head)

    c = CHUNK
    ti = lax.broadcasted_iota(I32, (tr, tr), 0)
    si = lax.broadcasted_iota(I32, (tr, tr), 1)
    tri = ((si <= ti) & (si // c == ti // c)).astype(BF16)
    parts = _split3(log_decay)
    cl = _mm(tri, parts[0]) + (_mm(tri, parts[1]) + _mm(tri, parts[2]))
    ec = jnp.exp(cl)
    inv = jnp.exp(-cl)
    rt = r * ec
    kt = kh * inv
    at = -kk * jnp.exp(cl - log_decay)
    bt = kk * a * inv
    rt_ref[0] = rt.astype(BF16)
    kt_ref[0] = kt.astype(BF16)
    bt_ref[0] = bt.astype(BF16)
    v_ref[0] = v.astype(BF16)
    bv_ref[0] = bonus * v
    g_ref[0] = g
    for ch in range(tr // c):
        pend_ref[0, ch] = ec[(ch + 1) * c - 1:(ch + 1) * c, :]

    lane = lax.broadcasted_iota(I32, (c, LANES), 1)
    first = lane < HEAD_DIM
    t_row = lax.broadcasted_iota(I32, (c, LANES), 0)
    s_col = lane % HEAD_DIM
    strict = s_col < t_row
    incl = s_col <= t_row
    eye2 = (s_col == t_row).astype(F32)

    def stack2(x):
        return jnp.concatenate([jnp.where(first, x, 0.0), jnp.where(first, 0.0, x)], axis=0)

    tiles = [(slice(ch * c, (ch + 1) * c), slice(hp * LANES, (hp + 1) * LANES))
             for ch in range(tr // c) for hp in range(WIDTH // LANES)]
    rows2 = lambda top, bottom: jnp.concatenate([top, bottom], axis=0)
    a_t = [at[tl] for tl in tiles]
    l_ab, g_v = [], []
    for i, tl in enumerate(tiles):
        b_s, k_s, v_s = stack2(bt[tl]), stack2(kt[tl]), stack2(v[tl])
        ar = rows2(a_t[i], rt[tl])
        on_b = _dot1(ar, b_s, NT)
        on_k = _dot1(ar, k_s, NT)
        l_ab.append(jnp.where(strict, on_b[:c], 0.0))
        mrb_ref[(0,) + tl] = jnp.where(incl, on_b[c:], 0.0).astype(BF16)
        on_v = _dot1(rows2(jnp.where(strict, on_k[:c], 0.0), jnp.where(incl, on_k[c:], 0.0)), v_s)
        g_v.append(on_v[:c])
        yv_ref[(0,) + tl] = on_v[c:]
    tinv = [eye2 + l for l in l_ab]
    pw = [_dot1(x, stack2(x)) for x in l_ab]
    for _ in range(4):
        both = [_dot1(rows2(p_, t_), stack2(p_)) for p_, t_ in zip(pw, tinv)]
        tinv = [t_ + r_[c:] for t_, r_ in zip(tinv, both)]
        pw = [r_[:c] for r_ in both]
    tinv = [t_ + _dot1(t_, stack2(p_)) for t_, p_ in zip(tinv, pw)]
    for i, tl in enumerate(tiles):
        wu = _dot1(tinv[i], jnp.concatenate([stack2(a_t[i]), stack2(g_v[i])], axis=1))
        wt_ref[(0,) + tl] = wu[:, :LANES].astype(BF16)
        ut_ref[(0,) + tl] = wu[:, LANES:]


def _rw_prep(m, shift0, rwp):
    b, t, _ = m.shape
    tr = min(2 * CHUNK, t)
    row_spec = pl.BlockSpec((1, tr, WIDTH), lambda bi, i: (bi, i, 0))
    vec = lambda n: pl.BlockSpec((1, n), lambda bi, i: (0, 0))
    mat = lambda r: pl.BlockSpec((r, WIDTH), lambda bi, i: (0, 0))
    out = jax.ShapeDtypeStruct((b, t, WIDTH), F32)
    out_mxu = jax.ShapeDtypeStruct((b, t, WIDTH), BF16)
    return pl.pallas_call(
        functools.partial(_rw_prep_kernel, tr=tr),
        grid=(b, t // tr),
        in_specs=[pl.BlockSpec((1, tr, RW_PAD), lambda bi, i: (bi, i, 0)),
                  pl.BlockSpec((1, 1, RW_PAD), lambda bi, i: (bi, 0, 0)),
                  vec(RW_PAD), vec(WIDTH), mat(LANES), vec(WIDTH), mat(LANES), mat(RW_PAD - 3 * WIDTH - LANES),
                  vec(WIDTH), vec(WIDTH), vec(WIDTH)],
        out_specs=[row_spec] * 10 + [pl.BlockSpec((1, tr // CHUNK, 1, WIDTH), lambda bi, i: (bi, i, 0, 0))],
        out_shape=[out_mxu, out, out_mxu, out_mxu, out, out_mxu, out_mxu, out_mxu, out, out,
                   jax.ShapeDtypeStruct((b, t // CHUNK, 1, WIDTH), F32)],
        scratch_shapes=[pltpu.VMEM((1, RW_PAD), F32)],
        compiler_params=_cparams("arbitrary", "arbitrary"),
        name="rwkv_prep",
    )(m, shift0, rwp["mu"], rwp["w0"], rwp["w2"], rwp["a0"], rwp["a2"], rwp["g2"], rwp["k_k"], rwp["k_a"],
      rwp["r_k"])


def _rw_scan_kernel(wt_ref, ut_ref, rt_ref, mrb_ref, yv_ref, bt_ref, kt_ref, v_ref, bv_ref, g_ref, pend_ref,
                    s0_ref, lnw_ref, lnb_ref, o_ref, sout_ref, *s_refs, n_chunks, npair):
    ci = pl.program_id(2)

    @pl.when(ci == 0)
    def _():
        for pi in range(npair):
            s_refs[pi][...] = s0_ref[0, pi]

    c = CHUNK
    first = lax.broadcasted_iota(I32, (c, LANES), 1) < HEAD_DIM
    vi = lax.broadcasted_iota(I32, (LANES, LANES), 0)
    kj = lax.broadcasted_iota(I32, (LANES, LANES), 1)
    same_head_b = (vi // HEAD_DIM) == (kj // HEAD_DIM)
    same_head = same_head_b.astype(BF16)

    def stack2(x):
        return jnp.concatenate([jnp.where(first, x, 0.0), jnp.where(first, 0.0, x)], axis=0)

    def chunk(j, carry):
        rows = pl.ds(pl.multiple_of(j * c, c), c)
        p_all = pend_ref[0, j]
        pairs = range(npair)
        col = lambda pi: slice(pi * LANES, (pi + 1) * LANES)
        s = [s_refs[pi][...] for pi in pairs]
        u = [_dot1(wt_ref[0, rows, col(pi)], s[pi], NT) + ut_ref[0, rows, col(pi)] for pi in pairs]
        for pi in pairs:
            ds = _dot1(jnp.concatenate([u[pi].astype(BF16), v_ref[0, rows, col(pi)]], axis=0),
                       jnp.concatenate([bt_ref[0, rows, col(pi)], kt_ref[0, rows, col(pi)]], axis=0), TN)
            s_refs[pi][...] = (s[pi] + jnp.where(same_head_b, ds, 0.0)) * p_all[:, col(pi)]
        y = [_dot1(rt_ref[0, rows, col(pi)], s[pi], NT) + (_dot1(mrb_ref[0, rows, col(pi)], stack2(u[pi]))
                                                           + yv_ref[0, rows, col(pi)]) for pi in pairs]
        for pi in pairs:
            mu = _dot1(y[pi], same_head) * (1.0 / HEAD_DIM)
            dlt = y[pi] - mu
            var = _dot1(dlt * dlt, same_head) * (1.0 / HEAD_DIM)
            yn = dlt * lax.rsqrt(var + GN_EPS) * lnw_ref[:, col(pi)] + lnb_ref[:, col(pi)]
            o_ref[0, rows, col(pi)] = (yn + bv_ref[0, rows, col(pi)]) * g_ref[0, rows, col(pi)]
        return carry

    lax.fori_loop(0, n_chunks, chunk, 0)

    @pl.when(ci == pl.num_programs(2) - 1)
    def _():
        for pi in range(npair):
            sout_ref[0, pi] = s_refs[pi][...]


def _rw_scan(prep, s0_bd, lnw, lnb, npair=8):
    b, t, _ = prep[0].shape
    tc = min(512, t)
    n_chunks = tc // CHUNK
    ng = WIDTH // (npair * LANES)
    row_spec = pl.BlockSpec((1, tc, npair * LANES), lambda bi, h, i: (bi, i, h))
    pend_spec = pl.BlockSpec((1, n_chunks, 1, npair * LANES), lambda bi, h, i: (bi, i, 0, h))
    st_spec = pl.BlockSpec((1, npair, LANES, LANES), lambda bi, h, i: (bi, h, 0, 0))
    vec_spec = pl.BlockSpec((1, npair * LANES), lambda bi, h, i: (0, h))
    return pl.pallas_call(
        functools.partial(_rw_scan_kernel, n_chunks=n_chunks, npair=npair),
        grid=(b, ng, t // tc),
        in_specs=[row_spec] * 10 + [pend_spec, st_spec, vec_spec, vec_spec],
        out_specs=[row_spec, st_spec],
        out_shape=[jax.ShapeDtypeStruct((b, t, WIDTH), F32),
                   jax.ShapeDtypeStruct((b, WIDTH // LANES, LANES, LANES), F32)],
        scratch_shapes=[pltpu.VMEM((LANES, LANES), F32)] * npair,
        compiler_params=_cparams("arbitrary", "arbitrary", "arbitrary"),
        name="rwkv_scan",
    )(*prep, s0_bd, lnw, lnb)


def _merge_kernel(oa_ref, ob_ref, sg_ref, x_ref, g1_ref, sc2_ref, sh2_ref, pa_ref, pb_ref, wo_ref, lng_ref, lnb_ref,
                  wr_ref, br_ref, cnt0_ref, h_ref, u2_ref, route_ref, cnt_out_ref, cnt_ref):
    sg = sg_ref[0]
    ya = _mm(oa_ref[0].astype(BF16), pa_ref[...])
    yb = _mm(ob_ref[0].astype(BF16), pb_ref[...])
    merged = sg[:, :D_MODEL].astype(F32) * ya + sg[:, D_MODEL:].astype(F32) * yb
    mix = _mm(merged.astype(BF16), wo_ref[...])
    hin = ALPHA * x_ref[0] + g1_ref[0] * mix
    mu = jnp.mean(hin, axis=-1, keepdims=True)
    dlt = hin - mu
    var = jnp.mean(dlt * dlt, axis=-1, keepdims=True)
    h = dlt * lax.rsqrt(var + LN_EPS) * lng_ref[...] + lnb_ref[...]
    h_ref[0] = h
    u2 = h * (1.0 + sc2_ref[0]) + sh2_ref[0]
    u2_ref[0] = u2

    logits = _dot3(u2, wr_ref[...]) + br_ref[...]
    lane_i = lax.broadcasted_iota(I32, logits.shape, 1)
    lane = lane_i.astype(F32)
    neg = -jnp.inf
    first_at = lambda vals, top: jnp.min(jnp.where(vals == top, lane, float(LANES)), axis=1, keepdims=True)
    lg = jnp.where(lane_i < N_GROUPS, logits, neg)
    mg = jnp.max(lg, axis=1, keepdims=True)
    g_idx = first_at(lg, mg)
    p_g = 1.0 / jnp.sum(jnp.exp(lg - mg), axis=1, keepdims=True)
    lo_lane = N_GROUPS + PER_GROUP * g_idx
    le = jnp.where((lane >= lo_lane) & (lane < lo_lane + PER_GROUP), logits, neg)
    m1 = jnp.max(le, axis=1, keepdims=True)
    i1 = first_at(le, m1)
    le2 = jnp.where(lane == i1, neg, le)
    m2 = jnp.max(le2, axis=1, keepdims=True)
    i2 = first_at(le2, m2)
    ratio = jnp.exp(m2 - m1)
    w1 = p_g / (1.0 + ratio)
    w2 = p_g * ratio / (1.0 + ratio)
    @pl.when((pl.program_id(0) == 0) & (pl.program_id(1) == 0))
    def _():
        cnt_ref[...] = cnt0_ref[...]

    tm = logits.shape[0]
    chosen = ((lane == i1) | (lane == i2)).astype(BF16)
    earlier = (lax.broadcasted_iota(I32, (tm, tm), 1) < lax.broadcasted_iota(I32, (tm, tm), 0)).astype(BF16)
    before = _mm(earlier, chosen) + cnt_ref[...]
    rank1 = jnp.sum(jnp.where(lane == i1, before, 0.0), axis=1, keepdims=True)
    rank2 = jnp.sum(jnp.where(lane == i2, before, 0.0), axis=1, keepdims=True)
    cnt_ref[...] = before[tm - 1:tm, :] + chosen[tm - 1:tm, :].astype(F32)
    cnt_out_ref[...] = cnt_ref[...]

    route = jnp.where(lane_i == 0, i1 - N_GROUPS, 0.0)
    for k, val in enumerate((i2 - N_GROUPS, w1, w2, rank1, rank2), start=1):
        route = jnp.where(lane_i == k, val, route)
    route_ref[0] = route


def _merge(o_a, o_b, sg, x, g1, sc2, sh2, wts, cnt0):
    b, t, d = x.shape
    tm = min(256, t)
    const = lambda shape: pl.BlockSpec(shape, lambda bi, i: (0,) * len(shape), pipeline_mode=pl.Buffered(1))
    seq = pl.BlockSpec((1, 1, d), lambda bi, i: (bi, 0, 0))
    row = lambda n: pl.BlockSpec((1, tm, n), lambda bi, i: (bi, i, 0))
    return pl.pallas_call(
        _merge_kernel,
        grid=(b, t // tm),
        in_specs=[row(WIDTH), row(WIDTH), row(2 * d), row(d), seq, seq, seq,
                  const((WIDTH, d)), const((WIDTH, d)), const((d, d)), const((1, d)), const((1, d)),
                  const((d, LANES)), const((1, LANES)), const((1, LANES))],
        out_specs=[row(d), row(d), row(LANES), pl.BlockSpec((1, LANES), lambda bi, i: (0, 0))],
        out_shape=[jax.ShapeDtypeStruct((b, t, d), F32), jax.ShapeDtypeStruct((b, t, d), F32),
                   jax.ShapeDtypeStruct((b, t, LANES), F32), jax.ShapeDtypeStruct((1, LANES), F32)],
        scratch_shapes=[pltpu.VMEM((1, LANES), F32)],
        compiler_params=_cparams("arbitrary", "arbitrary"),
        name="merge_route",
    )(o_a, o_b, sg, x, g1, sc2, sh2, wts["p_a"], wts["p_b"], wts["w_out"], wts["ln1_g"], wts["ln1_b"],
      wts["w_route"], wts["b_route"], cnt0)


def _moe_kernel(be_ref, nused_ref, tok_ref, x_hbm, wg_ref, wu_ref, wd_ref, y_ref, xbuf, sem, *, tb):
    i = pl.program_id(0)
    n_used = nused_ref[0]
    last = n_used - 1

    def row_copy(block, r, slot):
        return pltpu.make_async_copy(x_hbm.at[pl.ds(tok_ref[block * tb + r], 1)], xbuf.at[slot, pl.ds(r, 1)],
                                     sem.at[slot])

    def wait_slot(slot):
        pltpu.make_async_copy(x_hbm.at[pl.ds(0, tb)], xbuf.at[slot], sem.at[slot]).wait()

    @pl.when((i == 0) & (n_used > 0))
    def _():
        def start(r, carry):
            row_copy(0, r, 0).start()
            return carry

        lax.fori_loop(0, tb, start, 0, unroll=8)

    @pl.when(i < n_used)
    def _():
        slot = i % 2
        wait_slot(slot)
        x = xbuf[slot].astype(BF16)
        nxt = jnp.minimum(i + 1, last)
        for r in range(tb):
            row_copy(nxt, r, 1 - slot).start()
        hg = _mm(x, wg_ref[0])
        hu = _mm(x, wu_ref[0])
        hid = (hg * _sigmoid(hg) * hu).astype(BF16)
        y_ref[...] = _mm(hid, wd_ref[0])

    @pl.when(i == last)
    def _():
        wait_slot(1 - i % 2)

    @pl.when(i >= n_used)
    def _():
        y_ref[...] = jnp.zeros_like(y_ref)


def _moe(block_expert, n_used, slot_token, x_all, wg, wu, wd, tb):
    n_blocks = block_expert.shape[0]
    d = x_all.shape[1]
    grid_spec = pltpu.PrefetchScalarGridSpec(
        num_scalar_prefetch=3,
        grid=(n_blocks,),
        in_specs=[pl.BlockSpec(memory_space=pl.ANY),
                  pl.BlockSpec((1, d, D_EXPERT), lambda i, be, nu, tok: (be[i], 0, 0), pipeline_mode=pl.Buffered(1)),
                  pl.BlockSpec((1, d, D_EXPERT), lambda i, be, nu, tok: (be[i], 0, 0), pipeline_mode=pl.Buffered(1)),
                  pl.BlockSpec((1, D_EXPERT, d), lambda i, be, nu, tok: (be[i], 0, 0), pipeline_mode=pl.Buffered(1))],
        out_specs=pl.BlockSpec((tb, d), lambda i, be, nu, tok: (i, 0)),
        scratch_shapes=[pltpu.VMEM((2, tb, d), F32), pltpu.SemaphoreType.DMA((2,))],
    )
    return pl.pallas_call(
        functools.partial(_moe_kernel, tb=tb),
        grid_spec=grid_spec,
        out_shape=jax.ShapeDtypeStruct((n_blocks * tb, d), F32),
        compiler_params=_cparams("arbitrary"),
        name="moe_experts",
    )(block_expert, n_used, slot_token, x_all, wg, wu, wd)


def _final_kernel(dest_ref, y_hbm, h_ref, rw_ref, g2_ref, lng_ref, lnb_ref, o_ref, ybuf, sem, *, tm, per_seq):
    step = pl.program_id(0) * per_seq + pl.program_id(1)
    n_steps = pl.num_programs(0) * per_seq

    def row_copy(blk, r, k, slot):
        return pltpu.make_async_copy(y_hbm.at[pl.ds(dest_ref[2 * (blk * tm + r) + k], 1)],
                                     ybuf.at[slot, k, pl.ds(r, 1)], sem.at[slot])

    def wait_slot(slot):
        for k in range(2):
            pltpu.make_async_copy(y_hbm.at[pl.ds(0, tm)], ybuf.at[slot, k], sem.at[slot]).wait()

    @pl.when(step == 0)
    def _():
        def start(r, carry):
            for k in range(2):
                row_copy(0, r, k, 0).start()
            return carry

        lax.fori_loop(0, tm, start, 0, unroll=4)

    slot = step % 2
    wait_slot(slot)
    nxt = jnp.minimum(step + 1, n_steps - 1)
    for r in range(tm):
        for k in range(2):
            row_copy(nxt, r, k, 1 - slot).start()
    rw = rw_ref[0]
    moe = rw[:, 2:3] * ybuf[slot, 0] + rw[:, 3:4] * ybuf[slot, 1]
    xin = ALPHA * h_ref[0] + g2_ref[0] * moe
    mu = jnp.mean(xin, axis=-1, keepdims=True)
    dlt = xin - mu
    var = jnp.mean(dlt * dlt, axis=-1, keepdims=True)
    o_ref[0] = dlt * lax.rsqrt(var + LN_EPS) * lng_ref[...] + lnb_ref[...]

    @pl.when(step == n_steps - 1)
    def _():
        wait_slot(1 - slot)


def _final(dest, y_sorted, h, route, g2, ln2_g, ln2_b):
    b, t, d = h.shape
    tm = min(256, t)
    per_seq = t // tm
    grid_spec = pltpu.PrefetchScalarGridSpec(
        num_scalar_prefetch=1,
        grid=(b, per_seq),
        in_specs=[pl.BlockSpec(memory_space=pl.ANY),
                  pl.BlockSpec((1, tm, d), lambda bi, i, de: (bi, i, 0)),
                  pl.BlockSpec((1, tm, LANES), lambda bi, i, de: (bi, i, 0)),
                  pl.BlockSpec((1, 1, d), lambda bi, i, de: (bi, 0, 0)),
                  pl.BlockSpec((1, d), lambda bi, i, de: (0, 0)),
                  pl.BlockSpec((1, d), lambda bi, i, de: (0, 0))],
        out_specs=pl.BlockSpec((1, tm, d), lambda bi, i, de: (bi, i, 0)),
        scratch_shapes=[pltpu.VMEM((2, 2, tm, d), F32), pltpu.SemaphoreType.DMA((2,))],
    )
    return pl.pallas_call(
        functools.partial(_final_kernel, tm=tm, per_seq=per_seq),
        grid_spec=grid_spec,
        out_shape=jax.ShapeDtypeStruct((b, t, d), F32),
        compiler_params=_cparams("arbitrary", "arbitrary"),
        name="combine_ln2",
    )(dest, y_sorted, h, route, g2, ln2_g, ln2_b)


def _pad_rows(a, rows):
    return jnp.pad(a, ((0, rows - a.shape[0]), (0, 0)))


def _mixers(x, ada, past_k, past_v, s0, shift0, wts, sb_tiles, cnt0):
    b, t, d = x.shape
    sh1, sc1, g1, sh2, sc2, g2 = [a[:, None, :] for a in jnp.split(ada, 6, axis=-1)]
    q = _modmm(x, sc1, sh1, wts["w_q"], 1024, name="proj_q")
    k = _modmm(x, sc1, sh1, wts["w_k"], 1024, name="proj_k")
    v = _modmm(x, sc1, sh1, wts["w_v"], 1024, name="proj_v")
    m = _modmm(x, sc1, sh1, wts["w_m"], RW_PAD // 3, name="proj_rwkv")
    sg = _modmm(x, sc1, sh1, wts["w_g"], 1024, out_dtype=BF16, sigmoid_out=True, name="proj_gates")

    tq, tk, nl = sb_tiles
    past = past_k.shape[1]
    o_a = _sb_attention(q, k, v, past_k.reshape(b, past, WIDTH), past_v.reshape(b, past, WIDTH), tq, tk, nl)

    shift0_p = jnp.pad(shift0, ((0, 0), (0, 0), (0, RW_PAD - RW_COLS)))
    prep = _rw_prep(m, shift0_p, wts["rw"])
    nhp = WIDTH // LANES
    s0_pairs = s0.reshape(b, nhp, 2, HEAD_DIM, HEAD_DIM)
    zero4 = ((0, 0), (0, 0))
    s0_bd = (jnp.pad(s0_pairs[:, :, 0], zero4 + ((0, HEAD_DIM), (0, HEAD_DIM)))
             + jnp.pad(s0_pairs[:, :, 1], zero4 + ((HEAD_DIM, 0), (HEAD_DIM, 0))))
    o_b, s_bd = _rw_scan(prep, s0_bd, wts["rw"]["lnx_w"], wts["rw"]["lnx_b"])
    s_new = jnp.stack([s_bd[:, :, :HEAD_DIM, :HEAD_DIM], s_bd[:, :, HEAD_DIM:, HEAD_DIM:]], axis=2)
    s_new = s_new.reshape(b, 2 * nhp, HEAD_DIM, HEAD_DIM)
    shift_new = m[:, t - 1:t, :RW_COLS]

    h, u2, route, cnt = _merge(o_a, o_b, sg, x, g1, sc2, sh2, wts, cnt0)
    return h, u2, route, cnt, g2, k, v, s_new, shift_new


def kernel(x_prompt, x_sample, cache_sb_k, cache_sb_v, state_rwkv, state_rwkv_shift, c_prompt, c_sample, w_ada, b_ada, w_in, tokshift_mu, rw_w0, rw_w2, rw_a0, rw_a2, rw_g2, rw_k_k, rw_k_a, rw_r_k, rw_lnx_w, rw_lnx_b, w_branch_a, w_branch_b, w_out, ln1_g, ln1_b, w_router_group, b_router_group, w_router_expert, b_router_expert, w_exp_gate, w_exp_up, w_exp_down, ln2_g, ln2_b):
    depth = w_ada.shape[0]
    assert depth == 1
    bp, tp, d = x_prompt.shape
    bs, ts, _ = x_sample.shape
    l = 0

    w_in_l = w_in[l]
    w_rw = w_in_l[:, 3 * WIDTH:3 * WIDTH + RW_COLS]
    row_vec = lambda a: a.reshape(1, -1)
    pad_cols = lambda a, n: jnp.pad(a, ((0, 0), (0, n - a.shape[1])))
    n_lg = RW_PAD - 3 * WIDTH - LANES
    wts = {
        "w_q": w_in_l[:, 0:WIDTH].astype(BF16),
        "w_k": w_in_l[:, WIDTH:2 * WIDTH].astype(BF16),
        "w_v": w_in_l[:, 2 * WIDTH:3 * WIDTH].astype(BF16),
        "w_m": pad_cols(w_rw, RW_PAD).astype(BF16),
        "w_g": w_in_l[:, 3 * WIDTH + RW_COLS:].astype(BF16),
        "p_a": w_branch_a[l].astype(BF16),
        "p_b": w_branch_b[l].astype(BF16),
        "w_out": w_out[l].astype(BF16),
        "ln1_g": row_vec(ln1_g[l]),
        "ln1_b": row_vec(ln1_b[l]),
        "w_route": pad_cols(jnp.concatenate([w_router_group[l], w_router_expert[l]], axis=1), LANES),
        "b_route": pad_cols(row_vec(jnp.concatenate([b_router_group[l], b_router_expert[l]])), LANES),
        "rw": {
            "mu": pad_cols(row_vec(tokshift_mu[l]), RW_PAD),
            "w0": row_vec(rw_w0[l]),
            "w2": _pad_rows(rw_w2[l], LANES),
            "a0": row_vec(rw_a0[l]),
            "a2": jnp.concatenate([jnp.zeros((LORA_W, WIDTH), F32), rw_a2[l]], axis=0),
            "g2": _pad_rows(rw_g2[l], n_lg),
            "k_k": row_vec(rw_k_k[l]),
            "k_a": row_vec(rw_k_a[l]),
            "r_k": row_vec(rw_r_k[l]),
            "lnx_w": row_vec(rw_lnx_w[l]),
            "lnx_b": row_vec(rw_lnx_b[l]),
        },
    }

    rows = -(-(bp + bs) // 8) * 8
    c_all = _pad_rows(jnp.concatenate([c_prompt, c_sample], axis=0), rows)
    ada = _ada(c_all, w_ada[l], b_ada[l])

    empty = jnp.zeros((bp, 0, WIDTH), F32)
    zero_state = jnp.zeros((bp, WIDTH // HEAD_DIM, HEAD_DIM, HEAD_DIM), F32)
    zero_shift = jnp.zeros((bp, 1, RW_COLS), F32)
    hp, u2p, route_p, cnt_p, g2p, kp, vp, sp_, shp = _mixers(
        x_prompt, ada[:bp], empty, empty, zero_state, zero_shift, wts, (min(256, tp), 256, 8),
        jnp.zeros((1, LANES), F32))
    hs, u2s, route_s, cnt_all, g2s, ks, vs, ss_, shs = _mixers(
        x_sample, ada[bp:bp + bs], cache_sb_k[l], cache_sb_v[l], state_rwkv[l], state_rwkv_shift[l], wts,
        (min(256, ts), 256, 8), cnt_p)

    n_p, n_s = bp * tp, bs * ts
    n_tok = n_p + n_s
    tb = 512
    route_all = jnp.concatenate([route_p.reshape(n_p, LANES), route_s.reshape(n_s, LANES)], axis=0)
    flat_e = route_all[:, 0:2].astype(I32).reshape(n_tok * 2)
    rank = route_all[:, 4:6].astype(I32).reshape(n_tok * 2)
    counts = cnt_all[0, N_GROUPS:N_GROUPS + N_EXPERTS].astype(I32)
    padded = (counts + tb - 1) // tb * tb
    pend = jnp.cumsum(padded)
    pstart = pend - padded
    dest = (pstart[flat_e] + rank).astype(I32)
    n_blocks = (2 * n_tok + N_EXPERTS * (tb - 1)) // tb + 1
    slot_token = jnp.zeros((n_blocks * tb,), I32).at[dest].set(
        jnp.arange(2 * n_tok, dtype=I32) // 2, unique_indices=True, mode="promise_in_bounds")
    block_start = jnp.arange(n_blocks, dtype=I32) * tb
    block_expert = jnp.minimum(jnp.sum((pend[None, :] <= block_start[:, None]).astype(I32), axis=1), N_EXPERTS - 1)
    n_used = (pend[-1] // tb).astype(I32).reshape(1)

    u2_all = jnp.concatenate([u2p.reshape(n_p, d), u2s.reshape(n_s, d)], axis=0)
    y_sorted = _moe(block_expert, n_used, slot_token, u2_all, w_exp_gate[l].astype(BF16),
                    w_exp_up[l].astype(BF16), w_exp_down[l].astype(BF16), tb)

    ln2g, ln2b = row_vec(ln2_g[l]), row_vec(ln2_b[l])
    y_prompt = _final(dest[:2 * n_p], y_sorted, hp, route_p, g2p, ln2g, ln2b)
    y_sample = _final(dest[2 * n_p:], y_sorted, hs, route_s, g2s, ln2g, ln2b)

    heads = lambda a: a.reshape(1, a.shape[0], a.shape[1], WIDTH // HEAD_DIM, HEAD_DIM)
    return (y_prompt, y_sample, heads(kp), heads(vp), sp_[None], shp[None],
            heads(ks), heads(vs), ss_[None], shs[None])
```
